```python
import jax, jax.numpy as jnp
from jax import lax
import numpy as np

D_MODEL = 1024
BATCH = 8
SEQ = 2048
DEPTH = 2

HEAD_DIM = 64
ROPE_THETA = 10000.0
Q_BLOCK = 128
A_HEADS = 8
A_KV_HEADS = 2
A_WINDOW = 128
B_HEADS = 4
IDX_HEADS = 4
IDX_DIM = 64
IDX_TOPK_MAX = 256
IDX_W_SCALE = (IDX_HEADS * IDX_DIM) ** -0.5
C_HEADS = 4
C_NOPE = 64
C_ROPE = 32
C_V = 64
C_Q_RANK = 256
C_KV_RANK = 128
A_Q = A_HEADS * HEAD_DIM
A_KV = A_KV_HEADS * HEAD_DIM
B_Q = B_HEADS * HEAD_DIM
B_KV = HEAD_DIM
IDX_Q = IDX_HEADS * IDX_DIM
IN_SPLITS = (A_Q, A_KV, A_KV, B_Q, B_KV, B_KV, IDX_Q, IDX_DIM, IDX_HEADS, C_Q_RANK, C_KV_RANK, C_ROPE)
IN_COLS = A_Q + 2 * A_KV + B_Q + 2 * B_KV + IDX_Q + IDX_DIM + IDX_HEADS + C_Q_RANK + C_KV_RANK + C_ROPE
MIX_WIDTH = A_HEADS * HEAD_DIM + B_HEADS * HEAD_DIM + C_HEADS * C_V
N_EXPERTS = 16
N_GROUPS = 4
EXPERTS_PER_GROUP = N_EXPERTS // N_GROUPS
TOP_K = 2
D_EXPERT = 256
DEEPNORM_ALPHA = (2 * DEPTH) ** 0.25
DEEPNORM_BETA = (8 * DEPTH) ** -0.25
LN_EPS = 1e-5
RMS_EPS = 1e-6

kernel_name = 'hybrid_swa_dsa_mla_grouped_moe_deepnorm'


def rope_tables(pos, dim):
    inv = 1.0 / (ROPE_THETA ** (jnp.arange(0, dim, 2, dtype=jnp.float32) / dim))
    ang = pos.astype(jnp.float32)[:, None] * inv[None, :]
    return jnp.cos(ang), jnp.sin(ang)


def apply_rope(x, cos, sin):
    d2 = x.shape[-1] // 2
    shape = (1, cos.shape[0]) + (1,) * (x.ndim - 3) + (d2,)
    c = cos.reshape(shape)
    s = sin.reshape(shape)
    xf = x.astype(jnp.float32)
    x1, x2 = xf[..., :d2], xf[..., d2:]
    return jnp.concatenate([x1 * c - x2 * s, x2 * c + x1 * s], axis=-1).astype(x.dtype)


def layer_norm(x, g, b):
    xf = x.astype(jnp.float32)
    mu = jnp.mean(xf, axis=-1, keepdims=True)
    var = jnp.mean(jnp.square(xf - mu), axis=-1, keepdims=True)
    return ((xf - mu) * lax.rsqrt(var + LN_EPS) * g.astype(jnp.float32) + b.astype(jnp.float32)).astype(x.dtype)


def rms_norm(x, g):
    xf = x.astype(jnp.float32)
    return (xf * lax.rsqrt(jnp.mean(jnp.square(xf), axis=-1, keepdims=True) + RMS_EPS) * g.astype(jnp.float32)).astype(x.dtype)


def split_cols(h):
    offs = np.cumsum(IN_SPLITS)[:-1].tolist()
    return jnp.split(h, offs, axis=-1)


def to_blocks(t):
    b, s = t.shape[0], t.shape[1]
    return jnp.moveaxis(t.reshape((b, s // Q_BLOCK, Q_BLOCK) + t.shape[2:]), 1, 0)


def sliding_window_sink_attn(q, k, v, sinks):
    B, S, Hq, D = q.shape
    Hkv = k.shape[2]
    G = Hq // Hkv
    W = A_WINDOW
    nb = S // W
    qb = q.reshape(B, nb, W, Hkv, G, D)

    def band(t):
        tb = t.reshape(B, nb, W, Hkv, D)
        prev = jnp.pad(tb[:, :-1], ((0, 0), (1, 0), (0, 0), (0, 0), (0, 0)))
        return jnp.concatenate([prev, tb], axis=2)

    kb, vb = band(k), band(v)
    s = jnp.einsum('bnqhgd,bnkhd->bnhgqk', qb, kb).astype(jnp.float32) * (D ** -0.5)
    qi = jnp.arange(W)[:, None]
    kj = jnp.arange(2 * W)[None, :]
    diff = qi + W - kj
    in_window = (diff >= 0) & (diff < W)
    not_pad = (jnp.arange(nb)[:, None, None] > 0) | (kj[None] >= W)
    mask = (in_window[None] & not_pad)[None, :, None, None]
    s = jnp.where(mask, s, -jnp.inf)
    sink = jnp.broadcast_to(sinks.astype(jnp.float32).reshape(1, 1, Hkv, G, 1, 1), s.shape[:-1] + (1,))
    p = jax.nn.softmax(jnp.concatenate([s, sink], axis=-1), axis=-1)[..., :-1]
    o = jnp.einsum('bnhgqk,bnkhd->bnqhgd', p.astype(v.dtype), vb)
    return o.reshape(B, S, Hq * D)


def indexer_sparse_attn(q, k, v, q_idx, k_idx, w_idx):
    B, S, H, D = q.shape
    topk = min(IDX_TOPK_MAX, S // 4)
    nb = S // Q_BLOCK
    key_pos = jnp.arange(S)
    bidx = jnp.arange(B)[:, None]

    def block_fn(args):
        qb, qib, wb, t0 = args
        tpos = t0 + jnp.arange(Q_BLOCK)
        logits = jnp.einsum('bthd,bsd->bths', qib, k_idx).astype(jnp.float32)
        score = jnp.einsum('bth,bths->bts', wb.astype(jnp.float32), jax.nn.relu(logits))
        causal = key_pos[None, :] <= tpos[:, None]
        score = jnp.where(causal[None], score, -jnp.inf)
        _, idx = lax.top_k(score, topk)
        flat = idx.reshape(B, Q_BLOCK * topk)
        kg = k[bidx, flat].reshape(B, Q_BLOCK, topk, D)
        vg = v[bidx, flat].reshape(B, Q_BLOCK, topk, D)
        valid = idx <= tpos[None, :, None]
        s = jnp.einsum('bthd,btkd->bthk', qb, kg).astype(jnp.float32) * (D ** -0.5)
        s = jnp.where(valid[:, :, None, :], s, -jnp.inf)
        p = jax.nn.softmax(s, axis=-1)
        return jnp.einsum('bthk,btkd->bthd', p.astype(vg.dtype), vg)

    t0s = jnp.arange(nb, dtype=jnp.int32) * Q_BLOCK
    out = lax.map(block_fn, (to_blocks(q), to_blocks(q_idx), to_blocks(w_idx), t0s))
    return jnp.moveaxis(out, 0, 1).reshape(B, S, H * D)


def mla_attn(q_nope, q_rope, k_nope, k_rope, v):
    B, S, H, Dv = v.shape
    scale = (q_nope.shape[-1] + q_rope.shape[-1]) ** -0.5
    nb = S // Q_BLOCK
    key_pos = jnp.arange(S)

    def block_fn(args):
        qn, qr, t0 = args
        tpos = t0 + jnp.arange(Q_BLOCK)
        s = (jnp.einsum('bthd,bshd->bhts', qn, k_nope) + jnp.einsum('bthd,bsd->bhts', qr, k_rope)).astype(jnp.float32) * scale
        causal = key_pos[None, :] <= tpos[:, None]
        s = jnp.where(causal[None, None], s, -jnp.inf)
        p = jax.nn.softmax(s, axis=-1)
        return jnp.einsum('bhts,bshd->bthd', p.astype(v.dtype), v)

    t0s = jnp.arange(nb, dtype=jnp.int32) * Q_BLOCK
    out = lax.map(block_fn, (to_blocks(q_nope), to_blocks(q_rope), t0s))
    return jnp.moveaxis(out, 0, 1).reshape(B, S, H * Dv)


def moe_ffn(x, w_router, router_bias, w_gate, w_up, w_down):
    B, S, D = x.shape
    N = B * S
    xt = x.reshape(N, D)
    scores = jax.nn.sigmoid(jnp.dot(xt, w_router).astype(jnp.float32))
    biased = (scores + router_bias.astype(jnp.float32)).reshape(N, N_GROUPS, EXPERTS_PER_GROUP)
    group_score = jnp.sum(lax.top_k(biased, TOP_K)[0], axis=-1)
    g_sel = jnp.argmax(group_score, axis=-1)
    in_group = jnp.take_along_axis(biased, g_sel[:, None, None], axis=1)[:, 0]
    _, local = lax.top_k(in_group, TOP_K)
    expert_idx = g_sel[:, None] * EXPERTS_PER_GROUP + local
    w = jnp.take_along_axis(scores, expert_idx, axis=1)
    w = w / jnp.sum(w, axis=-1, keepdims=True)
    gates = jnp.sum(jax.nn.one_hot(expert_idx, N_EXPERTS, dtype=jnp.float32) * w[..., None], axis=1)
    h = jax.nn.silu(jnp.einsum('nd,edf->enf', xt, w_gate)) * jnp.einsum('nd,edf->enf', xt, w_up)
    h = h * gates.T.astype(h.dtype)[:, :, None]
    y = jnp.einsum('enf,efd->nd', h, w_down)
    return y.reshape(B, S, D)


def setup_inputs(seed: int = 0) -> dict:
    key = jax.random.key(seed)
    ks = jax.random.split(key, 20)
    f32 = jnp.float32
    nrm = lambda k, shape, scale: jax.random.normal(k, shape, f32) * scale
    return {
        'x': nrm(ks[0], (BATCH, SEQ, D_MODEL), 1.0),
        'w_in': nrm(ks[1], (DEPTH, D_MODEL, IN_COLS), D_MODEL ** -0.5),
        'attn_sinks': nrm(ks[2], (DEPTH, A_HEADS), 0.5),
        'c_q_norm_g': 1.0 + nrm(ks[3], (DEPTH, C_Q_RANK), 0.02),
        'c_kv_norm_g': 1.0 + nrm(ks[4], (DEPTH, C_KV_RANK), 0.02),
        'w_uq': nrm(ks[5], (DEPTH, C_Q_RANK, C_HEADS * (C_NOPE + C_ROPE)), C_Q_RANK ** -0.5),
        'w_ukv': nrm(ks[6], (DEPTH, C_KV_RANK, C_HEADS * (C_NOPE + C_V)), C_KV_RANK ** -0.5),
        'w_out': nrm(ks[7], (DEPTH, MIX_WIDTH, D_MODEL), DEEPNORM_BETA * MIX_WIDTH ** -0.5),
        'ln1_g': 1.0 + nrm(ks[8], (DEPTH, D_MODEL), 0.02),
        'ln1_b': nrm(ks[9], (DEPTH, D_MODEL), 0.02),
        'w_router': nrm(ks[10], (D_MODEL, N_EXPERTS), D_MODEL ** -0.5),
        'router_bias': nrm(ks[11], (N_EXPERTS,), 0.01),
        'w_gate': nrm(ks[12], (DEPTH, N_EXPERTS, D_MODEL, D_EXPERT), D_MODEL ** -0.5),
        'w_up': nrm(ks[13], (DEPTH, N_EXPERTS, D_MODEL, D_EXPERT), D_MODEL ** -0.5),
        'w_down': nrm(ks[14], (DEPTH, N_EXPERTS, D_EXPERT, D_MODEL), DEEPNORM_BETA * D_EXPERT ** -0.5),
        'ln2_g': 1.0 + nrm(ks[15], (DEPTH, D_MODEL), 0.02),
        'ln2_b': nrm(ks[16], (DEPTH, D_MODEL), 0.02),
    }


def reference(x, w_in, attn_sinks, c_q_norm_g, c_kv_norm_g, w_uq, w_ukv, w_out, ln1_g, ln1_b, w_router, router_bias, w_gate, w_up, w_down, ln2_g, ln2_b):
    B, S, _ = x.shape
    pos = jnp.arange(S)
    cos_h, sin_h = rope_tables(pos, HEAD_DIM)
    cos_r, sin_r = rope_tables(pos, C_ROPE)
    for l in range(DEPTH):
        h = jnp.einsum('bsd,dc->bsc', x, w_in[l])
        a_q, a_k, a_v, b_q, b_k, b_v, i_q, i_k, i_w, c_q, c_kv, c_kr = split_cols(h)
        o_a = sliding_window_sink_attn(
            apply_rope(a_q.reshape(B, S, A_HEADS, HEAD_DIM), cos_h, sin_h),
            apply_rope(a_k.reshape(B, S, A_KV_HEADS, HEAD_DIM), cos_h, sin_h),
            a_v.reshape(B, S, A_KV_HEADS, HEAD_DIM), attn_sinks[l])
        o_b = indexer_sparse_attn(
            apply_rope(b_q.reshape(B, S, B_HEADS, HEAD_DIM), cos_h, sin_h),
            apply_rope(b_k, cos_h, sin_h), b_v,
            apply_rope(i_q.reshape(B, S, IDX_HEADS, IDX_DIM), cos_h, sin_h),
            apply_rope(i_k, cos_h, sin_h), i_w * IDX_W_SCALE)
        q_c = jnp.einsum('bsr,rc->bsc', rms_norm(c_q, c_q_norm_g[l]), w_uq[l]).reshape(B, S, C_HEADS, C_NOPE + C_ROPE)
        kv_c = jnp.einsum('bsr,rc->bsc', rms_norm(c_kv, c_kv_norm_g[l]), w_ukv[l]).reshape(B, S, C_HEADS, C_NOPE + C_V)
        o_c = mla_attn(q_c[..., :C_NOPE], apply_rope(q_c[..., C_NOPE:], cos_r, sin_r),
                       kv_c[..., :C_NOPE], apply_rope(c_kr, cos_r, sin_r), kv_c[..., C_NOPE:])
        mix = jnp.concatenate([o_a, o_b, o_c], axis=-1)
        x = layer_norm(DEEPNORM_ALPHA * x + jnp.einsum('bsm,md->bsd', mix, w_out[l]), ln1_g[l], ln1_b[l])
        x = layer_norm(DEEPNORM_ALPHA * x + moe_ffn(x, w_router, router_bias, w_gate[l], w_up[l], w_down[l]), ln2_g[l], ln2_b[l])
    return x
```

```python
import functools
import math

import jax
import jax.numpy as jnp
import numpy as np
from jax import lax
from jax.experimental import pallas as pl
from jax.experimental.pallas import tpu as pltpu

HEAD_DIM = 64
ROPE_THETA = 10000.0
A_HEADS = 8
A_KV_HEADS = 2
A_WINDOW = 128
B_HEADS = 4
IDX_HEADS = 4
IDX_DIM = 64
IDX_TOPK_MAX = 256
C_HEADS = 4
C_NOPE = 64
C_ROPE = 32
C_V = 64
C_Q_RANK = 256
C_KV_RANK = 128
N_EXPERTS = 16
N_GROUPS = 4
EXPERTS_PER_GROUP = 4
D_EXPERT = 256
LN_EPS = 1e-5
RMS_EPS = 1e-6

LANES = 128
NEG_BIG = -1e30
INT_MIN = -(2 ** 31)
VMEM_LIMIT = 56 * 1024 * 1024

BF16 = jnp.bfloat16
F32 = jnp.float32

_NT = (((1,), (1,)), ((), ()))


def _dot(a, b):
    return jnp.dot(a, b, preferred_element_type=F32)


def _dot_nt(a, b):
    return lax.dot_general(a, b, _NT, preferred_element_type=F32)


N_ROPE_UNITS = 10


def _inproj_kernel(x_ref, w_ref, ch_ref, s1h_ref, s2h_ref, cr_ref, s1r_ref, s2r_ref, gq_ref, gkv_ref,
                   wuq_ref, wukv_ref,
                   aq_ref, ak_ref, bq_ref, bkik_ref, iq_ref, av_ref, bv_ref, iw_ref, qc_ref, kc_ref, vc_ref,
                   *, c_scale):
    xb = x_ref[...].astype(BF16)
    tm = xb.shape[0]
    lane = lax.broadcasted_iota(jnp.int32, (tm, LANES), 1)
    ch, s1h, s2h = ch_ref[...], s1h_ref[...], s2h_ref[...]
    cr, s1r, s2r = cr_ref[...], s1r_ref[...], s2r_ref[...]

    def rope_h(u):
        return u * ch + pltpu.roll(u, 96, 1) * s1h + pltpu.roll(u, 32, 1) * s2h

    def rope_r(u):
        return u * cr + pltpu.roll(u, 112, 1) * s1r + pltpu.roll(u, 16, 1) * s2r

    rope_dst = [(aq_ref, 0), (aq_ref, 1), (aq_ref, 2), (aq_ref, 3), (ak_ref, 0),
                (bq_ref, 0), (bq_ref, 1), (bkik_ref, 0), (iq_ref, 0), (iq_ref, 1)]
    for g in range(N_ROPE_UNITS // 2):
        hg = _dot(xb, w_ref[:, g * 256:(g + 1) * 256])
        for half in range(2):
            ref, k = rope_dst[2 * g + half]
            ref[:, k * LANES:(k + 1) * LANES] = rope_h(hg[:, half * LANES:(half + 1) * LANES]).astype(BF16)

    hv = _dot(xb, w_ref[:, 1280:1536])
    for k in range(2):
        u = hv[:, k * LANES:(k + 1) * LANES]
        av_ref[:, k * LANES:(k + 1) * LANES] = jnp.where(lane == 64, 1.0, u).astype(BF16)

    hb = _dot(xb, w_ref[:, 1536:1664])
    iw_ref[...] = hb
    bv_ref[...] = jnp.where(lane < 64, hb, jnp.where(lane == 64, 1.0, 0.0)).astype(BF16)

    cq = _dot(xb, w_ref[:, 1664:1920])
    cqn = cq * lax.rsqrt(jnp.mean(cq * cq, axis=1, keepdims=True) + RMS_EPS) * gq_ref[...]
    qc = _dot(cqn.astype(BF16), wuq_ref[...])
    for h in range(C_HEADS):
        u = qc[:, h * LANES:(h + 1) * LANES]
        qc_ref[:, h * LANES:(h + 1) * LANES] = (rope_r(u) * c_scale).astype(BF16)

    ckv = _dot(xb, w_ref[:, 1920:2176])
    ckv_lat = ckv[:, :LANES]
    kr = rope_r(ckv[:, LANES:])
    ckvn = ckv_lat * lax.rsqrt(jnp.mean(ckv_lat * ckv_lat, axis=1, keepdims=True) + RMS_EPS) * gkv_ref[...]
    kv = _dot(ckvn.astype(BF16), wukv_ref[...])
    for h in range(C_HEADS):
        kc_ref[:, h * LANES:(h + 1) * LANES] = (kv[:, h * LANES:(h + 1) * LANES] + kr).astype(BF16)
        v = kv[:, (C_HEADS + h) * LANES:(C_HEADS + h + 1) * LANES]
        vc_ref[:, h * LANES:(h + 1) * LANES] = jnp.where(lane == 64, 1.0, v).astype(BF16)


def _pack_w_in(w):
    d = w.shape[0]
    z = lambda n: jnp.zeros((d, n), F32)
    o = np.cumsum([0, 512, 128, 128, 256, 64, 64, 256, 64, 4, 256, 128, 32]).tolist()
    a_q, a_k, a_v, b_q, b_k, b_v, i_q, i_k, i_w, c_q, c_kv, c_kr = [w[:, o[j]:o[j + 1]] for j in range(12)]
    qs = HEAD_DIM ** -0.5
    ws = (IDX_HEADS * IDX_DIM) ** -0.5
    cols = [a_q * qs, a_k, b_q * qs, b_k, i_k, i_q,
            a_v[:, :64], z(64), a_v[:, 64:], z(64),
            b_v, i_w * ws, z(60),
            c_q, c_kv, z(64), c_kr, z(32)]
    return jnp.concatenate(cols, axis=1).astype(BF16)


def _pack_mla_w(w_uq, w_ukv):
    r = w_uq.shape[0]
    q = w_uq.reshape(r, C_HEADS, C_NOPE + C_ROPE)
    q = jnp.pad(q, ((0, 0), (0, 0), (0, LANES - C_NOPE - C_ROPE))).reshape(r, C_HEADS * LANES)
    r2 = w_ukv.shape[0]
    kv = w_ukv.reshape(r2, C_HEADS, C_NOPE + C_V)
    k = jnp.pad(kv[:, :, :C_NOPE], ((0, 0), (0, 0), (0, LANES - C_NOPE))).reshape(r2, C_HEADS * LANES)
    v = jnp.pad(kv[:, :, C_NOPE:], ((0, 0), (0, 0), (0, LANES - C_V))).reshape(r2, C_HEADS * LANES)
    return q.astype(BF16), jnp.concatenate([k, v], axis=1).astype(BF16)


def _rope_tables(s):
    pos = jnp.arange(s, dtype=F32)[:, None]
    lane = np.arange(LANES)
    inv_h = 1.0 / (ROPE_THETA ** (jnp.arange(0, HEAD_DIM, 2, dtype=F32) / HEAD_DIM))
    ang = pos * inv_h[None, :]
    cos, sin = jnp.cos(ang), jnp.sin(ang)
    j = lane % 32
    lo = jnp.asarray((lane % 64) < 32)
    ch = cos[:, j]
    s1h = jnp.where(lo, -sin[:, j], 0.0)
    s2h = jnp.where(lo, 0.0, sin[:, j])
    inv_r = 1.0 / (ROPE_THETA ** (jnp.arange(0, C_ROPE, 2, dtype=F32) / C_ROPE))
    angr = pos * inv_r[None, :]
    cosr, sinr = jnp.cos(angr), jnp.sin(angr)
    jr = lane % 16
    in_rope = jnp.asarray((lane >= 64) & (lane < 96))
    first = jnp.asarray((lane >= 64) & (lane < 80))
    second = jnp.asarray((lane >= 80) & (lane < 96))
    cr = jnp.where(in_rope, cosr[:, jr], 1.0)
    s1r = jnp.where(first, -sinr[:, jr], 0.0)
    s2r = jnp.where(second, sinr[:, jr], 0.0)
    return ch, s1h, s2h, cr, s1r, s2r


def _inproj(x2, w_p, tables, gq, gkv, wuq_p, wukv_p, s, tm):
    n, d = x2.shape
    nt = n // tm
    spt = s // tm
    tab_spec = pl.BlockSpec((tm, LANES), lambda t: (t % spt, 0))
    full = lambda a: pl.BlockSpec(a.shape, lambda t: (0,) * a.ndim)
    row = lambda c: pl.BlockSpec((tm, c), lambda t: (t, 0))
    out_cols = [512, 128, 256, 128, 256, 256, 128, 128, 512, 512, 512]
    out_dt = [BF16] * 7 + [F32] + [BF16] * 3
    return pl.pallas_call(
        functools.partial(_inproj_kernel, c_scale=(C_NOPE + C_ROPE) ** -0.5),
        grid=(nt,),
        in_specs=[row(d), full(w_p)] + [tab_spec] * 6 + [full(gq), full(gkv), full(wuq_p), full(wukv_p)],
        out_specs=[row(c) for c in out_cols],
        out_shape=[jax.ShapeDtypeStruct((n, c), dt) for c, dt in zip(out_cols, out_dt)],
        compiler_params=pltpu.CompilerParams(dimension_semantics=("parallel",), vmem_limit_bytes=VMEM_LIMIT),
    )(x2, w_p, *tables, gq, gkv, wuq_p, wukv_p)


def _swa_kernel(sink_ref, q_ref, kc_ref, kp_ref, vc_ref, vp_ref, o_ref):
    i = pl.program_id(1)
    w = A_WINDOW
    qi = lax.broadcasted_iota(jnp.int32, (w, 2 * w), 0)
    kj = lax.broadcasted_iota(jnp.int32, (w, 2 * w), 1)
    diff = qi + w - kj
    ok = (diff >= 0) & (diff < w) & ((kj >= w) | (i > 0))
    bias = jnp.where(ok, 0.0, NEG_BIG)
    group = A_HEADS // A_KV_HEADS
    for g in range(A_KV_HEADS):
        k2 = jnp.concatenate([kp_ref[:, g * 64:(g + 1) * 64], kc_ref[:, g * 64:(g + 1) * 64]], axis=0)
        v2 = jnp.concatenate([vp_ref[:, g * LANES:(g + 1) * LANES], vc_ref[:, g * LANES:(g + 1) * LANES]], axis=0)
        for hh in range(group):
            h = g * group + hh
            s = _dot_nt(q_ref[:, h * 64:(h + 1) * 64], k2) + bias
            sink = sink_ref[h]
            m = jnp.maximum(jnp.max(s, axis=1, keepdims=True), sink)
            p = jnp.exp(s - m)
            o = _dot(p.astype(BF16), v2)
            denom = o[:, 64:65] + jnp.exp(sink - m)
            o_ref[:, h * 64:(h + 1) * 64] = (o[:, :64] / denom).astype(BF16)


def _swa(sinks, aq, ak, av, b, s):
    n = aq.shape[0]
    w = A_WINDOW
    nb = s // w
    cur = lambda bb, i: (bb * nb + i, 0)
    prev = lambda bb, i: (bb * nb + jnp.maximum(i - 1, 0), 0)
    return pl.pallas_call(
        _swa_kernel,
        grid=(b, nb),
        in_specs=[pl.BlockSpec(memory_space=pltpu.SMEM),
                  pl.BlockSpec((w, 512), cur),
                  pl.BlockSpec((w, 128), cur), pl.BlockSpec((w, 128), prev),
                  pl.BlockSpec((w, 256), cur), pl.BlockSpec((w, 256), prev)],
        out_specs=pl.BlockSpec((w, 512), cur),
        out_shape=jax.ShapeDtypeStruct((n, 512), BF16),
        compiler_params=pltpu.CompilerParams(dimension_semantics=("parallel", "parallel"),
                                             vmem_limit_bytes=VMEM_LIMIT),
    )(sinks, aq, ak, ak, av, av)


DSA_QB = 256
DSA_RB = 128


def _dsa_kernel(bq_ref, iq_ref, iw_ref, kk_ref, v_ref, o_ref, sc_ref, tau_ref, j_ref, m_ref, acc_ref,
                *, topk, idx_bits):
    i = pl.program_id(1)
    qb = DSA_QB
    row = lax.broadcasted_iota(jnp.int32, (qb, qb), 0)
    col = lax.broadcasted_iota(jnp.int32, (qb, qb), 1)
    n_chunks = i + 1
    neg_inf = float("-inf")

    iq = iq_ref[...]
    iq_st = jnp.concatenate([iq[:, h * 64:(h + 1) * 64] for h in range(IDX_HEADS)], axis=0)
    iw = iw_ref[...]

    def idx_body(c, carry):
        start = pl.multiple_of(c * qb, qb)
        ik = kk_ref[pl.ds(start, qb), 64:128]
        lg = _dot_nt(iq_st, ik)
        sc = jnp.zeros((qb, qb), F32)
        for h in range(IDX_HEADS):
            sc = sc + iw[:, 64 + h:65 + h] * jnp.maximum(lg[h * qb:(h + 1) * qb], 0.0)
        sc_ref[c] = jnp.where(c * qb + col <= i * qb + row, sc, neg_inf)
        return carry

    lax.fori_loop(0, n_chunks, idx_body, 0)

    rb = DSA_RB
    colr = lax.broadcasted_iota(jnp.int32, (rb, qb), 1)
    for r in range(qb // rb):
        rows = slice(r * rb, (r + 1) * rb)
        tpos = i * qb + r * rb + lax.broadcasted_iota(jnp.int32, (rb, 1), 0)

        def count(pred_fn):
            def cbody(c, acc):
                k = sc_ref[c, rows, :]
                hit = jnp.where(pred_fn(k, c), 1.0, 0.0)
                return acc + hit[:, :LANES] + hit[:, LANES:]
            acc = lax.fori_loop(0, n_chunks, cbody, jnp.zeros((rb, LANES), F32))
            return jnp.sum(acc, axis=1, keepdims=True)

        def key_to_f32(key_u):
            key = key_u ^ INT_MIN
            return pltpu.bitcast(key ^ ((key >> 31) & 0x7FFFFFFF), F32)

        def bit_body(step, prefix):
            cand_u = prefix | lax.shift_left(jnp.int32(1), 31 - step)
            cand = key_to_f32(cand_u)
            cnt = count(lambda k, c: k >= cand)
            return jnp.where(cnt >= topk, cand_u, prefix)

        prefix = lax.fori_loop(0, 32, bit_body, jnp.zeros((rb, 1), jnp.int32))
        tau = key_to_f32(prefix)
        tau = jnp.where(tau != tau, neg_inf, tau)
        c_gt = count(lambda k, c: k > tau)
        c_eq = count(lambda k, c: k == tau)
        need = topk - c_gt

        def tie_search(_):
            def jbody(step, q):
                cand = q | lax.shift_left(jnp.int32(1), idx_bits - 1 - step)
                cnt = count(lambda k, c: (k == tau) & (c * qb + colr < cand))
                return jnp.where(cnt < need, cand, q)
            return lax.fori_loop(0, idx_bits, jbody, jnp.zeros((rb, 1), jnp.int32))

        any_split = jnp.max(c_eq - need) > 0.0
        jcut = lax.cond(any_split, tie_search, lambda _: jnp.full((rb, 1), 2 ** idx_bits, jnp.int32), 0)
        few = tpos < int(topk)
        tau_ref[rows, :] = jnp.where(few, neg_inf, tau)
        j_ref[rows, :] = jnp.where(few, tpos, jcut)

    tau = tau_ref[...]
    jcut = j_ref[...]
    m_ref[...] = jnp.full(m_ref.shape, NEG_BIG, F32)
    acc_ref[...] = jnp.zeros(acc_ref.shape, F32)

    def att_body(c, carry):
        start = pl.multiple_of(c * qb, qb)
        k = sc_ref[c]
        sel =(k > tau) | ((k == tau) & (c * qb + col <= jcut))
        bias = jnp.where(sel, 0.0, NEG_BIG)
        kc = kk_ref[pl.ds(start, qb), 0:64]
        vc = v_ref[pl.ds(start, qb), :]
        for h in range(B_HEADS):
            s = _dot_nt(bq_ref[:, h * 64:(h + 1) * 64], kc) + bias
            m_old = m_ref[h]
            m_new = jnp.maximum(m_old, jnp.max(s, axis=1, keepdims=True))
            p = jnp.exp(s - m_new)
            acc_ref[h] = acc_ref[h] * jnp.exp(m_old - m_new) + _dot(p.astype(BF16), vc)
            m_ref[h] = m_new
        return carry

    lax.fori_loop(0, n_chunks, att_body, 0)
    for h in range(B_HEADS):
        a = acc_ref[h]
        o_ref[:, h * 64:(h + 1) * 64] = (a[:, :64] / a[:, 64:65]).astype(BF16)


def _dsa(bq, iq, iw, bkik, bv, b, s):
    n = bq.shape[0]
    qb = DSA_QB
    nb = s // qb
    topk = min(IDX_TOPK_MAX, s // 4)
    idx_bits = max(1, int(math.ceil(math.log2(s))))
    blk = lambda bb, i: (bb * nb + i, 0)
    seq = lambda bb, i: (bb, 0)
    return pl.pallas_call(
        functools.partial(_dsa_kernel, topk=float(topk), idx_bits=idx_bits),
        grid=(b, nb),
        in_specs=[pl.BlockSpec((qb, 256), blk), pl.BlockSpec((qb, 256), blk), pl.BlockSpec((qb, 128), blk),
                  pl.BlockSpec((s, 128), seq), pl.BlockSpec((s, 128), seq)],
        out_specs=pl.BlockSpec((qb, 256), blk),
        out_shape=jax.ShapeDtypeStruct((n, 256), BF16),
        scratch_shapes=[pltpu.VMEM((nb, qb, qb), F32),
                        pltpu.VMEM((qb, 1), F32), pltpu.VMEM((qb, 1), jnp.int32),
                        pltpu.VMEM((B_HEADS, qb, 1), F32), pltpu.VMEM((B_HEADS, qb, LANES), F32)],
        compiler_params=pltpu.CompilerParams(dimension_semantics=("parallel", "arbitrary"),
                                             vmem_limit_bytes=VMEM_LIMIT),
    )(bq, iq, iw, bkik, bv)


MLA_QB = 256


def _mla_kernel(q_ref, k_ref, v_ref, o_ref, m_ref, acc_ref):
    i = pl.program_id(1)
    qb = MLA_QB
    row = lax.broadcasted_iota(jnp.int32, (qb, qb), 0)
    col = lax.broadcasted_iota(jnp.int32, (qb, qb), 1)
    m_ref[...] = jnp.full(m_ref.shape, NEG_BIG, F32)
    acc_ref[...] = jnp.zeros(acc_ref.shape, F32)

    def chunk(c, bias):
        start = pl.multiple_of(c * qb, qb)
        for h in range(C_HEADS):
            hs = slice(h * LANES, (h + 1) * LANES)
            s = _dot_nt(q_ref[:, hs], k_ref[pl.ds(start, qb), hs])
            if bias is not None:
                s = s + bias
            m_old = m_ref[h]
            m_new = jnp.maximum(m_old, jnp.max(s, axis=1, keepdims=True))
            p = jnp.exp(s - m_new)
            acc_ref[h] = acc_ref[h] * jnp.exp(m_old - m_new) + _dot(p.astype(BF16), v_ref[pl.ds(start, qb), hs])
            m_ref[h] = m_new

    def body(c, carry):
        chunk(c, None)
        return carry

    lax.fori_loop(0, i, body, 0)
    chunk(i, jnp.where(col <= row, 0.0, NEG_BIG))
    for h in range(C_HEADS):
        a = acc_ref[h]
        o_ref[:, h * 64:(h + 1) * 64] = (a[:, :64] / a[:, 64:65]).astype(BF16)


def _mla(qc, kc, vc, b, s):
    n = qc.shape[0]
    qb = MLA_QB
    nb = s // qb
    blk = lambda bb, i: (bb * nb + i, 0)
    seq = lambda bb, i: (bb, 0)
    return pl.pallas_call(
        _mla_kernel,
        grid=(b, nb),
        in_specs=[pl.BlockSpec((qb, 512), blk), pl.BlockSpec((s, 512), seq), pl.BlockSpec((s, 512), seq)],
        out_specs=pl.BlockSpec((qb, 256), blk),
        out_shape=jax.ShapeDtypeStruct((n, 256), BF16),
        scratch_shapes=[pltpu.VMEM((C_HEADS, qb, 1), F32), pltpu.VMEM((C_HEADS, qb, LANES), F32)],
        compiler_params=pltpu.CompilerParams(dimension_semantics=("parallel", "arbitrary"),
                                             vmem_limit_bytes=VMEM_LIMIT),
    )(qc, kc, vc)


def _layer_norm(z, g, b):
    mu = jnp.mean(z, axis=1, keepdims=True)
    zc = z - mu
    var = jnp.mean(zc * zc, axis=1, keepdims=True)
    return zc * lax.rsqrt(var + LN_EPS) * g + b


def _outproj_kernel(x_ref, oa_ref, ob_ref, oc_ref, wo_ref, g_ref, b_ref, wr_ref, rb_ref, x1_ref, gates_ref,
                    *, alpha):
    y = _dot(oa_ref[...], wo_ref[0:512, :]) + _dot(ob_ref[...], wo_ref[512:768, :]) \
        + _dot(oc_ref[...], wo_ref[768:1024, :])
    x1 = _layer_norm(alpha * x_ref[...] + y, g_ref[...], b_ref[...])
    x1_ref[...] = x1

    logits = lax.dot_general(wr_ref[...], x1, _NT, preferred_element_type=F32, precision=lax.Precision.HIGHEST)
    scores = jax.nn.sigmoid(logits)
    biased = scores + rb_ref[...]
    rows = [biased[e:e + 1, :] for e in range(N_EXPERTS)]
    gscore = []
    for g in range(N_GROUPS):
        v = rows[g * 4:(g + 1) * 4]
        best = None
        for a in range(4):
            for c in range(a + 1, 4):
                pair = v[a] + v[c]
                best = pair if best is None else jnp.maximum(best, pair)
        gscore.append(best)
    gmax = jnp.maximum(jnp.maximum(gscore[0], gscore[1]), jnp.maximum(gscore[2], gscore[3]))
    taken = jnp.zeros_like(gmax) > 1.0
    sel_rows = []
    for g in range(N_GROUPS):
        g_sel = (gscore[g] == gmax) & jnp.logical_not(taken)
        taken = taken | g_sel
        v = rows[g * 4:(g + 1) * 4]
        for e in range(4):
            rank = jnp.zeros_like(gmax)
            for j in range(4):
                if j == e:
                    continue
                ahead = (v[j] > v[e]) | ((v[j] == v[e]) & (j < e))
                rank = rank + jnp.where(ahead, 1.0, 0.0)
            sel_rows.append(g_sel & (rank < 2.0))
    wsel = [jnp.where(sel_rows[e], scores[e:e + 1, :], 0.0) for e in range(N_EXPERTS)]
    total = wsel[0]
    for e in range(1, N_EXPERTS):
        total = total + wsel[e]
    gates_ref[...] = jnp.concatenate(wsel, axis=0) / total


def _outproj(x2, oa, ob, oc, wo, g, bb, wr_t, rbias, alpha, tm):
    n, d = x2.shape
    nt = n // tm
    full = lambda a: pl.BlockSpec(a.shape, lambda t: (0,) * a.ndim)
    row = lambda c: pl.BlockSpec((tm, c), lambda t: (t, 0))
    return pl.pallas_call(
        functools.partial(_outproj_kernel, alpha=alpha),
        grid=(nt,),
        in_specs=[row(d), row(512), row(256), row(256), full(wo), full(g), full(bb), full(wr_t), full(rbias)],
        out_specs=[row(d), pl.BlockSpec((N_EXPERTS, tm), lambda t: (0, t))],
        out_shape=[jax.ShapeDtypeStruct((n, d), F32), jax.ShapeDtypeStruct((N_EXPERTS, n), F32)],
        compiler_params=pltpu.CompilerParams(dimension_semantics=("parallel",), vmem_limit_bytes=VMEM_LIMIT),
    )(x2, oa, ob, oc, wo, g, bb, wr_t, rbias)


def _moe_kernel(x_ref, gates_ref, wg_ref, wu_ref, wd_ref, g_ref, b_ref, o_ref, xb_ref, acc_ref, *, alpha):
    e = pl.program_id(1)

    @pl.when(e == 0)
    def _():
        xb_ref[...] = x_ref[...].astype(BF16)
        acc_ref[...] = jnp.zeros(acc_ref.shape, F32)

    xb = xb_ref[...]
    gates = gates_ref[...]
    lane = lax.broadcasted_iota(jnp.int32, gates.shape, 1)
    gate = jnp.sum(jnp.where(lane == e, gates, 0.0), axis=1, keepdims=True)
    hg = _dot(xb, wg_ref[0])
    hu = _dot(xb, wu_ref[0])
    h = hg * jax.nn.sigmoid(hg) * hu * gate
    acc_ref[...] += _dot(h.astype(BF16), wd_ref[0])

    @pl.when(e == N_EXPERTS - 1)
    def _():
        o_ref[...] = _layer_norm(alpha * x_ref[...] + acc_ref[...], g_ref[...], b_ref[...])


def _moe(x1, gates, wg, wu, wd, g, bb, alpha, tm):
    n, d = x1.shape
    nt = n // tm
    f = wg.shape[2]
    full = lambda a: pl.BlockSpec(a.shape, lambda t, e: (0,) * a.ndim)
    return pl.pallas_call(
        functools.partial(_moe_kernel, alpha=alpha),
        grid=(nt, N_EXPERTS),
        in_specs=[pl.BlockSpec((tm, d), lambda t, e: (t, 0)), pl.BlockSpec((tm, N_EXPERTS), lambda t, e: (t, 0)),
                  pl.BlockSpec((1, d, f), lambda t, e: (e, 0, 0)), pl.BlockSpec((1, d, f), lambda t, e: (e, 0, 0)),
                  pl.BlockSpec((1, f, d), lambda t, e: (e, 0, 0)), full(g), full(bb)],
        out_specs=pl.BlockSpec((tm, d), lambda t, e: (t, 0)),
        out_shape=jax.ShapeDtypeStruct((n, d), F32),
        scratch_shapes=[pltpu.VMEM((tm, d), BF16), pltpu.VMEM((tm, d), F32)],
        compiler_params=pltpu.CompilerParams(dimension_semantics=("parallel", "arbitrary"),
                                             vmem_limit_bytes=VMEM_LIMIT),
    )(x1, gates, wg, wu, wd, g, bb)


def kernel(x, w_in, attn_sinks, c_q_norm_g, c_kv_norm_g, w_uq, w_ukv, w_out, ln1_g, ln1_b, w_router, router_bias,
           w_gate, w_up, w_down, ln2_g, ln2_b):
    b, s, d = x.shape
    depth = w_in.shape[0]
    n = b * s
    alpha = (2 * depth) ** 0.25
    tm = min(512, s)
    tables = _rope_tables(s)
    wr_t = w_router.T
    rbias = router_bias.reshape(N_EXPERTS, 1)
    x2 = x.reshape(n, d)
    for l in range(depth):
        w_p = _pack_w_in(w_in[l])
        wuq_p, wukv_p = _pack_mla_w(w_uq[l], w_ukv[l])
        aq, ak, bq, bkik, iq, av, bv, iw, qc, kc, vc = _inproj(
            x2, w_p, tables, c_q_norm_g[l].reshape(1, -1), c_kv_norm_g[l].reshape(1, -1), wuq_p, wukv_p, s, tm)
        oa = _swa(attn_sinks[l], aq, ak, av, b, s)
        ob = _dsa(bq, iq, iw, bkik, bv, b, s)
        oc = _mla(qc, kc, vc, b, s)
        x1, gates_t = _outproj(x2, oa, ob, oc, w_out[l].astype(BF16), ln1_g[l].reshape(1, d), ln1_b[l].reshape(1, d),
                               wr_t, rbias, alpha, tm)
        x2 = _moe(x1, gates_t.T, w_gate[l].astype(BF16), w_up[l].astype(BF16), w_down[l].astype(BF16),
                  ln2_g[l].reshape(1, d), ln2_b[l].reshape(1, d), alpha, tm)
    return x2.reshape(b, s, d)
```

```python
import functools
import math

import jax
import jax.numpy as jnp
import numpy as np
from jax import lax
from jax.experimental import pallas as pl
from jax.experimental.pallas import tpu as pltpu

HEAD_DIM = 64
ROPE_THETA = 10000.0
A_HEADS = 8
A_KV_HEADS = 2
A_WINDOW = 128
B_HEADS = 4
IDX_HEADS = 4
IDX_DIM = 64
IDX_TOPK_MAX = 256
C_HEADS = 4
C_NOPE = 64
C_ROPE = 32
C_V = 64
C_Q_RANK = 256
C_KV_RANK = 128
N_EXPERTS = 16
N_GROUPS = 4
EXPERTS_PER_GROUP = 4
D_EXPERT = 256
LN_EPS = 1e-5
RMS_EPS = 1e-6

LANES = 128
NEG_BIG = -1e30
LOG2E = math.log2(math.e)
INT_MIN = -(2 ** 31)
VMEM_LIMIT = 56 * 1024 * 1024

BF16 = jnp.bfloat16
F32 = jnp.float32

_NT = (((1,), (1,)), ((), ()))


def _dot(a, b):
    return jnp.dot(a, b, preferred_element_type=F32)


def _dot_nt(a, b):
    return lax.dot_general(a, b, _NT, preferred_element_type=F32)


N_ROPE_UNITS = 10


def _inproj_kernel(x_ref, w_ref, ch_ref, s1h_ref, s2h_ref, cr_ref, s1r_ref, s2r_ref, gq_ref, gkv_ref,
                   wuq_ref, wukv_ref,
                   aq_ref, ak_ref, bq_ref, bk_ref, ik_ref, iq_ref, av_ref, bv_ref, iw_ref, qc_ref, kc_ref, vc_ref,
                   *, b_scale, c_scale):
    xb = x_ref[...].astype(BF16)
    tm = xb.shape[0]
    lane = lax.broadcasted_iota(jnp.int32, (tm, LANES), 1)
    ch, s1h, s2h = ch_ref[...], s1h_ref[...], s2h_ref[...]
    cr, s1r, s2r = cr_ref[...], s1r_ref[...], s2r_ref[...]
    low = lane < 64

    def rope_h(u):
        return u * ch + pltpu.roll(u, 96, 1) * s1h + pltpu.roll(u, 32, 1) * s2h

    def rope_r(u):
        return u * cr + pltpu.roll(u, 112, 1) * s1r + pltpu.roll(u, 16, 1) * s2r

    def split(u):
        return jnp.where(low, u, 0.0), jnp.where(low, pltpu.roll(u, 64, 1), 0.0)

    def put(ref, k, u):
        ref[:, k * LANES:(k + 1) * LANES] = u.astype(BF16)

    for g in range(N_ROPE_UNITS // 2):
        hg = _dot(xb, w_ref[:, g * 256:(g + 1) * 256])
        for half in range(2):
            unit = 2 * g + half
            u = rope_h(hg[:, half * LANES:(half + 1) * LANES])
            if unit < 4:
                put(aq_ref, unit, u)
            elif unit == 4:
                put(ak_ref, 0, u)
            elif unit < 7:
                h0, h1 = split(u * b_scale)
                put(bq_ref, 2 * (unit - 5), h0)
                put(bq_ref, 2 * (unit - 5) + 1, h1)
            elif unit == 7:
                h0, h1 = split(u)
                put(bk_ref, 0, h0)
                put(ik_ref, 0, h1)
            else:
                h0, h1 = split(u)
                put(iq_ref, 2 * (unit - 8), h0)
                put(iq_ref, 2 * (unit - 8) + 1, h1)

    hv = _dot(xb, w_ref[:, 1280:1536])
    for k in range(2):
        u = hv[:, k * LANES:(k + 1) * LANES]
        av_ref[:, k * LANES:(k + 1) * LANES] = jnp.where(lane == 64, 1.0, u).astype(BF16)

    hb = _dot(xb, w_ref[:, 1536:1664])
    iw_ref[...] = hb
    bv_ref[...] = jnp.where(lane < 64, hb, jnp.where(lane == 64, 1.0, 0.0)).astype(BF16)

    cq = _dot(xb, w_ref[:, 1664:1920])
    cqn = cq * lax.rsqrt(jnp.mean(cq * cq, axis=1, keepdims=True) + RMS_EPS) * gq_ref[...]
    qc = _dot(cqn.astype(BF16), wuq_ref[...])
    for h in range(C_HEADS):
        u = qc[:, h * LANES:(h + 1) * LANES]
        qc_ref[:, h * LANES:(h + 1) * LANES] = (rope_r(u) * c_scale).astype(BF16)

    ckv = _dot(xb, w_ref[:, 1920:2176])
    ckv_lat = ckv[:, :LANES]
    kr = rope_r(ckv[:, LANES:])
    ckvn = ckv_lat * lax.rsqrt(jnp.mean(ckv_lat * ckv_lat, axis=1, keepdims=True) + RMS_EPS) * gkv_ref[...]
    kv = _dot(ckvn.astype(BF16), wukv_ref[...])
    for h in range(C_HEADS):
        kc_ref[:, h * LANES:(h + 1) * LANES] = (kv[:, h * LANES:(h + 1) * LANES] + kr).astype(BF16)
        v = kv[:, (C_HEADS + h) * LANES:(C_HEADS + h + 1) * LANES]
        vc_ref[:, h * LANES:(h + 1) * LANES] = jnp.where(lane == 64, 1.0, v).astype(BF16)


def _pack_w_in(w):
    d = w.shape[0]
    z = lambda n: jnp.zeros((d, n), F32)
    o = np.cumsum([0, 512, 128, 128, 256, 64, 64, 256, 64, 4, 256, 128, 32]).tolist()
    a_q, a_k, a_v, b_q, b_k, b_v, i_q, i_k, i_w, c_q, c_kv, c_kr = [w[:, o[j]:o[j + 1]] for j in range(12)]
    qs = HEAD_DIM ** -0.5
    ws = (IDX_HEADS * IDX_DIM) ** -0.5
    cols = [a_q * qs, a_k, b_q, b_k, i_k, i_q,
            a_v[:, :64], z(64), a_v[:, 64:], z(64),
            b_v, i_w * ws, z(60),
            c_q, c_kv, z(64), c_kr, z(32)]
    return jnp.concatenate(cols, axis=1).astype(BF16)


def _pack_mla_w(w_uq, w_ukv):
    r = w_uq.shape[0]
    q = w_uq.reshape(r, C_HEADS, C_NOPE + C_ROPE)
    q = jnp.pad(q, ((0, 0), (0, 0), (0, LANES - C_NOPE - C_ROPE))).reshape(r, C_HEADS * LANES)
    r2 = w_ukv.shape[0]
    kv = w_ukv.reshape(r2, C_HEADS, C_NOPE + C_V)
    k = jnp.pad(kv[:, :, :C_NOPE], ((0, 0), (0, 0), (0, LANES - C_NOPE))).reshape(r2, C_HEADS * LANES)
    v = jnp.pad(kv[:, :, C_NOPE:], ((0, 0), (0, 0), (0, LANES - C_V))).reshape(r2, C_HEADS * LANES)
    return q.astype(BF16), jnp.concatenate([k, v], axis=1).astype(BF16)


def _rope_tables(s):
    pos = jnp.arange(s, dtype=F32)[:, None]
    lane = np.arange(LANES)
    inv_h = 1.0 / (ROPE_THETA ** (jnp.arange(0, HEAD_DIM, 2, dtype=F32) / HEAD_DIM))
    ang = pos * inv_h[None, :]
    cos, sin = jnp.cos(ang), jnp.sin(ang)
    j = lane % 32
    lo = jnp.asarray((lane % 64) < 32)
    ch = cos[:, j]
    s1h = jnp.where(lo, -sin[:, j], 0.0)
    s2h = jnp.where(lo, 0.0, sin[:, j])
    inv_r = 1.0 / (ROPE_THETA ** (jnp.arange(0, C_ROPE, 2, dtype=F32) / C_ROPE))
    angr = pos * inv_r[None, :]
    cosr, sinr = jnp.cos(angr), jnp.sin(angr)
    jr = lane % 16
    in_rope = jnp.asarray((lane >= 64) & (lane < 96))
    first = jnp.asarray((lane >= 64) & (lane < 80))
    second = jnp.asarray((lane >= 80) & (lane < 96))
    cr = jnp.where(in_rope, cosr[:, jr], 1.0)
    s1r = jnp.where(first, -sinr[:, jr], 0.0)
    s2r = jnp.where(second, sinr[:, jr], 0.0)
    return ch, s1h, s2h, cr, s1r, s2r


def _inproj(x2, w_p, tables, gq, gkv, wuq_p, wukv_p, s, tm):
    n, d = x2.shape
    nt = n // tm
    spt = s // tm
    tab_spec = pl.BlockSpec((tm, LANES), lambda t: (t % spt, 0))
    full = lambda a: pl.BlockSpec(a.shape, lambda t: (0,) * a.ndim)
    row = lambda c: pl.BlockSpec((tm, c), lambda t: (t, 0))
    out_cols = [512, 128, 512, 128, 128, 512, 256, 128, 128, 512, 512, 512]
    out_dt = [BF16] * 8 + [F32] + [BF16] * 3
    return pl.pallas_call(
        functools.partial(_inproj_kernel, b_scale=HEAD_DIM ** -0.5 * LOG2E,
                          c_scale=(C_NOPE + C_ROPE) ** -0.5 * LOG2E),
        grid=(nt,),
        in_specs=[row(d), full(w_p)] + [tab_spec] * 6 + [full(gq), full(gkv), full(wuq_p), full(wukv_p)],
        out_specs=[row(c) for c in out_cols],
        out_shape=[jax.ShapeDtypeStruct((n, c), dt) for c, dt in zip(out_cols, out_dt)],
        compiler_params=pltpu.CompilerParams(dimension_semantics=("parallel",), vmem_limit_bytes=VMEM_LIMIT),
    )(x2, w_p, *tables, gq, gkv, wuq_p, wukv_p)


def _swa_kernel(sink_ref, q_ref, kc_ref, kp_ref, vc_ref, vp_ref, o_ref):
    i = pl.program_id(1)
    w = A_WINDOW
    qi = lax.broadcasted_iota(jnp.int32, (w, 2 * w), 0)
    kj = lax.broadcasted_iota(jnp.int32, (w, 2 * w), 1)
    diff = qi + w - kj
    ok = (diff >= 0) & (diff < w) & ((kj >= w) | (i > 0))
    bias = jnp.where(ok, 0.0, NEG_BIG)
    group = A_HEADS // A_KV_HEADS
    for g in range(A_KV_HEADS):
        k2 = jnp.concatenate([kp_ref[:, g * 64:(g + 1) * 64], kc_ref[:, g * 64:(g + 1) * 64]], axis=0)
        v2 = jnp.concatenate([vp_ref[:, g * LANES:(g + 1) * LANES], vc_ref[:, g * LANES:(g + 1) * LANES]], axis=0)
        for hh in range(group):
            h = g * group + hh
            s = _dot_nt(q_ref[:, h * 64:(h + 1) * 64], k2) + bias
            sink = sink_ref[h]
            m = jnp.maximum(jnp.max(s, axis=1, keepdims=True), sink)
            p = jnp.exp(s - m)
            o = _dot(p.astype(BF16), v2)
            denom = o[:, 64:65] + jnp.exp(sink - m)
            o_ref[:, h * 64:(h + 1) * 64] = (o[:, :64] / denom).astype(BF16)


def _swa(sinks, aq, ak, av, b, s):
    n = aq.shape[0]
    w = A_WINDOW
    nb = s // w
    cur = lambda bb, i: (bb * nb + i, 0)
    prev = lambda bb, i: (bb * nb + jnp.maximum(i - 1, 0), 0)
    return pl.pallas_call(
        _swa_kernel,
        grid=(b, nb),
        in_specs=[pl.BlockSpec(memory_space=pltpu.SMEM),
                  pl.BlockSpec((w, 512), cur),
                  pl.BlockSpec((w, 128), cur), pl.BlockSpec((w, 128), prev),
                  pl.BlockSpec((w, 256), cur), pl.BlockSpec((w, 256), prev)],
        out_specs=pl.BlockSpec((w, 512), cur),
        out_shape=jax.ShapeDtypeStruct((n, 512), BF16),
        compiler_params=pltpu.CompilerParams(dimension_semantics=("parallel", "parallel"),
                                             vmem_limit_bytes=VMEM_LIMIT),
    )(sinks, aq, ak, ak, av, av)


ATT_KC = 256
DSA_RB = 128


def _att_qb(s):
    return min(512, s)


def _fold_max(s):
    return jnp.maximum(s[:, :LANES], s[:, LANES:])


def _dsa_kernel(bq_ref, iq_ref, iw_ref, bk_ref, ik_ref, v_ref, o_ref,
                sc_ref, s_ref, wb_ref, tau_ref, j_ref, ist_ref, qst_ref, m_ref, acc_ref, *, topk, idx_bits):
    i = pl.program_id(1)
    qb = bq_ref.shape[0]
    kc = ATT_KC
    per = qb // kc
    n_full = per * i
    n_chunks = n_full + per
    row = lax.broadcasted_iota(jnp.int32, (qb, kc), 0)
    col = lax.broadcasted_iota(jnp.int32, (qb, kc), 1)
    lane = lax.broadcasted_iota(jnp.int32, (qb, LANES), 1)
    neg_inf = float("-inf")
    kstart = lambda c: pl.multiple_of(c * kc, kc)
    heads = range(B_HEADS)
    hrows = lambda h: slice(h * qb, (h + 1) * qb)
    halves = [slice(t * LANES, (t + 1) * LANES) for t in range(kc // LANES)]

    for h in range(IDX_HEADS):
        ist_ref[hrows(h), :] = iq_ref[:, h * LANES:(h + 1) * LANES]
        wb_ref[h] = jnp.broadcast_to(iw_ref[:, 64 + h:65 + h], (qb, LANES))

    def score(c):
        lg = _dot_nt(ist_ref[...], ik_ref[pl.ds(kstart(c), kc), :])
        out = []
        for cols in halves:
            acc = None
            for h in range(IDX_HEADS):
                t = wb_ref[h] * jnp.maximum(lg[hrows(h), cols], 0.0)
                acc = t if acc is None else acc + t
            out.append(acc)
        return jnp.concatenate(out, axis=1)

    def idx_body(c, carry):
        sc_ref[c] = score(c)
        return carry

    lax.fori_loop(0, n_full, idx_body, 0)
    for d in range(per):
        sc_ref[n_full + d] = jnp.where(d * kc + col <= row, score(n_full + d), neg_inf)

    rb = DSA_RB
    colr = lax.broadcasted_iota(jnp.int32, (rb, kc), 1)

    def search_body(r, carry):
        rows = pl.ds(pl.multiple_of(r * rb, rb), rb)
        tpos = i * qb + r * rb + lax.broadcasted_iota(jnp.int32, (rb, 1), 0)
        n_r = n_full + (r * rb + rb - 1) // kc + 1

        def count(pred_fn):
            def cbody(c, acc):
                k = sc_ref[c, rows, :]
                hit = jnp.where(pred_fn(k, c), 1.0, 0.0)
                for cols in halves:
                    acc = acc + hit[:, cols]
                return acc
            acc = lax.fori_loop(0, n_r, cbody, jnp.zeros((rb, LANES), F32))
            return jnp.sum(acc, axis=1, keepdims=True)

        def key_to_f32(key_u):
            key = key_u ^ INT_MIN
            return pltpu.bitcast(key ^ ((key >> 31) & 0x7FFFFFFF), F32)

        def bit_body(step, prefix):
            cand_u = prefix | lax.shift_left(jnp.int32(1), 31 - step)
            cand = key_to_f32(cand_u)
            cnt = count(lambda k, c: k >= cand)
            return jnp.where(cnt >= topk, cand_u, prefix)

        prefix = lax.fori_loop(0, 32, bit_body, jnp.zeros((rb, 1), jnp.int32))
        tau = key_to_f32(prefix)
        tau = jnp.where(tau != tau, neg_inf, tau)
        c_gt = count(lambda k, c: k > tau)
        c_eq = count(lambda k, c: k == tau)
        need = topk - c_gt

        def tie_search(_):
            def jbody(step, q):
                cand = q | lax.shift_left(jnp.int32(1), idx_bits - 1 - step)
                cnt = count(lambda k, c: (k == tau) & (c * kc + colr < cand))
                return jnp.where(cnt < need, cand, q)
            return lax.fori_loop(0, idx_bits, jbody, jnp.zeros((rb, 1), jnp.int32))

        any_split = jnp.max(c_eq - need) > 0.0
        jcut = lax.cond(any_split, tie_search, lambda _: jnp.full((rb, 1), 2 ** idx_bits, jnp.int32), 0)
        few = tpos < int(topk)
        tau_ref[rows, :] = jnp.broadcast_to(jnp.where(few, neg_inf, tau), (rb, LANES))
        j_ref[rows, :] = jnp.broadcast_to(jnp.where(few, tpos, jcut), (rb, LANES))
        return carry

    lax.fori_loop(0, qb // rb, search_body, 0)

    for h in heads:
        qst_ref[hrows(h), :] = bq_ref[:, h * LANES:(h + 1) * LANES]
    m_ref[...] = jnp.full(m_ref.shape, NEG_BIG, F32)

    def max_body(c, carry):
        s = _dot_nt(qst_ref[...], bk_ref[pl.ds(kstart(c), kc), :])
        for t, cols in enumerate(halves):
            k = sc_ref[c, :, cols]
            tau = tau_ref[...]
            sel = (k > tau) | ((k == tau) & (c * kc + t * LANES + lane <= j_ref[...]))
            bias = jnp.where(sel, 0.0, NEG_BIG)
            for h in heads:
                sm = s[hrows(h), cols] + bias
                s_ref[h, c, :, cols] = sm
                m_ref[hrows(h), :] = jnp.maximum(m_ref[hrows(h), :], sm)
        return carry

    lax.fori_loop(0, n_chunks, max_body, 0)
    for h in heads:
        m = jnp.max(m_ref[hrows(h), :], axis=1, keepdims=True)
        m_ref[hrows(h), :] = jnp.broadcast_to(m, (qb, LANES))

    acc_ref[...] = jnp.zeros(acc_ref.shape, F32)

    def pv_body(c, carry):
        ps = []
        for h in heads:
            mb = m_ref[hrows(h), :]
            ps.append(jnp.exp2(s_ref[h, c] - jnp.concatenate([mb] * len(halves), axis=1)).astype(BF16))
        acc_ref[...] += _dot(jnp.concatenate(ps, axis=0), v_ref[pl.ds(kstart(c), kc), :])
        return carry

    lax.fori_loop(0, n_chunks, pv_body, 0)
    for h in heads:
        acc = acc_ref[hrows(h), :]
        o_ref[:, h * 64:(h + 1) * 64] = (acc[:, :64] / acc[:, 64:65]).astype(BF16)


def _dsa(bq, iq, iw, bk, ik, bv, b, s):
    n = bq.shape[0]
    qb = _att_qb(s)
    nb = s // qb
    nkc = s // ATT_KC
    topk = min(IDX_TOPK_MAX, s // 4)
    idx_bits = max(1, int(math.ceil(math.log2(s))))
    blk = lambda bb, i: (bb * nb + i, 0)
    seq = lambda bb, i: (bb, 0)
    return pl.pallas_call(
        functools.partial(_dsa_kernel, topk=float(topk), idx_bits=idx_bits),
        grid=(b, nb),
        in_specs=[pl.BlockSpec((qb, 512), blk), pl.BlockSpec((qb, 512), blk), pl.BlockSpec((qb, 128), blk),
                  pl.BlockSpec((s, 128), seq), pl.BlockSpec((s, 128), seq), pl.BlockSpec((s, 128), seq)],
        out_specs=pl.BlockSpec((qb, 256), blk),
        out_shape=jax.ShapeDtypeStruct((n, 256), BF16),
        scratch_shapes=[pltpu.VMEM((nkc, qb, ATT_KC), F32),
                        pltpu.VMEM((B_HEADS, nkc, qb, ATT_KC), F32),
                        pltpu.VMEM((IDX_HEADS, qb, LANES), F32),
                        pltpu.VMEM((qb, LANES), F32), pltpu.VMEM((qb, LANES), jnp.int32),
                        pltpu.VMEM((IDX_HEADS * qb, LANES), BF16), pltpu.VMEM((B_HEADS * qb, LANES), BF16),
                        pltpu.VMEM((B_HEADS * qb, LANES), F32), pltpu.VMEM((B_HEADS * qb, LANES), F32)],
        compiler_params=pltpu.CompilerParams(dimension_semantics=("parallel", "arbitrary"),
                                             vmem_limit_bytes=VMEM_LIMIT),
    )(bq, iq, iw, bk, ik, bv)


def _mla_kernel(q_ref, k_ref, v_ref, o_ref, s_ref, m_ref, acc_ref):
    i = pl.program_id(1)
    qb = q_ref.shape[0]
    kc = ATT_KC
    per = qb // kc
    n_full = per * i
    n_chunks = n_full + per
    row = lax.broadcasted_iota(jnp.int32, (qb, kc), 0)
    col = lax.broadcasted_iota(jnp.int32, (qb, kc), 1)
    kstart = lambda c: pl.multiple_of(c * kc, kc)
    heads = range(C_HEADS)
    hs = lambda h: slice(h * LANES, (h + 1) * LANES)
    n_halves = kc // LANES
    m_ref[...] = jnp.full(m_ref.shape, NEG_BIG, F32)

    def max_step(c, bias):
        for h in heads:
            s = _dot_nt(q_ref[:, hs(h)], k_ref[pl.ds(kstart(c), kc), hs(h)])
            if bias is not None:
                s = s + bias
            s_ref[h, c] = s
            m_ref[h] = jnp.maximum(m_ref[h], _fold_max(s))

    def max_body(c, carry):
        max_step(c, None)
        return carry

    lax.fori_loop(0, n_full, max_body, 0)
    for d in range(per):
        max_step(n_full + d, jnp.where(d * kc + col <= row, 0.0, NEG_BIG))
    for h in heads:
        m_ref[h] = jnp.broadcast_to(jnp.max(m_ref[h], axis=1, keepdims=True), (qb, LANES))
    acc_ref[...] = jnp.zeros(acc_ref.shape, F32)

    def pv_body(c, carry):
        for h in heads:
            p = jnp.exp2(s_ref[h, c] - jnp.concatenate([m_ref[h]] * n_halves, axis=1))
            acc_ref[h] += _dot(p.astype(BF16), v_ref[pl.ds(kstart(c), kc), hs(h)])
        return carry

    lax.fori_loop(0, n_chunks, pv_body, 0)
    for h in heads:
        acc = acc_ref[h]
        o_ref[:, h * 64:(h + 1) * 64] = (acc[:, :64] / acc[:, 64:65]).astype(BF16)


def _mla(qc, kc, vc, b, s):
    n = qc.shape[0]
    qb = _att_qb(s)
    nb = s // qb
    nkc = s // ATT_KC
    blk = lambda bb, i: (bb * nb + i, 0)
    seq = lambda bb, i: (bb, 0)
    return pl.pallas_call(
        _mla_kernel,
        grid=(b, nb),
        in_specs=[pl.BlockSpec((qb, 512), blk), pl.BlockSpec((s, 512), seq), pl.BlockSpec((s, 512), seq)],
        out_specs=pl.BlockSpec((qb, 256), blk),
        out_shape=jax.ShapeDtypeStruct((n, 256), BF16),
        scratch_shapes=[pltpu.VMEM((C_HEADS, nkc, qb, ATT_KC), F32), pltpu.VMEM((C_HEADS, qb, LANES), F32),
                        pltpu.VMEM((C_HEADS, qb, LANES), F32)],
        compiler_params=pltpu.CompilerParams(dimension_semantics=("parallel", "arbitrary"),
                                             vmem_limit_bytes=VMEM_LIMIT),
    )(qc, kc, vc)


def _layer_norm(z, g, b):
    mu = jnp.mean(z, axis=1, keepdims=True)
    zc = z - mu
    var = jnp.mean(zc * zc, axis=1, keepdims=True)
    return zc * lax.rsqrt(var + LN_EPS) * g + b


def _outproj_kernel(x_ref, oa_ref, ob_ref, oc_ref, wo_ref, g_ref, b_ref, wr_ref, rb_ref, x1_ref, gates_ref,
                    *, alpha):
    y = _dot(oa_ref[...], wo_ref[0:512, :]) + _dot(ob_ref[...], wo_ref[512:768, :]) \
        + _dot(oc_ref[...], wo_ref[768:1024, :])
    x1 = _layer_norm(alpha * x_ref[...] + y, g_ref[...], b_ref[...])
    x1_ref[...] = x1

    logits = lax.dot_general(wr_ref[...], x1, _NT, preferred_element_type=F32, precision=lax.Precision.HIGHEST)
    scores = jax.nn.sigmoid(logits)
    biased = scores + rb_ref[...]
    rows = [biased[e:e + 1, :] for e in range(N_EXPERTS)]
    gscore = []
    for g in range(N_GROUPS):
        v = rows[g * 4:(g + 1) * 4]
        best = None
        for a in range(4):
            for c in range(a + 1, 4):
                pair = v[a] + v[c]
                best = pair if best is None else jnp.maximum(best, pair)
        gscore.append(best)
    gmax = jnp.maximum(jnp.maximum(gscore[0], gscore[1]), jnp.maximum(gscore[2], gscore[3]))
    taken = jnp.zeros_like(gmax) > 1.0
    sel_rows = []
    for g in range(N_GROUPS):
        g_sel = (gscore[g] == gmax) & jnp.logical_not(taken)
        taken = taken | g_sel
        v = rows[g * 4:(g + 1) * 4]
        for e in range(4):
            rank = jnp.zeros_like(gmax)
            for j in range(4):
                if j == e:
                    continue
                ahead = (v[j] > v[e]) | ((v[j] == v[e]) & (j < e))
                rank = rank + jnp.where(ahead, 1.0, 0.0)
            sel_rows.append(g_sel & (rank < 2.0))
    wsel = [jnp.where(sel_rows[e], scores[e:e + 1, :], 0.0) for e in range(N_EXPERTS)]
    total = wsel[0]
    for e in range(1, N_EXPERTS):
        total = total + wsel[e]
    gates_ref[...] = jnp.concatenate(wsel, axis=0) / total


def _outproj(x2, oa, ob, oc, wo, g, bb, wr_t, rbias, alpha, tm):
    n, d = x2.shape
    nt = n // tm
    full = lambda a: pl.BlockSpec(a.shape, lambda t: (0,) * a.ndim)
    row = lambda c: pl.BlockSpec((tm, c), lambda t: (t, 0))
    return pl.pallas_call(
        functools.partial(_outproj_kernel, alpha=alpha),
        grid=(nt,),
        in_specs=[row(d), row(512), row(256), row(256), full(wo), full(g), full(bb), full(wr_t), full(rbias)],
        out_specs=[row(d), pl.BlockSpec((N_EXPERTS, tm), lambda t: (0, t))],
        out_shape=[jax.ShapeDtypeStruct((n, d), F32), jax.ShapeDtypeStruct((N_EXPERTS, n), F32)],
        compiler_params=pltpu.CompilerParams(dimension_semantics=("parallel",), vmem_limit_bytes=VMEM_LIMIT),
    )(x2, oa, ob, oc, wo, g, bb, wr_t, rbias)


def _moe_kernel(x_ref, gates_ref, wg_ref, wu_ref, wd_ref, g_ref, b_ref, o_ref, xb_ref, acc_ref, *, alpha):
    e = pl.program_id(1)

    @pl.when(e == 0)
    def _():
        xb_ref[...] = x_ref[...].astype(BF16)
        acc_ref[...] = jnp.zeros(acc_ref.shape, F32)

    xb = xb_ref[...]
    gates = gates_ref[...]
    lane = lax.broadcasted_iota(jnp.int32, gates.shape, 1)
    gate = jnp.sum(jnp.where(lane == e, gates, 0.0), axis=1, keepdims=True)
    hg = _dot(xb, wg_ref[0])
    hu = _dot(xb, wu_ref[0])
    h = hg * jax.nn.sigmoid(hg) * hu * gate
    acc_ref[...] += _dot(h.astype(BF16), wd_ref[0])

    @pl.when(e == N_EXPERTS - 1)
    def _():
        o_ref[...] = _layer_norm(alpha * x_ref[...] + acc_ref[...], g_ref[...], b_ref[...])


def _moe(x1, gates, wg, wu, wd, g, bb, alpha, tm):
    n, d = x1.shape
    nt = n // tm
    f = wg.shape[2]
    full = lambda a: pl.BlockSpec(a.shape, lambda t, e: (0,) * a.ndim)
    return pl.pallas_call(
        functools.partial(_moe_kernel, alpha=alpha),
        grid=(nt, N_EXPERTS),
        in_specs=[pl.BlockSpec((tm, d), lambda t, e: (t, 0)), pl.BlockSpec((tm, N_EXPERTS), lambda t, e: (t, 0)),
                  pl.BlockSpec((1, d, f), lambda t, e: (e, 0, 0)), pl.BlockSpec((1, d, f), lambda t, e: (e, 0, 0)),
                  pl.BlockSpec((1, f, d), lambda t, e: (e, 0, 0)), full(g), full(bb)],
        out_specs=pl.BlockSpec((tm, d), lambda t, e: (t, 0)),
        out_shape=jax.ShapeDtypeStruct((n, d), F32),
        scratch_shapes=[pltpu.VMEM((tm, d), BF16), pltpu.VMEM((tm, d), F32)],
        compiler_params=pltpu.CompilerParams(dimension_semantics=("parallel", "arbitrary"),
                                             vmem_limit_bytes=VMEM_LIMIT),
    )(x1, gates, wg, wu, wd, g, bb)


def kernel(x, w_in, attn_sinks, c_q_norm_g, c_kv_norm_g, w_uq, w_ukv, w_out, ln1_g, ln1_b, w_router, router_bias,
           w_gate, w_up, w_down, ln2_g, ln2_b):
    b, s, d = x.shape
    depth = w_in.shape[0]
    n = b * s
    alpha = (2 * depth) ** 0.25
    tm = min(512, s)
    tables = _rope_tables(s)
    wr_t = w_router.T
    rbias = router_bias.reshape(N_EXPERTS, 1)
    x2 = x.reshape(n, d)
    for l in range(depth):
        w_p = _pack_w_in(w_in[l])
        wuq_p, wukv_p = _pack_mla_w(w_uq[l], w_ukv[l])
        aq, ak, bq, bk, ik, iq, av, bv, iw, qc, kc, vc = _inproj(
            x2, w_p, tables, c_q_norm_g[l].reshape(1, -1), c_kv_norm_g[l].reshape(1, -1), wuq_p, wukv_p, s, tm)
        oa = _swa(attn_sinks[l], aq, ak, av, b, s)
        ob = _dsa(bq, iq, iw, bk, ik, bv, b, s)
        oc = _mla(qc, kc, vc, b, s)
        x1, gates_t = _outproj(x2, oa, ob, oc, w_out[l].astype(BF16), ln1_g[l].reshape(1, d), ln1_b[l].reshape(1, d),
                               wr_t, rbias, alpha, tm)
        x2 = _moe(x1, gates_t.T, w_gate[l].astype(BF16), w_up[l].astype(BF16), w_down[l].astype(BF16),
                  ln2_g[l].reshape(1, d), ln2_b[l].reshape(1, d), alpha, tm)
    return x2.reshape(b, s, d)
```

```python
import functools
import math

import jax
import jax.numpy as jnp
import numpy as np
from jax import lax
from jax.experimental import pallas as pl
from jax.experimental.pallas import tpu as pltpu

HEAD_DIM = 64
ROPE_THETA = 10000.0
A_HEADS = 8
A_KV_HEADS = 2
A_WINDOW = 128
B_HEADS = 4
IDX_HEADS = 4
IDX_DIM = 64
IDX_TOPK_MAX = 256
C_HEADS = 4
C_NOPE = 64
C_ROPE = 32
C_V = 64
C_Q_RANK = 256
C_KV_RANK = 128
N_EXPERTS = 16
N_GROUPS = 4
EXPERTS_PER_GROUP = 4
D_EXPERT = 256
LN_EPS = 1e-5
RMS_EPS = 1e-6

LANES = 128
SUBLANES = 8
NEG_BIG = -1e30
LOG2E = math.log2(math.e)
INT_MIN = -(2 ** 31)
VMEM_LIMIT = 56 * 1024 * 1024

BF16 = jnp.bfloat16
F32 = jnp.float32

_NT = (((1,), (1,)), ((), ()))


def _dot(a, b):
    return jnp.dot(a, b, preferred_element_type=F32)


def _dot_nt(a, b):
    return lax.dot_general(a, b, _NT, preferred_element_type=F32)


N_ROPE_UNITS = 10


def _inproj_kernel(x_ref, w_ref, ch_ref, s1h_ref, s2h_ref, cr_ref, s1r_ref, s2r_ref, gq_ref, gkv_ref,
                   wuq_ref, wukv_ref, wt_ref,
                   aq_ref, ak_ref, bq_ref, bk_ref, ik_ref, iq_ref, av_ref, vt_ref, iwt_ref, qc_ref, kc_ref, vc_ref,
                   *, b_scale, c_scale):
    xb = x_ref[...].astype(BF16)
    tm = xb.shape[0]
    lane = lax.broadcasted_iota(jnp.int32, (tm, LANES), 1)
    ch, s1h, s2h = ch_ref[...], s1h_ref[...], s2h_ref[...]
    cr, s1r, s2r = cr_ref[...], s1r_ref[...], s2r_ref[...]
    low = lane < 64

    def rope_h(u):
        return u * ch + pltpu.roll(u, 96, 1) * s1h + pltpu.roll(u, 32, 1) * s2h

    def rope_r(u):
        return u * cr + pltpu.roll(u, 112, 1) * s1r + pltpu.roll(u, 16, 1) * s2r

    def split(u):
        return jnp.where(low, u, 0.0), jnp.where(low, pltpu.roll(u, 64, 1), 0.0)

    def put(ref, k, u):
        ref[:, k * LANES:(k + 1) * LANES] = u.astype(BF16)

    for g in range(N_ROPE_UNITS // 2):
        hg = _dot(xb, w_ref[:, g * 256:(g + 1) * 256])
        for half in range(2):
            unit = 2 * g + half
            u = rope_h(hg[:, half * LANES:(half + 1) * LANES])
            if unit < 4:
                put(aq_ref, unit, u)
            elif unit == 4:
                put(ak_ref, 0, u)
            elif unit < 7:
                h0, h1 = split(u * b_scale)
                put(bq_ref, 2 * (unit - 5), h0)
                put(bq_ref, 2 * (unit - 5) + 1, h1)
            elif unit == 7:
                h0, h1 = split(u)
                put(bk_ref, 0, h0)
                put(ik_ref, 0, h1)
            else:
                h0, h1 = split(u)
                put(iq_ref, 2 * (unit - 8), h0)
                put(iq_ref, 2 * (unit - 8) + 1, h1)

    hv = _dot(xb, w_ref[:, 1280:1536])
    for k in range(2):
        u = hv[:, k * LANES:(k + 1) * LANES]
        av_ref[:, k * LANES:(k + 1) * LANES] = jnp.where(lane == 64, 1.0, u).astype(BF16)

    hb = _dot_nt(wt_ref[...], xb)
    sub = lax.broadcasted_iota(jnp.int32, (LANES, tm), 0)
    vt = jnp.where(sub == 64, 1.0, hb[:LANES]).astype(BF16)
    for j in range(tm // ATT_KC):
        vt_ref[j] = vt[:, j * ATT_KC:(j + 1) * ATT_KC]
    iwt_ref[...] = hb[LANES:LANES + 8]

    cq = _dot(xb, w_ref[:, 1536:1792])
    cqn = cq * lax.rsqrt(jnp.mean(cq * cq, axis=1, keepdims=True) + RMS_EPS) * gq_ref[...]
    qc = _dot(cqn.astype(BF16), wuq_ref[...])
    for h in range(C_HEADS):
        u = qc[:, h * LANES:(h + 1) * LANES]
        qc_ref[:, h * LANES:(h + 1) * LANES] = (rope_r(u) * c_scale).astype(BF16)

    ckv = _dot(xb, w_ref[:, 1792:2048])
    ckv_lat = ckv[:, :LANES]
    kr = rope_r(ckv[:, LANES:])
    ckvn = ckv_lat * lax.rsqrt(jnp.mean(ckv_lat * ckv_lat, axis=1, keepdims=True) + RMS_EPS) * gkv_ref[...]
    kv = _dot(ckvn.astype(BF16), wukv_ref[...])
    for h in range(C_HEADS):
        kc_ref[:, h * LANES:(h + 1) * LANES] = (kv[:, h * LANES:(h + 1) * LANES] + kr).astype(BF16)
        v = kv[:, (C_HEADS + h) * LANES:(C_HEADS + h + 1) * LANES]
        vc_ref[:, h * LANES:(h + 1) * LANES] = jnp.where(lane == 64, 1.0, v).astype(BF16)


def _pack_w_in(w):
    d = w.shape[0]
    z = lambda n: jnp.zeros((d, n), F32)
    o = np.cumsum([0, 512, 128, 128, 256, 64, 64, 256, 64, 4, 256, 128, 32]).tolist()
    a_q, a_k, a_v, b_q, b_k, b_v, i_q, i_k, i_w, c_q, c_kv, c_kr = [w[:, o[j]:o[j + 1]] for j in range(12)]
    qs = HEAD_DIM ** -0.5
    ws = (IDX_HEADS * IDX_DIM) ** -0.5
    cols = [a_q * qs, a_k, b_q, b_k, i_k, i_q,
            a_v[:, :64], z(64), a_v[:, 64:], z(64),
            c_q, c_kv, z(64), c_kr, z(32)]
    rows_t = jnp.concatenate([b_v, z(64), i_w * ws, z(LANES - IDX_HEADS)], axis=1).T
    return jnp.concatenate(cols, axis=1).astype(BF16), rows_t.astype(BF16)


def _pack_mla_w(w_uq, w_ukv):
    r = w_uq.shape[0]
    q = w_uq.reshape(r, C_HEADS, C_NOPE + C_ROPE)
    q = jnp.pad(q, ((0, 0), (0, 0), (0, LANES - C_NOPE - C_ROPE))).reshape(r, C_HEADS * LANES)
    r2 = w_ukv.shape[0]
    kv = w_ukv.reshape(r2, C_HEADS, C_NOPE + C_V)
    k = jnp.pad(kv[:, :, :C_NOPE], ((0, 0), (0, 0), (0, LANES - C_NOPE))).reshape(r2, C_HEADS * LANES)
    v = jnp.pad(kv[:, :, C_NOPE:], ((0, 0), (0, 0), (0, LANES - C_V))).reshape(r2, C_HEADS * LANES)
    return q.astype(BF16), jnp.concatenate([k, v], axis=1).astype(BF16)


def _rope_tables(s):
    pos = jnp.arange(s, dtype=F32)[:, None]
    lane = np.arange(LANES)
    inv_h = 1.0 / (ROPE_THETA ** (jnp.arange(0, HEAD_DIM, 2, dtype=F32) / HEAD_DIM))
    ang = pos * inv_h[None, :]
    cos, sin = jnp.cos(ang), jnp.sin(ang)
    j = lane % 32
    lo = jnp.asarray((lane % 64) < 32)
    ch = cos[:, j]
    s1h = jnp.where(lo, -sin[:, j], 0.0)
    s2h = jnp.where(lo, 0.0, sin[:, j])
    inv_r = 1.0 / (ROPE_THETA ** (jnp.arange(0, C_ROPE, 2, dtype=F32) / C_ROPE))
    angr = pos * inv_r[None, :]
    cosr, sinr = jnp.cos(angr), jnp.sin(angr)
    jr = lane % 16
    in_rope = jnp.asarray((lane >= 64) & (lane < 96))
    first = jnp.asarray((lane >= 64) & (lane < 80))
    second = jnp.asarray((lane >= 80) & (lane < 96))
    cr = jnp.where(in_rope, cosr[:, jr], 1.0)
    s1r = jnp.where(first, -sinr[:, jr], 0.0)
    s2r = jnp.where(second, sinr[:, jr], 0.0)
    return ch, s1h, s2h, cr, s1r, s2r


def _inproj(x2, w_p, w_t, tables, gq, gkv, wuq_p, wukv_p, s, tm):
    n, d = x2.shape
    nt = n // tm
    spt = s // tm
    cpt = tm // ATT_KC
    tab_spec = pl.BlockSpec((tm, LANES), lambda t: (t % spt, 0))
    full = lambda a: pl.BlockSpec(a.shape, lambda t: (0,) * a.ndim)
    row = lambda c: pl.BlockSpec((tm, c), lambda t: (t, 0))
    cols_a, cols_c = [512, 128, 512, 128, 128, 512, 256], [512, 512, 512]
    out_specs = ([row(c) for c in cols_a]
                 + [pl.BlockSpec((cpt, LANES, ATT_KC), lambda t: (t, 0, 0)), pl.BlockSpec((8, tm), lambda t: (0, t))]
                 + [row(c) for c in cols_c])
    out_shape = ([jax.ShapeDtypeStruct((n, c), BF16) for c in cols_a]
                 + [jax.ShapeDtypeStruct((n // ATT_KC, LANES, ATT_KC), BF16), jax.ShapeDtypeStruct((8, n), F32)]
                 + [jax.ShapeDtypeStruct((n, c), BF16) for c in cols_c])
    return pl.pallas_call(
        functools.partial(_inproj_kernel, b_scale=HEAD_DIM ** -0.5 * LOG2E,
                          c_scale=(C_NOPE + C_ROPE) ** -0.5 * LOG2E),
        grid=(nt,),
        in_specs=[row(d), full(w_p)] + [tab_spec] * 6 + [full(gq), full(gkv), full(wuq_p), full(wukv_p), full(w_t)],
        out_specs=out_specs,
        out_shape=out_shape,
        compiler_params=pltpu.CompilerParams(dimension_semantics=("parallel",), vmem_limit_bytes=VMEM_LIMIT),
    )(x2, w_p, *tables, gq, gkv, wuq_p, wukv_p, w_t)


def _swa_kernel(sink_ref, q_ref, kc_ref, kp_ref, vc_ref, vp_ref, o_ref):
    i = pl.program_id(1)
    w = A_WINDOW
    qi = lax.broadcasted_iota(jnp.int32, (w, 2 * w), 0)
    kj = lax.broadcasted_iota(jnp.int32, (w, 2 * w), 1)
    diff = qi + w - kj
    ok = (diff >= 0) & (diff < w) & ((kj >= w) | (i > 0))
    bias = jnp.where(ok, 0.0, NEG_BIG)
    group = A_HEADS // A_KV_HEADS
    for g in range(A_KV_HEADS):
        k2 = jnp.concatenate([kp_ref[:, g * 64:(g + 1) * 64], kc_ref[:, g * 64:(g + 1) * 64]], axis=0)
        v2 = jnp.concatenate([vp_ref[:, g * LANES:(g + 1) * LANES], vc_ref[:, g * LANES:(g + 1) * LANES]], axis=0)
        for hh in range(group):
            h = g * group + hh
            s = _dot_nt(q_ref[:, h * 64:(h + 1) * 64], k2) + bias
            sink = sink_ref[h]
            m = jnp.maximum(jnp.max(s, axis=1, keepdims=True), sink)
            p = jnp.exp(s - m)
            o = _dot(p.astype(BF16), v2)
            denom = o[:, 64:65] + jnp.exp(sink - m)
            o_ref[:, h * 64:(h + 1) * 64] = (o[:, :64] / denom).astype(BF16)


def _swa(sinks, aq, ak, av, b, s):
    n = aq.shape[0]
    w = A_WINDOW
    nb = s // w
    cur = lambda bb, i: (bb * nb + i, 0)
    prev = lambda bb, i: (bb * nb + jnp.maximum(i - 1, 0), 0)
    return pl.pallas_call(
        _swa_kernel,
        grid=(b, nb),
        in_specs=[pl.BlockSpec(memory_space=pltpu.SMEM),
                  pl.BlockSpec((w, 512), cur),
                  pl.BlockSpec((w, 128), cur), pl.BlockSpec((w, 128), prev),
                  pl.BlockSpec((w, 256), cur), pl.BlockSpec((w, 256), prev)],
        out_specs=pl.BlockSpec((w, 512), cur),
        out_shape=jax.ShapeDtypeStruct((n, 512), BF16),
        compiler_params=pltpu.CompilerParams(dimension_semantics=("parallel", "parallel"),
                                             vmem_limit_bytes=VMEM_LIMIT),
    )(sinks, aq, ak, ak, av, av)


ATT_KC = 256


def _att_qb(s):
    return min(512, s)


def _fold_max(s):
    return jnp.maximum(s[:, :LANES], s[:, LANES:])


def _dsa_kernel(bq_ref, iq_ref, iwt_ref, bk_ref, ik_ref, vt_ref, o_ref,
                sc_ref, s_ref, tau_ref, j_ref, ist_ref, qst_ref, m_ref, acc_ref, *, topk, idx_bits):
    i = pl.program_id(1)
    qb = bq_ref.shape[0]
    kc = ATT_KC
    per = qb // kc
    n_full = per * i
    n_chunks = n_full + per
    kidx = lax.broadcasted_iota(jnp.int32, (kc, qb), 0)
    rpos = lax.broadcasted_iota(jnp.int32, (kc, qb), 1)
    neg_inf = float("-inf")
    kstart = lambda c: pl.multiple_of(c * kc, kc)
    heads = range(B_HEADS)
    hrows = lambda h: slice(h * qb, (h + 1) * qb)
    groups = [slice(g * SUBLANES, (g + 1) * SUBLANES) for g in range(kc // SUBLANES)]

    def fold(x, op):
        out = x[groups[0]]
        for g in groups[1:]:
            out = op(out, x[g])
        return out

    for h in range(IDX_HEADS):
        ist_ref[hrows(h), :] = iq_ref[:, h * LANES:(h + 1) * LANES]

    def score(c):
        lg = _dot_nt(ik_ref[pl.ds(kstart(c), kc), :], ist_ref[...])
        acc = None
        for h in range(IDX_HEADS):
            t = iwt_ref[h:h + 1, :] * jnp.maximum(lg[:, hrows(h)], 0.0)
            acc = t if acc is None else acc + t
        return acc

    def idx_body(c, carry):
        sc_ref[c] = score(c)
        return carry

    lax.fori_loop(0, n_full, idx_body, 0)
    for d in range(per):
        sc_ref[n_full + d] = jnp.where(d * kc + kidx <= rpos, score(n_full + d), neg_inf)

    rw = qb // per
    kidx_r = lax.broadcasted_iota(jnp.int32, (kc, rw), 0)
    for r in range(per):
        rows = slice(r * rw, (r + 1) * rw)
        tpos = i * qb + r * rw + lax.broadcasted_iota(jnp.int32, (1, rw), 1)
        n_r = n_full + r + 1

        def count(pred_fn):
            def cbody(c, acc):
                k = sc_ref[c, :, rows]
                return acc + fold(jnp.where(pred_fn(k, c), 1.0, 0.0), jnp.add)
            acc = lax.fori_loop(0, n_r, cbody, jnp.zeros((SUBLANES, rw), F32))
            return jnp.sum(acc, axis=0, keepdims=True)

        def key_to_f32(key_u):
            key = key_u ^ INT_MIN
            return pltpu.bitcast(key ^ ((key >> 31) & 0x7FFFFFFF), F32)

        def bit_body(step, prefix):
            cand_u = prefix | lax.shift_left(jnp.int32(1), 31 - step)
            cand = key_to_f32(cand_u)
            cnt = count(lambda k, c: k >= cand)
            return jnp.where(cnt >= topk, cand_u, prefix)

        prefix = lax.fori_loop(0, 32, bit_body, jnp.zeros((1, rw), jnp.int32))
        tau = key_to_f32(prefix)
        tau = jnp.where(tau != tau, neg_inf, tau)
        c_gt = count(lambda k, c: k > tau)
        c_eq = count(lambda k, c: k == tau)
        need = topk - c_gt

        def tie_search(_):
            def jbody(step, q):
                cand = q | lax.shift_left(jnp.int32(1), idx_bits - 1 - step)
                cnt = count(lambda k, c: (k == tau) & (c * kc + kidx_r < cand))
                return jnp.where(cnt < need, cand, q)
            return lax.fori_loop(0, idx_bits, jbody, jnp.zeros((1, rw), jnp.int32))

        any_split = jnp.max(c_eq - need) > 0.0
        jcut = lax.cond(any_split, tie_search, lambda _: jnp.full((1, rw), 2 ** idx_bits, jnp.int32), 0)
        few = tpos < int(topk)
        tau_ref[:, rows] = jnp.broadcast_to(jnp.where(few, neg_inf, tau), (SUBLANES, rw))
        j_ref[:, rows] = jnp.broadcast_to(jnp.where(few, tpos, jcut), (SUBLANES, rw))

    for h in heads:
        qst_ref[hrows(h), :] = bq_ref[:, h * LANES:(h + 1) * LANES]
    m_ref[...] = jnp.full(m_ref.shape, NEG_BIG, F32)

    def max_body(c, carry):
        s = _dot_nt(bk_ref[pl.ds(kstart(c), kc), :], qst_ref[...])
        k = sc_ref[c]
        tau = tau_ref[0:1, :]
        sel = (k > tau) | ((k == tau) & (c * kc + kidx <= j_ref[0:1, :]))
        bias = jnp.where(sel, 0.0, NEG_BIG)
        for h in heads:
            sm = s[:, hrows(h)] + bias
            s_ref[c, :, hrows(h)] = sm
            m_ref[:, hrows(h)] = jnp.maximum(m_ref[:, hrows(h)], fold(sm, jnp.maximum))
        return carry

    lax.fori_loop(0, n_chunks, max_body, 0)
    m_ref[...] = jnp.broadcast_to(jnp.max(m_ref[...], axis=0, keepdims=True), m_ref.shape)

    acc_ref[...] = jnp.zeros(acc_ref.shape, F32)

    def pv_body(c, carry):
        p = jnp.exp2(s_ref[c] - m_ref[0:1, :]).astype(BF16)
        acc_ref[...] += _dot(vt_ref[c], p)
        return carry

    lax.fori_loop(0, n_chunks, pv_body, 0)
    for h in heads:
        acc = acc_ref[:, hrows(h)]
        o_ref[:, h * 64:(h + 1) * 64] = (acc / acc[64:65, :]).T[:, :64].astype(BF16)


def _dsa(bq, iq, iwt, bk, ik, vt, b, s):
    n = bq.shape[0]
    qb = _att_qb(s)
    nb = s // qb
    nkc = s // ATT_KC
    topk = min(IDX_TOPK_MAX, s // 4)
    idx_bits = max(1, int(math.ceil(math.log2(s))))
    blk = lambda bb, i: (bb * nb + i, 0)
    seq = lambda bb, i: (bb, 0)
    return pl.pallas_call(
        functools.partial(_dsa_kernel, topk=float(topk), idx_bits=idx_bits),
        grid=(b, nb),
        in_specs=[pl.BlockSpec((qb, 512), blk), pl.BlockSpec((qb, 512), blk),
                  pl.BlockSpec((8, qb), lambda bb, i: (0, bb * nb + i)),
                  pl.BlockSpec((s, 128), seq), pl.BlockSpec((s, 128), seq),
                  pl.BlockSpec((nkc, LANES, ATT_KC), lambda bb, i: (bb, 0, 0))],
        out_specs=pl.BlockSpec((qb, 256), blk),
        out_shape=jax.ShapeDtypeStruct((n, 256), BF16),
        scratch_shapes=[pltpu.VMEM((nkc, ATT_KC, qb), F32),
                        pltpu.VMEM((nkc, ATT_KC, B_HEADS * qb), F32),
                        pltpu.VMEM((SUBLANES, qb), F32), pltpu.VMEM((SUBLANES, qb), jnp.int32),
                        pltpu.VMEM((IDX_HEADS * qb, LANES), BF16), pltpu.VMEM((B_HEADS * qb, LANES), BF16),
                        pltpu.VMEM((SUBLANES, B_HEADS * qb), F32), pltpu.VMEM((LANES, B_HEADS * qb), F32)],
        compiler_params=pltpu.CompilerParams(dimension_semantics=("parallel", "arbitrary"),
                                             vmem_limit_bytes=VMEM_LIMIT),
    )(bq, iq, iwt, bk, ik, vt)


def _mla_kernel(q_ref, k_ref, v_ref, o_ref, s_ref, m_ref, acc_ref):
    i = pl.program_id(1)
    qb = q_ref.shape[0]
    kc = ATT_KC
    per = qb // kc
    n_full = per * i
    n_chunks = n_full + per
    row = lax.broadcasted_iota(jnp.int32, (qb, kc), 0)
    col = lax.broadcasted_iota(jnp.int32, (qb, kc), 1)
    kstart = lambda c: pl.multiple_of(c * kc, kc)
    heads = range(C_HEADS)
    hs = lambda h: slice(h * LANES, (h + 1) * LANES)
    n_halves = kc // LANES
    m_ref[...] = jnp.full(m_ref.shape, NEG_BIG, F32)

    def max_step(c, bias):
        for h in heads:
            s = _dot_nt(q_ref[:, hs(h)], k_ref[pl.ds(kstart(c), kc), hs(h)])
            if bias is not None:
                s = s + bias
            s_ref[h, c] = s
            m_ref[h] = jnp.maximum(m_ref[h], _fold_max(s))

    def max_body(c, carry):
        max_step(c, None)
        return carry

    lax.fori_loop(0, n_full, max_body, 0)
    for d in range(per):
        max_step(n_full + d, jnp.where(d * kc + col <= row, 0.0, NEG_BIG))
    for h in heads:
        m_ref[h] = jnp.broadcast_to(jnp.max(m_ref[h], axis=1, keepdims=True), (qb, LANES))
    acc_ref[...] = jnp.zeros(acc_ref.shape, F32)

    def pv_body(c, carry):
        for h in heads:
            p = jnp.exp2(s_ref[h, c] - jnp.concatenate([m_ref[h]] * n_halves, axis=1))
            acc_ref[h] += _dot(p.astype(BF16), v_ref[pl.ds(kstart(c), kc), hs(h)])
        return carry

    lax.fori_loop(0, n_chunks, pv_body, 0)
    for h in heads:
        acc = acc_ref[h]
        o_ref[:, h * 64:(h + 1) * 64] = (acc[:, :64] / acc[:, 64:65]).astype(BF16)


def _mla(qc, kc, vc, b, s):
    n = qc.shape[0]
    qb = _att_qb(s)
    nb = s // qb
    nkc = s // ATT_KC
    blk = lambda bb, i: (bb * nb + i, 0)
    seq = lambda bb, i: (bb, 0)
    return pl.pallas_call(
        _mla_kernel,
        grid=(b, nb),
        in_specs=[pl.BlockSpec((qb, 512), blk), pl.BlockSpec((s, 512), seq), pl.BlockSpec((s, 512), seq)],
        out_specs=pl.BlockSpec((qb, 256), blk),
        out_shape=jax.ShapeDtypeStruct((n, 256), BF16),
        scratch_shapes=[pltpu.VMEM((C_HEADS, nkc, qb, ATT_KC), F32), pltpu.VMEM((C_HEADS, qb, LANES), F32),
                        pltpu.VMEM((C_HEADS, qb, LANES), F32)],
        compiler_params=pltpu.CompilerParams(dimension_semantics=("parallel", "arbitrary"),
                                             vmem_limit_bytes=VMEM_LIMIT),
    )(qc, kc, vc)


def _layer_norm(z, g, b):
    mu = jnp.mean(z, axis=1, keepdims=True)
    zc = z - mu
    var = jnp.mean(zc * zc, axis=1, keepdims=True)
    return zc * lax.rsqrt(var + LN_EPS) * g + b


def _outproj_kernel(x_ref, oa_ref, ob_ref, oc_ref, wo_ref, g_ref, b_ref, wr_ref, rb_ref, x1_ref, gates_ref,
                    *, alpha):
    y = _dot(oa_ref[...], wo_ref[0:512, :]) + _dot(ob_ref[...], wo_ref[512:768, :]) \
        + _dot(oc_ref[...], wo_ref[768:1024, :])
    x1 = _layer_norm(alpha * x_ref[...] + y, g_ref[...], b_ref[...])
    x1_ref[...] = x1

    logits = lax.dot_general(wr_ref[...], x1, _NT, preferred_element_type=F32, precision=lax.Precision.HIGHEST)
    scores = jax.nn.sigmoid(logits)
    biased = scores + rb_ref[...]
    rows = [biased[e:e + 1, :] for e in range(N_EXPERTS)]
    gscore = []
    for g in range(N_GROUPS):
        v = rows[g * 4:(g + 1) * 4]
        best = None
        for a in range(4):
            for c in range(a + 1, 4):
                pair = v[a] + v[c]
                best = pair if best is None else jnp.maximum(best, pair)
        gscore.append(best)
    gmax = jnp.maximum(jnp.maximum(gscore[0], gscore[1]), jnp.maximum(gscore[2], gscore[3]))
    taken = jnp.zeros_like(gmax) > 1.0
    sel_rows = []
    for g in range(N_GROUPS):
        g_sel = (gscore[g] == gmax) & jnp.logical_not(taken)
        taken = taken | g_sel
        v = rows[g * 4:(g + 1) * 4]
        for e in range(4):
            rank = jnp.zeros_like(gmax)
            for j in range(4):
                if j == e:
                    continue
                ahead = (v[j] > v[e]) | ((v[j] == v[e]) & (j < e))
                rank = rank + jnp.where(ahead, 1.0, 0.0)
            sel_rows.append(g_sel & (rank < 2.0))
    wsel = [jnp.where(sel_rows[e], scores[e:e + 1, :], 0.0) for e in range(N_EXPERTS)]
    total = wsel[0]
    for e in range(1, N_EXPERTS):
        total = total + wsel[e]
    gates_ref[...] = jnp.concatenate(wsel, axis=0) / total


def _outproj(x2, oa, ob, oc, wo, g, bb, wr_t, rbias, alpha, tm):
    n, d = x2.shape
    nt = n // tm
    full = lambda a: pl.BlockSpec(a.shape, lambda t: (0,) * a.ndim)
    row = lambda c: pl.BlockSpec((tm, c), lambda t: (t, 0))
    return pl.pallas_call(
        functools.partial(_outproj_kernel, alpha=alpha),
        grid=(nt,),
        in_specs=[row(d), row(512), row(256), row(256), full(wo), full(g), full(bb), full(wr_t), full(rbias)],
        out_specs=[row(d), pl.BlockSpec((N_EXPERTS, tm), lambda t: (0, t))],
        out_shape=[jax.ShapeDtypeStruct((n, d), F32), jax.ShapeDtypeStruct((N_EXPERTS, n), F32)],
        compiler_params=pltpu.CompilerParams(dimension_semantics=("parallel",), vmem_limit_bytes=VMEM_LIMIT),
    )(x2, oa, ob, oc, wo, g, bb, wr_t, rbias)


def _moe_kernel(x_ref, gates_ref, wg_ref, wu_ref, wd_ref, g_ref, b_ref, o_ref, xb_ref, acc_ref, *, alpha):
    e = pl.program_id(1)

    @pl.when(e == 0)
    def _():
        xb_ref[...] = x_ref[...].astype(BF16)
        acc_ref[...] = jnp.zeros(acc_ref.shape, F32)

    xb = xb_ref[...]
    gates = gates_ref[...]
    lane = lax.broadcasted_iota(jnp.int32, gates.shape, 1)
    gate = jnp.sum(jnp.where(lane == e, gates, 0.0), axis=1, keepdims=True)
    hg = _dot(xb, wg_ref[0])
    hu = _dot(xb, wu_ref[0])
    h = hg * jax.nn.sigmoid(hg) * hu * gate
    acc_ref[...] += _dot(h.astype(BF16), wd_ref[0])

    @pl.when(e == N_EXPERTS - 1)
    def _():
        o_ref[...] = _layer_norm(alpha * x_ref[...] + acc_ref[...], g_ref[...], b_ref[...])


def _moe(x1, gates, wg, wu, wd, g, bb, alpha, tm):
    n, d = x1.shape
    nt = n // tm
    f = wg.shape[2]
    full = lambda a: pl.BlockSpec(a.shape, lambda t, e: (0,) * a.ndim)
    return pl.pallas_call(
        functools.partial(_moe_kernel, alpha=alpha),
        grid=(nt, N_EXPERTS),
        in_specs=[pl.BlockSpec((tm, d), lambda t, e: (t, 0)), pl.BlockSpec((tm, N_EXPERTS), lambda t, e: (t, 0)),
                  pl.BlockSpec((1, d, f), lambda t, e: (e, 0, 0)), pl.BlockSpec((1, d, f), lambda t, e: (e, 0, 0)),
                  pl.BlockSpec((1, f, d), lambda t, e: (e, 0, 0)), full(g), full(bb)],
        out_specs=pl.BlockSpec((tm, d), lambda t, e: (t, 0)),
        out_shape=jax.ShapeDtypeStruct((n, d), F32),
        scratch_shapes=[pltpu.VMEM((tm, d), BF16), pltpu.VMEM((tm, d), F32)],
        compiler_params=pltpu.CompilerParams(dimension_semantics=("parallel", "arbitrary"),
                                             vmem_limit_bytes=VMEM_LIMIT),
    )(x1, gates, wg, wu, wd, g, bb)


def kernel(x, w_in, attn_sinks, c_q_norm_g, c_kv_norm_g, w_uq, w_ukv, w_out, ln1_g, ln1_b, w_router, router_bias,
           w_gate, w_up, w_down, ln2_g, ln2_b):
    b, s, d = x.shape
    depth = w_in.shape[0]
    n = b * s
    alpha = (2 * depth) ** 0.25
    tm = min(512, s)
    tables = _rope_tables(s)
    wr_t = w_router.T
    rbias = router_bias.reshape(N_EXPERTS, 1)
    x2 = x.reshape(n, d)
    for l in range(depth):
        w_p, w_t = _pack_w_in(w_in[l])
        wuq_p, wukv_p = _pack_mla_w(w_uq[l], w_ukv[l])
        aq, ak, bq, bk, ik, iq, av, vt, iwt, qc, kc, vc = _inproj(
            x2, w_p, w_t, tables, c_q_norm_g[l].reshape(1, -1), c_kv_norm_g[l].reshape(1, -1), wuq_p, wukv_p, s, tm)
        oa = _swa(attn_sinks[l], aq, ak, av, b, s)
        ob = _dsa(bq, iq, iwt, bk, ik, vt, b, s)
        oc = _mla(qc, kc, vc, b, s)
        x1, gates_t = _outproj(x2, oa, ob, oc, w_out[l].astype(BF16), ln1_g[l].reshape(1, d), ln1_b[l].reshape(1, d),
                               wr_t, rbias, alpha, tm)
        x2 = _moe(x1, gates_t.T, w_gate[l].astype(BF16), w_up[l].astype(BF16), w_down[l].astype(BF16),
                  ln2_g[l].reshape(1, d), ln2_b[l].reshape(1, d), alpha, tm)
    return x2.reshape(b, s, d)
```

```python
import functools
import math

import jax
import jax.numpy as jnp
import numpy as np
from jax import lax
from jax.experimental import pallas as pl
from jax.experimental.pallas import tpu as pltpu

HEAD_DIM = 64
ROPE_THETA = 10000.0
A_HEADS = 8
A_KV_HEADS = 2
A_WINDOW = 128
B_HEADS = 4
IDX_HEADS = 4
IDX_DIM = 64
IDX_TOPK_MAX = 256
C_HEADS = 4
C_NOPE = 64
C_ROPE = 32
C_V = 64
C_Q_RANK = 256
C_KV_RANK = 128
N_EXPERTS = 16
N_GROUPS = 4
EXPERTS_PER_GROUP = 4
D_EXPERT = 256
LN_EPS = 1e-5
RMS_EPS = 1e-6

LANES = 128
SUBLANES = 8
NEG_BIG = -1e30
LOG2E = math.log2(math.e)
INT_MIN = -(2 ** 31)
VMEM_LIMIT = 56 * 1024 * 1024

BF16 = jnp.bfloat16
F32 = jnp.float32

_NT = (((1,), (1,)), ((), ()))


def _dot(a, b):
    return jnp.dot(a, b, preferred_element_type=F32)


def _dot_nt(a, b):
    return lax.dot_general(a, b, _NT, preferred_element_type=F32)


N_ROPE_UNITS = 10


def _inproj_kernel(x_ref, w_ref, ch_ref, s1h_ref, s2h_ref, cr_ref, s1r_ref, s2r_ref, gq_ref, gkv_ref,
                   wuq_ref, wukv_ref, wt_ref,
                   aq_ref, ak_ref, bq_ref, bk_ref, ik_ref, iq_ref, av_ref, vt_ref, iwt_ref, qc_ref, kc_ref, vc_ref,
                   *, b_scale, c_scale):
    xb = x_ref[...].astype(BF16)
    tm = xb.shape[0]
    lane = lax.broadcasted_iota(jnp.int32, (tm, LANES), 1)
    ch, s1h, s2h = ch_ref[...], s1h_ref[...], s2h_ref[...]
    cr, s1r, s2r = cr_ref[...], s1r_ref[...], s2r_ref[...]
    low = lane < 64

    def rope_h(u):
        return u * ch + pltpu.roll(u, 96, 1) * s1h + pltpu.roll(u, 32, 1) * s2h

    def rope_r(u):
        return u * cr + pltpu.roll(u, 112, 1) * s1r + pltpu.roll(u, 16, 1) * s2r

    def split(u):
        return jnp.where(low, u, 0.0), jnp.where(low, pltpu.roll(u, 64, 1), 0.0)

    def put(ref, k, u):
        ref[:, k * LANES:(k + 1) * LANES] = u.astype(BF16)

    for g in range(N_ROPE_UNITS // 2):
        hg = _dot(xb, w_ref[:, g * 256:(g + 1) * 256])
        for half in range(2):
            unit = 2 * g + half
            u = rope_h(hg[:, half * LANES:(half + 1) * LANES])
            if unit < 4:
                put(aq_ref, unit, u)
            elif unit == 4:
                put(ak_ref, 0, u)
            elif unit < 7:
                h0, h1 = split(u * b_scale)
                put(bq_ref, 2 * (unit - 5), h0)
                put(bq_ref, 2 * (unit - 5) + 1, h1)
            elif unit == 7:
                h0, h1 = split(u)
                put(bk_ref, 0, h0)
                put(ik_ref, 0, h1)
            else:
                h0, h1 = split(u)
                put(iq_ref, 2 * (unit - 8), h0)
                put(iq_ref, 2 * (unit - 8) + 1, h1)

    hv = _dot(xb, w_ref[:, 1280:1536])
    for k in range(2):
        u = hv[:, k * LANES:(k + 1) * LANES]
        av_ref[:, k * LANES:(k + 1) * LANES] = jnp.where(lane == 64, 1.0, u).astype(BF16)

    hb = _dot_nt(wt_ref[...], xb)
    sub = lax.broadcasted_iota(jnp.int32, (LANES, tm), 0)
    vt = jnp.where(sub == 64, 1.0, hb[:LANES]).astype(BF16)
    for j in range(tm // ATT_KC):
        vt_ref[j] = vt[:, j * ATT_KC:(j + 1) * ATT_KC]
    iwt_ref[...] = hb[LANES:LANES + 8]

    cq = _dot(xb, w_ref[:, 1536:1792])
    cqn = cq * lax.rsqrt(jnp.mean(cq * cq, axis=1, keepdims=True) + RMS_EPS) * gq_ref[...]
    qc = _dot(cqn.astype(BF16), wuq_ref[...])
    for h in range(C_HEADS):
        u = qc[:, h * LANES:(h + 1) * LANES]
        qc_ref[:, h * LANES:(h + 1) * LANES] = (rope_r(u) * c_scale).astype(BF16)

    ckv = _dot(xb, w_ref[:, 1792:2048])
    ckv_lat = ckv[:, :LANES]
    kr = rope_r(ckv[:, LANES:])
    ckvn = ckv_lat * lax.rsqrt(jnp.mean(ckv_lat * ckv_lat, axis=1, keepdims=True) + RMS_EPS) * gkv_ref[...]
    kv = _dot(ckvn.astype(BF16), wukv_ref[...])
    for h in range(C_HEADS):
        kc_ref[:, h * LANES:(h + 1) * LANES] = (kv[:, h * LANES:(h + 1) * LANES] + kr).astype(BF16)
        v = kv[:, (C_HEADS + h) * LANES:(C_HEADS + h + 1) * LANES]
        vc_ref[:, h * LANES:(h + 1) * LANES] = jnp.where(lane == 64, 1.0, v).astype(BF16)


def _pack_w_in(w):
    d = w.shape[0]
    z = lambda n: jnp.zeros((d, n), F32)
    o = np.cumsum([0, 512, 128, 128, 256, 64, 64, 256, 64, 4, 256, 128, 32]).tolist()
    a_q, a_k, a_v, b_q, b_k, b_v, i_q, i_k, i_w, c_q, c_kv, c_kr = [w[:, o[j]:o[j + 1]] for j in range(12)]
    qs = HEAD_DIM ** -0.5
    ws = (IDX_HEADS * IDX_DIM) ** -0.5
    cols = [a_q * qs, a_k, b_q, b_k, i_k, i_q,
            a_v[:, :64], z(64), a_v[:, 64:], z(64),
            c_q, c_kv, z(64), c_kr, z(32)]
    rows_t = jnp.concatenate([b_v, z(64), i_w * ws, z(LANES - IDX_HEADS)], axis=1).T
    return jnp.concatenate(cols, axis=1).astype(BF16), rows_t.astype(BF16)


def _pack_mla_w(w_uq, w_ukv):
    r = w_uq.shape[0]
    q = w_uq.reshape(r, C_HEADS, C_NOPE + C_ROPE)
    q = jnp.pad(q, ((0, 0), (0, 0), (0, LANES - C_NOPE - C_ROPE))).reshape(r, C_HEADS * LANES)
    r2 = w_ukv.shape[0]
    kv = w_ukv.reshape(r2, C_HEADS, C_NOPE + C_V)
    k = jnp.pad(kv[:, :, :C_NOPE], ((0, 0), (0, 0), (0, LANES - C_NOPE))).reshape(r2, C_HEADS * LANES)
    v = jnp.pad(kv[:, :, C_NOPE:], ((0, 0), (0, 0), (0, LANES - C_V))).reshape(r2, C_HEADS * LANES)
    return q.astype(BF16), jnp.concatenate([k, v], axis=1).astype(BF16)


def _rope_tables(s):
    pos = jnp.arange(s, dtype=F32)[:, None]
    lane = np.arange(LANES)
    inv_h = 1.0 / (ROPE_THETA ** (jnp.arange(0, HEAD_DIM, 2, dtype=F32) / HEAD_DIM))
    ang = pos * inv_h[None, :]
    cos, sin = jnp.cos(ang), jnp.sin(ang)
    j = lane % 32
    lo = jnp.asarray((lane % 64) < 32)
    ch = cos[:, j]
    s1h = jnp.where(lo, -sin[:, j], 0.0)
    s2h = jnp.where(lo, 0.0, sin[:, j])
    inv_r = 1.0 / (ROPE_THETA ** (jnp.arange(0, C_ROPE, 2, dtype=F32) / C_ROPE))
    angr = pos * inv_r[None, :]
    cosr, sinr = jnp.cos(angr), jnp.sin(angr)
    jr = lane % 16
    in_rope = jnp.asarray((lane >= 64) & (lane < 96))
    first = jnp.asarray((lane >= 64) & (lane < 80))
    second = jnp.asarray((lane >= 80) & (lane < 96))
    cr = jnp.where(in_rope, cosr[:, jr], 1.0)
    s1r = jnp.where(first, -sinr[:, jr], 0.0)
    s2r = jnp.where(second, sinr[:, jr], 0.0)
    return ch, s1h, s2h, cr, s1r, s2r


def _inproj(x2, w_p, w_t, tables, gq, gkv, wuq_p, wukv_p, s, tm):
    n, d = x2.shape
    nt = n // tm
    spt = s // tm
    cpt = tm // ATT_KC
    tab_spec = pl.BlockSpec((tm, LANES), lambda t: (t % spt, 0))
    full = lambda a: pl.BlockSpec(a.shape, lambda t: (0,) * a.ndim)
    row = lambda c: pl.BlockSpec((tm, c), lambda t: (t, 0))
    cols_a, cols_c = [512, 128, 512, 128, 128, 512, 256], [512, 512, 512]
    out_specs = ([row(c) for c in cols_a]
                 + [pl.BlockSpec((cpt, LANES, ATT_KC), lambda t: (t, 0, 0)), pl.BlockSpec((8, tm), lambda t: (0, t))]
                 + [row(c) for c in cols_c])
    out_shape = ([jax.ShapeDtypeStruct((n, c), BF16) for c in cols_a]
                 + [jax.ShapeDtypeStruct((n // ATT_KC, LANES, ATT_KC), BF16), jax.ShapeDtypeStruct((8, n), F32)]
                 + [jax.ShapeDtypeStruct((n, c), BF16) for c in cols_c])
    return pl.pallas_call(
        functools.partial(_inproj_kernel, b_scale=HEAD_DIM ** -0.5 * LOG2E,
                          c_scale=(C_NOPE + C_ROPE) ** -0.5 * LOG2E),
        grid=(nt,),
        in_specs=[row(d), full(w_p)] + [tab_spec] * 6 + [full(gq), full(gkv), full(wuq_p), full(wukv_p), full(w_t)],
        out_specs=out_specs,
        out_shape=out_shape,
        compiler_params=pltpu.CompilerParams(dimension_semantics=("parallel",), vmem_limit_bytes=VMEM_LIMIT),
    )(x2, w_p, *tables, gq, gkv, wuq_p, wukv_p, w_t)


def _swa_kernel(sink_ref, q_ref, kc_ref, kp_ref, vc_ref, vp_ref, o_ref):
    i = pl.program_id(1)
    w = A_WINDOW
    qi = lax.broadcasted_iota(jnp.int32, (w, 2 * w), 0)
    kj = lax.broadcasted_iota(jnp.int32, (w, 2 * w), 1)
    diff = qi + w - kj
    ok = (diff >= 0) & (diff < w) & ((kj >= w) | (i > 0))
    bias = jnp.where(ok, 0.0, NEG_BIG)
    group = A_HEADS // A_KV_HEADS
    for g in range(A_KV_HEADS):
        k2 = jnp.concatenate([kp_ref[:, g * 64:(g + 1) * 64], kc_ref[:, g * 64:(g + 1) * 64]], axis=0)
        v2 = jnp.concatenate([vp_ref[:, g * LANES:(g + 1) * LANES], vc_ref[:, g * LANES:(g + 1) * LANES]], axis=0)
        for hh in range(group):
            h = g * group + hh
            s = _dot_nt(q_ref[:, h * 64:(h + 1) * 64], k2) + bias
            sink = sink_ref[h]
            m = jnp.maximum(jnp.max(s, axis=1, keepdims=True), sink)
            p = jnp.exp(s - m)
            o = _dot(p.astype(BF16), v2)
            denom = o[:, 64:65] + jnp.exp(sink - m)
            o_ref[:, h * 64:(h + 1) * 64] = (o[:, :64] / denom).astype(BF16)


def _swa(sinks, aq, ak, av, b, s):
    n = aq.shape[0]
    w = A_WINDOW
    nb = s // w
    cur = lambda bb, i: (bb * nb + i, 0)
    prev = lambda bb, i: (bb * nb + jnp.maximum(i - 1, 0), 0)
    return pl.pallas_call(
        _swa_kernel,
        grid=(b, nb),
        in_specs=[pl.BlockSpec(memory_space=pltpu.SMEM),
                  pl.BlockSpec((w, 512), cur),
                  pl.BlockSpec((w, 128), cur), pl.BlockSpec((w, 128), prev),
                  pl.BlockSpec((w, 256), cur), pl.BlockSpec((w, 256), prev)],
        out_specs=pl.BlockSpec((w, 512), cur),
        out_shape=jax.ShapeDtypeStruct((n, 512), BF16),
        compiler_params=pltpu.CompilerParams(dimension_semantics=("parallel", "parallel"),
                                             vmem_limit_bytes=VMEM_LIMIT),
    )(sinks, aq, ak, ak, av, av)


ATT_KC = 256


def _att_qb(s):
    return min(512, s)


def _fold_max(s):
    return jnp.maximum(s[:, :LANES], s[:, LANES:])


def _dsa_kernel(bq_ref, iq_ref, iwt_ref, bk_ref, ik_ref, vt_ref, o_ref,
                sc_ref, s_ref, tau_ref, j_ref, ist_ref, qst_ref, m_ref, acc_ref, *, topk, idx_bits):
    i = pl.program_id(1)
    qb = bq_ref.shape[0]
    kc = ATT_KC
    per = qb // kc
    n_full = per * i
    n_chunks = n_full + per
    kidx = lax.broadcasted_iota(jnp.int32, (kc, qb), 0)
    rpos = lax.broadcasted_iota(jnp.int32, (kc, qb), 1)
    neg_inf = float("-inf")
    kstart = lambda c: pl.multiple_of(c * kc, kc)
    heads = range(B_HEADS)
    hrows = lambda h: slice(h * qb, (h + 1) * qb)
    groups = [slice(g * SUBLANES, (g + 1) * SUBLANES) for g in range(kc // SUBLANES)]

    def fold(x, op):
        out = x[groups[0]]
        for g in groups[1:]:
            out = op(out, x[g])
        return out

    for h in range(IDX_HEADS):
        ist_ref[hrows(h), :] = iq_ref[:, h * LANES:(h + 1) * LANES]

    def score(c):
        lg = _dot_nt(ik_ref[pl.ds(kstart(c), kc), :], ist_ref[...])
        acc = None
        for h in range(IDX_HEADS):
            t = iwt_ref[h:h + 1, :] * jnp.maximum(lg[:, hrows(h)], 0.0)
            acc = t if acc is None else acc + t
        return acc

    def idx_body(c, carry):
        sc_ref[c] = score(c)
        return carry

    lax.fori_loop(0, n_full, idx_body, 0)
    for d in range(per):
        sc_ref[n_full + d] = jnp.where(d * kc + kidx <= rpos, score(n_full + d), neg_inf)

    rw = qb // per
    kidx_r = lax.broadcasted_iota(jnp.int32, (kc, rw), 0)
    for r in range(per):
        rows = slice(r * rw, (r + 1) * rw)
        tpos = i * qb + r * rw + lax.broadcasted_iota(jnp.int32, (1, rw), 1)
        n_r = n_full + r + 1

        def count(pred_fn):
            def cbody(c, acc):
                k = sc_ref[c, :, rows]
                return acc + fold(jnp.where(pred_fn(k, c), 1.0, 0.0), jnp.add)
            acc = lax.fori_loop(0, n_r, cbody, jnp.zeros((SUBLANES, rw), F32))
            return jnp.sum(acc, axis=0, keepdims=True)

        def key_to_f32(key_u):
            key = key_u ^ INT_MIN
            return pltpu.bitcast(key ^ ((key >> 31) & 0x7FFFFFFF), F32)

        def bit_body(step, prefix):
            cand_u = prefix | lax.shift_left(jnp.int32(1), 31 - step)
            cand = key_to_f32(cand_u)
            cnt = count(lambda k, c: k >= cand)
            return jnp.where(cnt >= topk, cand_u, prefix)

        prefix = lax.fori_loop(0, 32, bit_body, jnp.zeros((1, rw), jnp.int32))
        tau = key_to_f32(prefix)
        tau = jnp.where(tau != tau, neg_inf, tau)
        c_gt = count(lambda k, c: k > tau)
        c_eq = count(lambda k, c: k == tau)
        need = topk - c_gt

        def tie_search(_):
            def jbody(step, q):
                cand = q | lax.shift_left(jnp.int32(1), idx_bits - 1 - step)
                cnt = count(lambda k, c: (k == tau) & (c * kc + kidx_r < cand))
                return jnp.where(cnt < need, cand, q)
            return lax.fori_loop(0, idx_bits, jbody, jnp.zeros((1, rw), jnp.int32))

        any_split = jnp.max(c_eq - need) > 0.0
        jcut = lax.cond(any_split, tie_search, lambda _: jnp.full((1, rw), 2 ** idx_bits, jnp.int32), 0)
        few = tpos < int(topk)
        tau_ref[:, rows] = jnp.broadcast_to(jnp.where(few, neg_inf, tau), (SUBLANES, rw))
        j_ref[:, rows] = jnp.broadcast_to(jnp.where(few, tpos, jcut), (SUBLANES, rw))

    for h in heads:
        qst_ref[hrows(h), :] = bq_ref[:, h * LANES:(h + 1) * LANES]
    m_ref[...] = jnp.full(m_ref.shape, NEG_BIG, F32)

    def max_body(c, carry):
        s = _dot_nt(bk_ref[pl.ds(kstart(c), kc), :], qst_ref[...])
        k = sc_ref[c]
        tau = tau_ref[0:1, :]
        sel = (k > tau) | ((k == tau) & (c * kc + kidx <= j_ref[0:1, :]))
        bias = jnp.where(sel, 0.0, NEG_BIG)
        for h in heads:
            sm = s[:, hrows(h)] + bias
            s_ref[c, :, hrows(h)] = sm
            m_ref[:, hrows(h)] = jnp.maximum(m_ref[:, hrows(h)], fold(sm, jnp.maximum))
        return carry

    lax.fori_loop(0, n_chunks, max_body, 0)
    m_ref[...] = jnp.broadcast_to(jnp.max(m_ref[...], axis=0, keepdims=True), m_ref.shape)

    acc_ref[...] = jnp.zeros(acc_ref.shape, F32)

    def pv_body(c, carry):
        p = jnp.exp2(s_ref[c] - m_ref[0:1, :]).astype(BF16)
        acc_ref[...] += _dot(vt_ref[c], p)
        return carry

    lax.fori_loop(0, n_chunks, pv_body, 0)
    for h in heads:
        acc = acc_ref[:, hrows(h)]
        o_ref[:, h * 64:(h + 1) * 64] = (acc / acc[64:65, :]).T[:, :64].astype(BF16)


def _dsa(bq, iq, iwt, bk, ik, vt, b, s):
    n = bq.shape[0]
    qb = _att_qb(s)
    nb = s // qb
    nkc = s // ATT_KC
    topk = min(IDX_TOPK_MAX, s // 4)
    idx_bits = max(1, int(math.ceil(math.log2(s))))
    blk = lambda bb, i: (bb * nb + i, 0)
    seq = lambda bb, i: (bb, 0)
    return pl.pallas_call(
        functools.partial(_dsa_kernel, topk=float(topk), idx_bits=idx_bits),
        grid=(b, nb),
        in_specs=[pl.BlockSpec((qb, 512), blk), pl.BlockSpec((qb, 512), blk),
                  pl.BlockSpec((8, qb), lambda bb, i: (0, bb * nb + i)),
                  pl.BlockSpec((s, 128), seq), pl.BlockSpec((s, 128), seq),
                  pl.BlockSpec((nkc, LANES, ATT_KC), lambda bb, i: (bb, 0, 0))],
        out_specs=pl.BlockSpec((qb, 256), blk),
        out_shape=jax.ShapeDtypeStruct((n, 256), BF16),
        scratch_shapes=[pltpu.VMEM((nkc, ATT_KC, qb), F32),
                        pltpu.VMEM((nkc, ATT_KC, B_HEADS * qb), F32),
                        pltpu.VMEM((SUBLANES, qb), F32), pltpu.VMEM((SUBLANES, qb), jnp.int32),
                        pltpu.VMEM((IDX_HEADS * qb, LANES), BF16), pltpu.VMEM((B_HEADS * qb, LANES), BF16),
                        pltpu.VMEM((SUBLANES, B_HEADS * qb), F32), pltpu.VMEM((LANES, B_HEADS * qb), F32)],
        compiler_params=pltpu.CompilerParams(dimension_semantics=("parallel", "arbitrary"),
                                             vmem_limit_bytes=VMEM_LIMIT),
    )(bq, iq, iwt, bk, ik, vt)


def _mla_kernel(q_ref, k_ref, v_ref, o_ref, s_ref, m_ref, acc_ref):
    i = pl.program_id(1)
    qb = q_ref.shape[0]
    kc = ATT_KC
    per = qb // kc
    n_full = per * i
    n_chunks = n_full + per
    row = lax.broadcasted_iota(jnp.int32, (qb, kc), 0)
    col = lax.broadcasted_iota(jnp.int32, (qb, kc), 1)
    kstart = lambda c: pl.multiple_of(c * kc, kc)
    heads = range(C_HEADS)
    hs = lambda h: slice(h * LANES, (h + 1) * LANES)
    n_halves = kc // LANES
    m_ref[...] = jnp.full(m_ref.shape, NEG_BIG, F32)

    def max_step(c, bias):
        for h in heads:
            s = _dot_nt(q_ref[:, hs(h)], k_ref[pl.ds(kstart(c), kc), hs(h)])
            if bias is not None:
                s = s + bias
            s_ref[h, c] = s
            m_ref[h] = jnp.maximum(m_ref[h], _fold_max(s))

    def max_body(c, carry):
        max_step(c, None)
        return carry

    lax.fori_loop(0, n_full, max_body, 0)
    for d in range(per):
        max_step(n_full + d, jnp.where(d * kc + col <= row, 0.0, NEG_BIG))
    for h in heads:
        m_ref[h] = jnp.broadcast_to(jnp.max(m_ref[h], axis=1, keepdims=True), (qb, LANES))
    acc_ref[...] = jnp.zeros(acc_ref.shape, F32)

    def pv_body(c, carry):
        for h in heads:
            p = jnp.exp2(s_ref[h, c] - jnp.concatenate([m_ref[h]] * n_halves, axis=1))
            acc_ref[h] += _dot(p.astype(BF16), v_ref[pl.ds(kstart(c), kc), hs(h)])
        return carry

    lax.fori_loop(0, n_chunks, pv_body, 0)
    for h in heads:
        acc = acc_ref[h]
        o_ref[:, h * 64:(h + 1) * 64] = (acc[:, :64] / acc[:, 64:65]).astype(BF16)


def _mla(qc, kc, vc, b, s):
    n = qc.shape[0]
    qb = _att_qb(s)
    nb = s // qb
    nkc = s // ATT_KC
    blk = lambda bb, i: (bb * nb + i, 0)
    seq = lambda bb, i: (bb, 0)
    return pl.pallas_call(
        _mla_kernel,
        grid=(b, nb),
        in_specs=[pl.BlockSpec((qb, 512), blk), pl.BlockSpec((s, 512), seq), pl.BlockSpec((s, 512), seq)],
        out_specs=pl.BlockSpec((qb, 256), blk),
        out_shape=jax.ShapeDtypeStruct((n, 256), BF16),
        scratch_shapes=[pltpu.VMEM((C_HEADS, nkc, qb, ATT_KC), F32), pltpu.VMEM((C_HEADS, qb, LANES), F32),
                        pltpu.VMEM((C_HEADS, qb, LANES), F32)],
        compiler_params=pltpu.CompilerParams(dimension_semantics=("parallel", "arbitrary"),
                                             vmem_limit_bytes=VMEM_LIMIT),
    )(qc, kc, vc)


def _layer_norm(z, g, b):
    mu = jnp.mean(z, axis=1, keepdims=True)
    zc = z - mu
    var = jnp.mean(zc * zc, axis=1, keepdims=True)
    return zc * lax.rsqrt(var + LN_EPS) * g + b


def _outproj_kernel(x_ref, oa_ref, ob_ref, oc_ref, wo_ref, g_ref, b_ref, wr_ref, rb_ref, x1_ref, gates_ref,
                    *, alpha):
    y = _dot(oa_ref[...], wo_ref[0:512, :]) + _dot(ob_ref[...], wo_ref[512:768, :]) \
        + _dot(oc_ref[...], wo_ref[768:1024, :])
    x1 = _layer_norm(alpha * x_ref[...] + y, g_ref[...], b_ref[...])
    x1_ref[...] = x1

    x1h = x1.astype(BF16)
    x1l = (x1 - x1h.astype(F32)).astype(BF16)
    wr = wr_ref[...]
    wrh = wr.astype(BF16)
    wrl = (wr - wrh.astype(F32)).astype(BF16)
    hi = _dot_nt(jnp.concatenate([wrh, wrl], axis=0), x1h)
    logits = hi[:N_EXPERTS] + hi[N_EXPERTS:] + _dot_nt(wrh, x1l)
    scores = jax.nn.sigmoid(logits)
    biased = scores + rb_ref[...]
    rows = [biased[e:e + 1, :] for e in range(N_EXPERTS)]
    gscore = []
    for g in range(N_GROUPS):
        v = rows[g * 4:(g + 1) * 4]
        best = None
        for a in range(4):
            for c in range(a + 1, 4):
                pair = v[a] + v[c]
                best = pair if best is None else jnp.maximum(best, pair)
        gscore.append(best)
    gmax = jnp.maximum(jnp.maximum(gscore[0], gscore[1]), jnp.maximum(gscore[2], gscore[3]))
    taken = jnp.zeros_like(gmax) > 1.0
    sel_rows = []
    for g in range(N_GROUPS):
        g_sel = (gscore[g] == gmax) & jnp.logical_not(taken)
        taken = taken | g_sel
        v = rows[g * 4:(g + 1) * 4]
        for e in range(4):
            rank = jnp.zeros_like(gmax)
            for j in range(4):
                if j == e:
                    continue
                ahead = (v[j] > v[e]) | ((v[j] == v[e]) & (j < e))
                rank = rank + jnp.where(ahead, 1.0, 0.0)
            sel_rows.append(g_sel & (rank < 2.0))
    wsel = [jnp.where(sel_rows[e], scores[e:e + 1, :], 0.0) for e in range(N_EXPERTS)]
    total = wsel[0]
    for e in range(1, N_EXPERTS):
        total = total + wsel[e]
    gates_ref[...] = jnp.concatenate(wsel, axis=0) / total


def _outproj(x2, oa, ob, oc, wo, g, bb, wr_t, rbias, alpha, tm):
    n, d = x2.shape
    nt = n // tm
    full = lambda a: pl.BlockSpec(a.shape, lambda t: (0,) * a.ndim)
    row = lambda c: pl.BlockSpec((tm, c), lambda t: (t, 0))
    return pl.pallas_call(
        functools.partial(_outproj_kernel, alpha=alpha),
        grid=(nt,),
        in_specs=[row(d), row(512), row(256), row(256), full(wo), full(g), full(bb), full(wr_t), full(rbias)],
        out_specs=[row(d), pl.BlockSpec((N_EXPERTS, tm), lambda t: (0, t))],
        out_shape=[jax.ShapeDtypeStruct((n, d), F32), jax.ShapeDtypeStruct((N_EXPERTS, n), F32)],
        compiler_params=pltpu.CompilerParams(dimension_semantics=("parallel",), vmem_limit_bytes=VMEM_LIMIT),
    )(x2, oa, ob, oc, wo, g, bb, wr_t, rbias)


def _moe_kernel(x_ref, gates_ref, wg_ref, wu_ref, wd_ref, g_ref, b_ref, o_ref, xb_ref, acc_ref, *, alpha):
    grp = pl.program_id(1)

    @pl.when(grp == 0)
    def _():
        xb_ref[...] = x_ref[...].astype(BF16)

    xb = xb_ref[...]
    hs = []
    for j in range(EXPERTS_PER_GROUP):
        hg = _dot(xb, wg_ref[j])
        hu = _dot(xb, wu_ref[j])
        hs.append((hg * jax.nn.sigmoid(hg) * hu * gates_ref[0, :, j:j + 1]).astype(BF16))
    wd = wd_ref[...]
    y = _dot(jnp.concatenate(hs, axis=1), wd.reshape(wd.shape[0] * wd.shape[1], wd.shape[2]))

    @pl.when(grp == 0)
    def _():
        acc_ref[...] = y

    @pl.when(grp > 0)
    def _():
        acc_ref[...] += y

    @pl.when(grp == N_GROUPS - 1)
    def _():
        o_ref[...] = _layer_norm(alpha * x_ref[...] + acc_ref[...], g_ref[...], b_ref[...])


def _moe(x1, gates_g, wg, wu, wd, g, bb, alpha, tm):
    n, d = x1.shape
    nt = n // tm
    f = wg.shape[2]
    epg = EXPERTS_PER_GROUP
    full = lambda a: pl.BlockSpec(a.shape, lambda t, e: (0,) * a.ndim)
    return pl.pallas_call(
        functools.partial(_moe_kernel, alpha=alpha),
        grid=(nt, N_GROUPS),
        in_specs=[pl.BlockSpec((tm, d), lambda t, e: (t, 0)), pl.BlockSpec((1, tm, epg), lambda t, e: (e, t, 0)),
                  pl.BlockSpec((epg, d, f), lambda t, e: (e, 0, 0)), pl.BlockSpec((epg, d, f), lambda t, e: (e, 0, 0)),
                  pl.BlockSpec((epg, f, d), lambda t, e: (e, 0, 0)), full(g), full(bb)],
        out_specs=pl.BlockSpec((tm, d), lambda t, e: (t, 0)),
        out_shape=jax.ShapeDtypeStruct((n, d), F32),
        scratch_shapes=[pltpu.VMEM((tm, d), BF16), pltpu.VMEM((tm, d), F32)],
        compiler_params=pltpu.CompilerParams(dimension_semantics=("parallel", "arbitrary"),
                                             vmem_limit_bytes=VMEM_LIMIT),
    )(x1, gates_g, wg, wu, wd, g, bb)


def kernel(x, w_in, attn_sinks, c_q_norm_g, c_kv_norm_g, w_uq, w_ukv, w_out, ln1_g, ln1_b, w_router, router_bias,
           w_gate, w_up, w_down, ln2_g, ln2_b):
    b, s, d = x.shape
    depth = w_in.shape[0]
    n = b * s
    alpha = (2 * depth) ** 0.25
    tm = min(512, s)
    tables = _rope_tables(s)
    wr_t = w_router.T
    rbias = router_bias.reshape(N_EXPERTS, 1)
    x2 = x.reshape(n, d)
    for l in range(depth):
        w_p, w_t = _pack_w_in(w_in[l])
        wuq_p, wukv_p = _pack_mla_w(w_uq[l], w_ukv[l])
        aq, ak, bq, bk, ik, iq, av, vt, iwt, qc, kc, vc = _inproj(
            x2, w_p, w_t, tables, c_q_norm_g[l].reshape(1, -1), c_kv_norm_g[l].reshape(1, -1), wuq_p, wukv_p, s, tm)
        oa = _swa(attn_sinks[l], aq, ak, av, b, s)
        ob = _dsa(bq, iq, iwt, bk, ik, vt, b, s)
        oc = _mla(qc, kc, vc, b, s)
        x1, gates_t = _outproj(x2, oa, ob, oc, w_out[l].astype(BF16), ln1_g[l].reshape(1, d), ln1_b[l].reshape(1, d),
                               wr_t, rbias, alpha, tm)
        gates_g = gates_t.reshape(N_GROUPS, EXPERTS_PER_GROUP, n).transpose(0, 2, 1)
        x2 = _moe(x1, gates_g, w_gate[l].astype(BF16), w_up[l].astype(BF16), w_down[l].astype(BF16),
                  ln2_g[l].reshape(1, d), ln2_b[l].reshape(1, d), alpha, min(1024, n))
    return x2.reshape(b, s, d)
```

```python
import functools
import math

import jax
import jax.numpy as jnp
import numpy as np
from jax import lax
from jax.experimental import pallas as pl
from jax.experimental.pallas import tpu as pltpu

HEAD_DIM = 64
ROPE_THETA = 10000.0
A_HEADS = 8
A_KV_HEADS = 2
A_WINDOW = 128
B_HEADS = 4
IDX_HEADS = 4
IDX_DIM = 64
IDX_TOPK_MAX = 256
C_HEADS = 4
C_NOPE = 64
C_ROPE = 32
C_V = 64
C_Q_RANK = 256
C_KV_RANK = 128
N_EXPERTS = 16
N_GROUPS = 4
EXPERTS_PER_GROUP = 4
D_EXPERT = 256
LN_EPS = 1e-5
RMS_EPS = 1e-6

LANES = 128
SUBLANES = 8
NEG_BIG = -1e30
LOG2E = math.log2(math.e)
INT_MIN = -(2 ** 31)
VMEM_LIMIT = 56 * 1024 * 1024

BF16 = jnp.bfloat16
F32 = jnp.float32

_NT = (((1,), (1,)), ((), ()))


def _dot(a, b):
    return jnp.dot(a, b, preferred_element_type=F32)


def _dot_nt(a, b):
    return lax.dot_general(a, b, _NT, preferred_element_type=F32)


N_ROPE_UNITS = 10


def _inproj_kernel(x_ref, w_ref, ch_ref, s1h_ref, s2h_ref, cr_ref, s1r_ref, s2r_ref, gq_ref, gkv_ref,
                   wuq_ref, wukv_ref, wt_ref,
                   aq_ref, ak_ref, bq_ref, bk_ref, ik_ref, iq_ref, avt_ref, vt_ref, iwt_ref, qc_ref, kc_ref, vc_ref,
                   *, b_scale, c_scale):
    xb = x_ref[...].astype(BF16)
    tm = xb.shape[0]
    lane = lax.broadcasted_iota(jnp.int32, (tm, LANES), 1)
    ch, s1h, s2h = ch_ref[...], s1h_ref[...], s2h_ref[...]
    cr, s1r, s2r = cr_ref[...], s1r_ref[...], s2r_ref[...]
    low = lane < 64

    def rope_h(u):
        return u * ch + pltpu.roll(u, 96, 1) * s1h + pltpu.roll(u, 32, 1) * s2h

    def rope_r(u):
        return u * cr + pltpu.roll(u, 112, 1) * s1r + pltpu.roll(u, 16, 1) * s2r

    def split(u):
        return jnp.where(low, u, 0.0), jnp.where(low, pltpu.roll(u, 64, 1), 0.0)

    def put(ref, k, u):
        ref[:, k * LANES:(k + 1) * LANES] = u.astype(BF16)

    for g in range(N_ROPE_UNITS // 2):
        hg = _dot(xb, w_ref[:, g * 256:(g + 1) * 256])
        for half in range(2):
            unit = 2 * g + half
            u = rope_h(hg[:, half * LANES:(half + 1) * LANES])
            if unit < 4:
                put(aq_ref, unit, u * LOG2E)
            elif unit == 4:
                swapped = pltpu.roll(u, 64, 1)
                put(ak_ref, 0, jnp.where(low, u, 0.0))
                put(ak_ref, 1, jnp.where(low, 0.0, swapped))
                put(ak_ref, 2, jnp.where(low, swapped, 0.0))
                put(ak_ref, 3, jnp.where(low, 0.0, u))
            elif unit < 7:
                h0, h1 = split(u * b_scale)
                put(bq_ref, 2 * (unit - 5), h0)
                put(bq_ref, 2 * (unit - 5) + 1, h1)
            elif unit == 7:
                h0, h1 = split(u)
                put(bk_ref, 0, h0)
                put(ik_ref, 0, h1)
            else:
                h0, h1 = split(u)
                put(iq_ref, 2 * (unit - 8), h0)
                put(iq_ref, 2 * (unit - 8) + 1, h1)

    hb = _dot_nt(wt_ref[...], xb)
    sub = lax.broadcasted_iota(jnp.int32, (LANES, tm), 0)
    with_ones = lambda piece: jnp.where(sub == 64, 1.0, hb[piece * LANES:(piece + 1) * LANES]).astype(BF16)
    vt = with_ones(0)
    for j in range(tm // ATT_KC):
        vt_ref[j] = vt[:, j * ATT_KC:(j + 1) * ATT_KC]
    iwt_ref[...] = hb[LANES:LANES + 8]
    for g in range(A_KV_HEADS):
        avt_ref[g] = with_ones(2 + g)

    cq = _dot(xb, w_ref[:, 1280:1536])
    cqn = cq * lax.rsqrt(jnp.mean(cq * cq, axis=1, keepdims=True) + RMS_EPS) * gq_ref[...]
    qc = _dot(cqn.astype(BF16), wuq_ref[...])
    for h in range(C_HEADS):
        u = qc[:, h * LANES:(h + 1) * LANES]
        qc_ref[:, h * LANES:(h + 1) * LANES] = (rope_r(u) * c_scale).astype(BF16)

    ckv = _dot(xb, w_ref[:, 1536:1792])
    ckv_lat = ckv[:, :LANES]
    kr = rope_r(ckv[:, LANES:])
    ckvn = ckv_lat * lax.rsqrt(jnp.mean(ckv_lat * ckv_lat, axis=1, keepdims=True) + RMS_EPS) * gkv_ref[...]
    kv = _dot(ckvn.astype(BF16), wukv_ref[...])
    for h in range(C_HEADS):
        kc_ref[:, h * LANES:(h + 1) * LANES] = (kv[:, h * LANES:(h + 1) * LANES] + kr).astype(BF16)
        v = kv[:, (C_HEADS + h) * LANES:(C_HEADS + h + 1) * LANES]
        vc_ref[:, h * LANES:(h + 1) * LANES] = jnp.where(lane == 64, 1.0, v).astype(BF16)


def _pack_w_in(w):
    d = w.shape[0]
    z = lambda n: jnp.zeros((d, n), F32)
    o = np.cumsum([0, 512, 128, 128, 256, 64, 64, 256, 64, 4, 256, 128, 32]).tolist()
    a_q, a_k, a_v, b_q, b_k, b_v, i_q, i_k, i_w, c_q, c_kv, c_kr = [w[:, o[j]:o[j + 1]] for j in range(12)]
    qs = HEAD_DIM ** -0.5
    ws = (IDX_HEADS * IDX_DIM) ** -0.5
    cols = [a_q * qs, a_k, b_q, b_k, i_k, i_q,
            c_q, c_kv, z(64), c_kr, z(32)]
    rows_t = jnp.concatenate([b_v, z(64), i_w * ws, z(LANES - IDX_HEADS),
                              a_v[:, :64], z(64), a_v[:, 64:], z(64)], axis=1).T
    return jnp.concatenate(cols, axis=1).astype(BF16), rows_t.astype(BF16)


def _pack_mla_w(w_uq, w_ukv):
    r = w_uq.shape[0]
    q = w_uq.reshape(r, C_HEADS, C_NOPE + C_ROPE)
    q = jnp.pad(q, ((0, 0), (0, 0), (0, LANES - C_NOPE - C_ROPE))).reshape(r, C_HEADS * LANES)
    r2 = w_ukv.shape[0]
    kv = w_ukv.reshape(r2, C_HEADS, C_NOPE + C_V)
    k = jnp.pad(kv[:, :, :C_NOPE], ((0, 0), (0, 0), (0, LANES - C_NOPE))).reshape(r2, C_HEADS * LANES)
    v = jnp.pad(kv[:, :, C_NOPE:], ((0, 0), (0, 0), (0, LANES - C_V))).reshape(r2, C_HEADS * LANES)
    return q.astype(BF16), jnp.concatenate([k, v], axis=1).astype(BF16)


def _rope_tables(s):
    pos = jnp.arange(s, dtype=F32)[:, None]
    lane = np.arange(LANES)
    inv_h = 1.0 / (ROPE_THETA ** (jnp.arange(0, HEAD_DIM, 2, dtype=F32) / HEAD_DIM))
    ang = pos * inv_h[None, :]
    cos, sin = jnp.cos(ang), jnp.sin(ang)
    j = lane % 32
    lo = jnp.asarray((lane % 64) < 32)
    ch = cos[:, j]
    s1h = jnp.where(lo, -sin[:, j], 0.0)
    s2h = jnp.where(lo, 0.0, sin[:, j])
    inv_r = 1.0 / (ROPE_THETA ** (jnp.arange(0, C_ROPE, 2, dtype=F32) / C_ROPE))
    angr = pos * inv_r[None, :]
    cosr, sinr = jnp.cos(angr), jnp.sin(angr)
    jr = lane % 16
    in_rope = jnp.asarray((lane >= 64) & (lane < 96))
    first = jnp.asarray((lane >= 64) & (lane < 80))
    second = jnp.asarray((lane >= 80) & (lane < 96))
    cr = jnp.where(in_rope, cosr[:, jr], 1.0)
    s1r = jnp.where(first, -sinr[:, jr], 0.0)
    s2r = jnp.where(second, sinr[:, jr], 0.0)
    return ch, s1h, s2h, cr, s1r, s2r


def _inproj(x2, w_p, w_t, tables, gq, gkv, wuq_p, wukv_p, s, tm):
    n, d = x2.shape
    nt = n // tm
    spt = s // tm
    cpt = tm // ATT_KC
    tab_spec = pl.BlockSpec((tm, LANES), lambda t: (t % spt, 0))
    full = lambda a: pl.BlockSpec(a.shape, lambda t: (0,) * a.ndim)
    row = lambda c: pl.BlockSpec((tm, c), lambda t: (t, 0))
    cols_a, cols_c = [512, 512, 512, 128, 128, 512], [512, 512, 512]
    out_specs = ([row(c) for c in cols_a]
                 + [pl.BlockSpec((A_KV_HEADS, LANES, tm), lambda t: (0, 0, t)),
                    pl.BlockSpec((cpt, LANES, ATT_KC), lambda t: (t, 0, 0)), pl.BlockSpec((8, tm), lambda t: (0, t))]
                 + [row(c) for c in cols_c])
    out_shape = ([jax.ShapeDtypeStruct((n, c), BF16) for c in cols_a]
                 + [jax.ShapeDtypeStruct((A_KV_HEADS, LANES, n), BF16),
                    jax.ShapeDtypeStruct((n // ATT_KC, LANES, ATT_KC), BF16), jax.ShapeDtypeStruct((8, n), F32)]
                 + [jax.ShapeDtypeStruct((n, c), BF16) for c in cols_c])
    return pl.pallas_call(
        functools.partial(_inproj_kernel, b_scale=HEAD_DIM ** -0.5 * LOG2E,
                          c_scale=(C_NOPE + C_ROPE) ** -0.5 * LOG2E),
        grid=(nt,),
        in_specs=[row(d), full(w_p)] + [tab_spec] * 6 + [full(gq), full(gkv), full(wuq_p), full(wukv_p), full(w_t)],
        out_specs=out_specs,
        out_shape=out_shape,
        compiler_params=pltpu.CompilerParams(dimension_semantics=("parallel",), vmem_limit_bytes=VMEM_LIMIT),
    )(x2, w_p, *tables, gq, gkv, wuq_p, wukv_p, w_t)


def _swa_kernel(sink_ref, q_ref, kc_ref, kp_ref, vc_ref, vp_ref, o_ref):
    i = pl.program_id(1)
    w = A_WINDOW
    qb = q_ref.shape[0]
    kj = lax.broadcasted_iota(jnp.int32, (2 * w, w), 0)
    qi = lax.broadcasted_iota(jnp.int32, (2 * w, w), 1)
    diff = qi + w - kj
    in_window = (diff >= 0) & (diff < w)
    groups = [slice(g * SUBLANES, (g + 1) * SUBLANES) for g in range(2 * w // SUBLANES)]
    lane_head = lax.broadcasted_iota(jnp.int32, (1, 4 * w), 1) // w
    per_group = A_HEADS // A_KV_HEADS
    for win in range(qb // w):
        rows = slice(win * w, (win + 1) * w)
        ok = in_window if win > 0 else in_window & ((kj >= w) | (i > 0))
        bias1 = jnp.where(ok, 0.0, NEG_BIG)
        bias = jnp.concatenate([bias1] * per_group, axis=1)
        for g in range(A_KV_HEADS):
            def keys(unit):
                cols = slice(unit * LANES, (unit + 1) * LANES)
                if win == 0:
                    return jnp.concatenate([kp_ref[:, cols], kc_ref[0:w, cols]], axis=0)
                return kc_ref[(win - 1) * w:(win + 1) * w, cols]
            if win == 0:
                vt = jnp.concatenate([vp_ref[g], vc_ref[g, :, 0:w]], axis=1)
            else:
                vt = vc_ref[g, :, (win - 1) * w:(win + 1) * w]
            q2 = jnp.concatenate([q_ref[rows, (2 * g + u) * LANES:(2 * g + u + 1) * LANES] for u in range(2)], axis=0)
            s = jnp.concatenate([_dot_nt(keys(2 * g), q2), _dot_nt(keys(2 * g + 1), q2)], axis=1) + bias
            order = [per_group * g, per_group * g + 2, per_group * g + 1, per_group * g + 3]
            sink = jnp.zeros((1, per_group * w), F32)
            for slot, h in enumerate(order):
                sink = jnp.where(lane_head == slot, sink_ref[h] * LOG2E, sink)
            parts = [s[gg] for gg in groups]
            while len(parts) > 1:
                parts = [jnp.maximum(parts[j], parts[j + 1]) for j in range(0, len(parts), 2)]
            m = jnp.maximum(jnp.max(parts[0], axis=0, keepdims=True), sink)
            p = jnp.exp2(s - m)
            o = _dot(vt, p.astype(BF16))
            o = (o / (o[64:65, :] + jnp.exp2(sink - m))).T
            for slot, h in enumerate(order):
                o_ref[rows, h * 64:(h + 1) * 64] = o[slot * w:(slot + 1) * w, :64].astype(BF16)


def _swa(sinks, aq, ak, avt, b, s):
    n = aq.shape[0]
    w = A_WINDOW
    qb = _att_qb(s)
    nb = s // qb
    wpb = qb // w
    cur = lambda bb, i: (bb * nb + i, 0)
    prev = lambda bb, i: ((bb * nb + i) * wpb - jnp.minimum(i, 1), 0)
    cur_t = lambda bb, i: (0, 0, bb * nb + i)
    prev_t = lambda bb, i: (0, 0, (bb * nb + i) * wpb - jnp.minimum(i, 1))
    return pl.pallas_call(
        _swa_kernel,
        grid=(b, nb),
        in_specs=[pl.BlockSpec(memory_space=pltpu.SMEM),
                  pl.BlockSpec((qb, 512), cur),
                  pl.BlockSpec((qb, 512), cur), pl.BlockSpec((w, 512), prev),
                  pl.BlockSpec((A_KV_HEADS, LANES, qb), cur_t), pl.BlockSpec((A_KV_HEADS, LANES, w), prev_t)],
        out_specs=pl.BlockSpec((qb, 512), cur),
        out_shape=jax.ShapeDtypeStruct((n, 512), BF16),
        compiler_params=pltpu.CompilerParams(dimension_semantics=("parallel", "parallel"),
                                             vmem_limit_bytes=VMEM_LIMIT),
    )(sinks, aq, ak, ak, avt, avt)


ATT_KC = 256


def _att_qb(s):
    return min(512, s)


def _fold_max(s):
    return jnp.maximum(s[:, :LANES], s[:, LANES:])


def _dsa_kernel(bq_ref, iq_ref, iwt_ref, bk_ref, ik_ref, vt_ref, o_ref,
                sc_ref, s_ref, tau_ref, j_ref, ist_ref, qst_ref, m_ref, acc_ref, *, topk, idx_bits):
    i = pl.program_id(1)
    qb = bq_ref.shape[0]
    kc = ATT_KC
    per = qb // kc
    n_full = per * i
    n_chunks = n_full + per
    kidx = lax.broadcasted_iota(jnp.int32, (kc, qb), 0)
    rpos = lax.broadcasted_iota(jnp.int32, (kc, qb), 1)
    neg_inf = float("-inf")
    kstart = lambda c: pl.multiple_of(c * kc, kc)
    heads = range(B_HEADS)
    hrows = lambda h: slice(h * qb, (h + 1) * qb)
    groups = [slice(g * SUBLANES, (g + 1) * SUBLANES) for g in range(kc // SUBLANES)]

    def fold(x, op):
        parts = [x[g] for g in groups]
        while len(parts) > 1:
            parts = [op(parts[j], parts[j + 1]) for j in range(0, len(parts), 2)]
        return parts[0]

    for h in range(IDX_HEADS):
        ist_ref[hrows(h), :] = iq_ref[:, h * LANES:(h + 1) * LANES]

    def score(c):
        lg = _dot_nt(ik_ref[pl.ds(kstart(c), kc), :], ist_ref[...])
        acc = None
        for h in range(IDX_HEADS):
            t = iwt_ref[h:h + 1, :] * jnp.maximum(lg[:, hrows(h)], 0.0)
            acc = t if acc is None else acc + t
        return acc

    def idx_body(c, carry):
        sc_ref[c] = score(c)
        return carry

    lax.fori_loop(0, n_full, idx_body, 0)
    for d in range(per):
        sc_ref[n_full + d] = jnp.where(d * kc + kidx <= rpos, score(n_full + d), neg_inf)

    rw = qb // per
    kidx_r = lax.broadcasted_iota(jnp.int32, (kc, rw), 0)
    for r in range(per):
        rows = slice(r * rw, (r + 1) * rw)
        tpos = i * qb + r * rw + lax.broadcasted_iota(jnp.int32, (1, rw), 1)
        n_r = n_full + r + 1

        def count(pred_fn):
            def cbody(c, acc):
                k = sc_ref[c, :, rows]
                return acc + fold(jnp.where(pred_fn(k, c), 1.0, 0.0), jnp.add)
            acc = lax.fori_loop(0, n_r, cbody, jnp.zeros((SUBLANES, rw), F32))
            return jnp.sum(acc, axis=0, keepdims=True)

        def key_to_f32(key_u):
            key = key_u ^ INT_MIN
            return pltpu.bitcast(key ^ ((key >> 31) & 0x7FFFFFFF), F32)

        def bit_body(step, prefix):
            cand_u = prefix | lax.shift_left(jnp.int32(1), 31 - step)
            cand = key_to_f32(cand_u)
            cnt = count(lambda k, c: k >= cand)
            return jnp.where(cnt >= topk, cand_u, prefix)

        prefix = lax.fori_loop(0, 32, bit_body, jnp.zeros((1, rw), jnp.int32))
        tau = key_to_f32(prefix)
        tau = jnp.where(tau != tau, neg_inf, tau)
        c_gt = count(lambda k, c: k > tau)
        c_eq = count(lambda k, c: k == tau)
        need = topk - c_gt

        def tie_search(_):
            def jbody(step, q):
                cand = q | lax.shift_left(jnp.int32(1), idx_bits - 1 - step)
                cnt = count(lambda k, c: (k == tau) & (c * kc + kidx_r < cand))
                return jnp.where(cnt < need, cand, q)
            return lax.fori_loop(0, idx_bits, jbody, jnp.zeros((1, rw), jnp.int32))

        any_split = jnp.max(c_eq - need) > 0.0
        jcut = lax.cond(any_split, tie_search, lambda _: jnp.full((1, rw), 2 ** idx_bits, jnp.int32), 0)
        few = tpos < int(topk)
        tau_ref[:, rows] = jnp.broadcast_to(jnp.where(few, neg_inf, tau), (SUBLANES, rw))
        j_ref[:, rows] = jnp.broadcast_to(jnp.where(few, tpos, jcut), (SUBLANES, rw))

    for h in heads:
        qst_ref[hrows(h), :] = bq_ref[:, h * LANES:(h + 1) * LANES]
    m_ref[...] = jnp.full(m_ref.shape, NEG_BIG, F32)

    def max_body(c, carry):
        s = _dot_nt(bk_ref[pl.ds(kstart(c), kc), :], qst_ref[...])
        k = sc_ref[c]
        tau = tau_ref[0:1, :]
        sel = (k > tau) | ((k == tau) & (c * kc + kidx <= j_ref[0:1, :]))
        bias = jnp.where(sel, 0.0, NEG_BIG)
        for h in heads:
            sm = s[:, hrows(h)] + bias
            s_ref[c, :, hrows(h)] = sm
            m_ref[:, hrows(h)] = jnp.maximum(m_ref[:, hrows(h)], fold(sm, jnp.maximum))
        return carry

    lax.fori_loop(0, n_chunks, max_body, 0)
    m_ref[...] = jnp.broadcast_to(jnp.max(m_ref[...], axis=0, keepdims=True), m_ref.shape)

    acc_ref[...] = jnp.zeros(acc_ref.shape, F32)

    def pv_body(c, carry):
        p = jnp.exp2(s_ref[c] - m_ref[0:1, :]).astype(BF16)
        acc_ref[...] += _dot(vt_ref[c], p)
        return carry

    lax.fori_loop(0, n_chunks, pv_body, 0)
    for h in heads:
        acc = acc_ref[:, hrows(h)]
        o_ref[:, h * 64:(h + 1) * 64] = (acc / acc[64:65, :]).T[:, :64].astype(BF16)


def _dsa(bq, iq, iwt, bk, ik, vt, b, s):
    n = bq.shape[0]
    qb = _att_qb(s)
    nb = s // qb
    nkc = s // ATT_KC
    topk = min(IDX_TOPK_MAX, s // 4)
    idx_bits = max(1, int(math.ceil(math.log2(s))))
    blk = lambda bb, i: (bb * nb + i, 0)
    seq = lambda bb, i: (bb, 0)
    return pl.pallas_call(
        functools.partial(_dsa_kernel, topk=float(topk), idx_bits=idx_bits),
        grid=(b, nb),
        in_specs=[pl.BlockSpec((qb, 512), blk), pl.BlockSpec((qb, 512), blk),
                  pl.BlockSpec((8, qb), lambda bb, i: (0, bb * nb + i)),
                  pl.BlockSpec((s, 128), seq), pl.BlockSpec((s, 128), seq),
                  pl.BlockSpec((nkc, LANES, ATT_KC), lambda bb, i: (bb, 0, 0))],
        out_specs=pl.BlockSpec((qb, 256), blk),
        out_shape=jax.ShapeDtypeStruct((n, 256), BF16),
        scratch_shapes=[pltpu.VMEM((nkc, ATT_KC, qb), F32),
                        pltpu.VMEM((nkc, ATT_KC, B_HEADS * qb), F32),
                        pltpu.VMEM((SUBLANES, qb), F32), pltpu.VMEM((SUBLANES, qb), jnp.int32),
                        pltpu.VMEM((IDX_HEADS * qb, LANES), BF16), pltpu.VMEM((B_HEADS * qb, LANES), BF16),
                        pltpu.VMEM((SUBLANES, B_HEADS * qb), F32), pltpu.VMEM((LANES, B_HEADS * qb), F32)],
        compiler_params=pltpu.CompilerParams(dimension_semantics=("parallel", "arbitrary"),
                                             vmem_limit_bytes=VMEM_LIMIT),
    )(bq, iq, iwt, bk, ik, vt)


def _mla_kernel(q_ref, k_ref, v_ref, o_ref, s_ref, m_ref, acc_ref):
    i = pl.program_id(1)
    qb = q_ref.shape[0]
    kc = ATT_KC
    per = qb // kc
    n_full = per * i
    n_chunks = n_full + per
    row = lax.broadcasted_iota(jnp.int32, (qb, kc), 0)
    col = lax.broadcasted_iota(jnp.int32, (qb, kc), 1)
    kstart = lambda c: pl.multiple_of(c * kc, kc)
    heads = range(C_HEADS)
    hs = lambda h: slice(h * LANES, (h + 1) * LANES)
    n_halves = kc // LANES
    m_ref[...] = jnp.full(m_ref.shape, NEG_BIG, F32)

    def max_step(c, bias):
        for h in heads:
            s = _dot_nt(q_ref[:, hs(h)], k_ref[pl.ds(kstart(c), kc), hs(h)])
            if bias is not None:
                s = s + bias
            s_ref[h, c] = s
            m_ref[h] = jnp.maximum(m_ref[h], _fold_max(s))

    def max_body(c, carry):
        max_step(c, None)
        return carry

    lax.fori_loop(0, n_full, max_body, 0)
    for d in range(per):
        max_step(n_full + d, jnp.where(d * kc + col <= row, 0.0, NEG_BIG))
    for h in heads:
        m_ref[h] = jnp.broadcast_to(jnp.max(m_ref[h], axis=1, keepdims=True), (qb, LANES))
    acc_ref[...] = jnp.zeros(acc_ref.shape, F32)

    def pv_body(c, carry):
        for h in heads:
            p = jnp.exp2(s_ref[h, c] - jnp.concatenate([m_ref[h]] * n_halves, axis=1))
            acc_ref[h] += _dot(p.astype(BF16), v_ref[pl.ds(kstart(c), kc), hs(h)])
        return carry

    lax.fori_loop(0, n_chunks, pv_body, 0)
    for h in heads:
        acc = acc_ref[h]
        o_ref[:, h * 64:(h + 1) * 64] = (acc[:, :64] / acc[:, 64:65]).astype(BF16)


def _mla(qc, kc, vc, b, s):
    n = qc.shape[0]
    qb = _att_qb(s)
    nb = s // qb
    nkc = s // ATT_KC
    blk = lambda bb, i: (bb * nb + i, 0)
    seq = lambda bb, i: (bb, 0)
    return pl.pallas_call(
        _mla_kernel,
        grid=(b, nb),
        in_specs=[pl.BlockSpec((qb, 512), blk), pl.BlockSpec((s, 512), seq), pl.BlockSpec((s, 512), seq)],
        out_specs=pl.BlockSpec((qb, 256), blk),
        out_shape=jax.ShapeDtypeStruct((n, 256), BF16),
        scratch_shapes=[pltpu.VMEM((C_HEADS, nkc, qb, ATT_KC), F32), pltpu.VMEM((C_HEADS, qb, LANES), F32),
                        pltpu.VMEM((C_HEADS, qb, LANES), F32)],
        compiler_params=pltpu.CompilerParams(dimension_semantics=("parallel", "arbitrary"),
                                             vmem_limit_bytes=VMEM_LIMIT),
    )(qc, kc, vc)


def _layer_norm(z, g, b):
    mu = jnp.mean(z, axis=1, keepdims=True)
    zc = z - mu
    var = jnp.mean(zc * zc, axis=1, keepdims=True)
    return zc * lax.rsqrt(var + LN_EPS) * g + b


def _outproj_kernel(x_ref, oa_ref, ob_ref, oc_ref, wo_ref, g_ref, b_ref, wr_ref, rb_ref, x1_ref, gates_ref,
                    *, alpha):
    y = _dot(oa_ref[...], wo_ref[0:512, :]) + _dot(ob_ref[...], wo_ref[512:768, :]) \
        + _dot(oc_ref[...], wo_ref[768:1024, :])
    x1 = _layer_norm(alpha * x_ref[...] + y, g_ref[...], b_ref[...])
    x1_ref[...] = x1

    x1h = x1.astype(BF16)
    x1l = (x1 - x1h.astype(F32)).astype(BF16)
    wr = wr_ref[...]
    wrh = wr.astype(BF16)
    wrl = (wr - wrh.astype(F32)).astype(BF16)
    hi = _dot_nt(jnp.concatenate([wrh, wrl], axis=0), x1h)
    logits = hi[:N_EXPERTS] + hi[N_EXPERTS:] + _dot_nt(wrh, x1l)
    scores = jax.nn.sigmoid(logits)
    biased = scores + rb_ref[...]
    rows = [biased[e:e + 1, :] for e in range(N_EXPERTS)]
    gscore = []
    for g in range(N_GROUPS):
        v = rows[g * 4:(g + 1) * 4]
        best = None
        for a in range(4):
            for c in range(a + 1, 4):
                pair = v[a] + v[c]
                best = pair if best is None else jnp.maximum(best, pair)
        gscore.append(best)
    gmax = jnp.maximum(jnp.maximum(gscore[0], gscore[1]), jnp.maximum(gscore[2], gscore[3]))
    taken = jnp.zeros_like(gmax) > 1.0
    sel_rows = []
    for g in range(N_GROUPS):
        g_sel = (gscore[g] == gmax) & jnp.logical_not(taken)
        taken = taken | g_sel
        v = rows[g * 4:(g + 1) * 4]
        for e in range(4):
            rank = jnp.zeros_like(gmax)
            for j in range(4):
                if j == e:
                    continue
                ahead = (v[j] > v[e]) | ((v[j] == v[e]) & (j < e))
                rank = rank + jnp.where(ahead, 1.0, 0.0)
            sel_rows.append(g_sel & (rank < 2.0))
    wsel = [jnp.where(sel_rows[e], scores[e:e + 1, :], 0.0) for e in range(N_EXPERTS)]
    total = wsel[0]
    for e in range(1, N_EXPERTS):
        total = total + wsel[e]
    gates_ref[...] = jnp.concatenate(wsel, axis=0) / total


def _outproj(x2, oa, ob, oc, wo, g, bb, wr_t, rbias, alpha, tm):
    n, d = x2.shape
    nt = n // tm
    full = lambda a: pl.BlockSpec(a.shape, lambda t: (0,) * a.ndim)
    row = lambda c: pl.BlockSpec((tm, c), lambda t: (t, 0))
    return pl.pallas_call(
        functools.partial(_outproj_kernel, alpha=alpha),
        grid=(nt,),
        in_specs=[row(d), row(512), row(256), row(256), full(wo), full(g), full(bb), full(wr_t), full(rbias)],
        out_specs=[row(d), pl.BlockSpec((N_EXPERTS, tm), lambda t: (0, t))],
        out_shape=[jax.ShapeDtypeStruct((n, d), F32), jax.ShapeDtypeStruct((N_EXPERTS, n), F32)],
        compiler_params=pltpu.CompilerParams(dimension_semantics=("parallel",), vmem_limit_bytes=VMEM_LIMIT),
    )(x2, oa, ob, oc, wo, g, bb, wr_t, rbias)


def _moe_kernel(x_ref, gates_ref, wg_ref, wu_ref, wd_ref, g_ref, b_ref, o_ref, xb_ref, acc_ref, *, alpha):
    grp = pl.program_id(1)

    @pl.when(grp == 0)
    def _():
        xb_ref[...] = x_ref[...].astype(BF16)

    xb = xb_ref[...]
    hs = []
    for j in range(EXPERTS_PER_GROUP):
        hg = _dot(xb, wg_ref[j])
        hu = _dot(xb, wu_ref[j])
        hs.append((hg * jax.nn.sigmoid(hg) * hu * gates_ref[0, :, j:j + 1]).astype(BF16))
    wd = wd_ref[...]
    y = _dot(jnp.concatenate(hs, axis=1), wd.reshape(wd.shape[0] * wd.shape[1], wd.shape[2]))

    @pl.when(grp == 0)
    def _():
        acc_ref[...] = y

    @pl.when(grp > 0)
    def _():
        acc_ref[...] += y

    @pl.when(grp == N_GROUPS - 1)
    def _():
        o_ref[...] = _layer_norm(alpha * x_ref[...] + acc_ref[...], g_ref[...], b_ref[...])


def _moe(x1, gates_g, wg, wu, wd, g, bb, alpha, tm):
    n, d = x1.shape
    nt = n // tm
    f = wg.shape[2]
    epg = EXPERTS_PER_GROUP
    full = lambda a: pl.BlockSpec(a.shape, lambda t, e: (0,) * a.ndim)
    return pl.pallas_call(
        functools.partial(_moe_kernel, alpha=alpha),
        grid=(nt, N_GROUPS),
        in_specs=[pl.BlockSpec((tm, d), lambda t, e: (t, 0)), pl.BlockSpec((1, tm, epg), lambda t, e: (e, t, 0)),
                  pl.BlockSpec((epg, d, f), lambda t, e: (e, 0, 0)), pl.BlockSpec((epg, d, f), lambda t, e: (e, 0, 0)),
                  pl.BlockSpec((epg, f, d), lambda t, e: (e, 0, 0)), full(g), full(bb)],
        out_specs=pl.BlockSpec((tm, d), lambda t, e: (t, 0)),
        out_shape=jax.ShapeDtypeStruct((n, d), F32),
        scratch_shapes=[pltpu.VMEM((tm, d), BF16), pltpu.VMEM((tm, d), F32)],
        compiler_params=pltpu.CompilerParams(dimension_semantics=("parallel", "arbitrary"),
                                             vmem_limit_bytes=VMEM_LIMIT),
    )(x1, gates_g, wg, wu, wd, g, bb)


def kernel(x, w_in, attn_sinks, c_q_norm_g, c_kv_norm_g, w_uq, w_ukv, w_out, ln1_g, ln1_b, w_router, router_bias,
           w_gate, w_up, w_down, ln2_g, ln2_b):
    b, s, d = x.shape
    depth = w_in.shape[0]
    n = b * s
    alpha = (2 * depth) ** 0.25
    tm = min(512, s)
    tables = _rope_tables(s)
    wr_t = w_router.T
    rbias = router_bias.reshape(N_EXPERTS, 1)
    x2 = x.reshape(n, d)
    for l in range(depth):
        w_p, w_t = _pack_w_in(w_in[l])
        wuq_p, wukv_p = _pack_mla_w(w_uq[l], w_ukv[l])
        aq, ak, bq, bk, ik, iq, avt, vt, iwt, qc, kc, vc = _inproj(
            x2, w_p, w_t, tables, c_q_norm_g[l].reshape(1, -1), c_kv_norm_g[l].reshape(1, -1), wuq_p, wukv_p, s, tm)
        oa = _swa(attn_sinks[l], aq, ak, avt, b, s)
        ob = _dsa(bq, iq, iwt, bk, ik, vt, b, s)
        oc = _mla(qc, kc, vc, b, s)
        x1, gates_t = _outproj(x2, oa, ob, oc, w_out[l].astype(BF16), ln1_g[l].reshape(1, d), ln1_b[l].reshape(1, d),
                               wr_t, rbias, alpha, tm)
        gates_g = gates_t.reshape(N_GROUPS, EXPERTS_PER_GROUP, n).transpose(0, 2, 1)
        x2 = _moe(x1, gates_g, w_gate[l].astype(BF16), w_up[l].astype(BF16), w_down[l].astype(BF16),
                  ln2_g[l].reshape(1, d), ln2_b[l].reshape(1, d), alpha, min(1024, n))
    return x2.reshape(b, s, d)
```

```python
import functools
import math

import jax
import jax.numpy as jnp
import numpy as np
from jax import lax
from jax.experimental import pallas as pl
from jax.experimental.pallas import tpu as pltpu

HEAD_DIM = 64
ROPE_THETA = 10000.0
A_HEADS = 8
A_KV_HEADS = 2
A_WINDOW = 128
B_HEADS = 4
IDX_HEADS = 4
IDX_DIM = 64
IDX_TOPK_MAX = 256
C_HEADS = 4
C_NOPE = 64
C_ROPE = 32
C_V = 64
C_Q_RANK = 256
C_KV_RANK = 128
N_EXPERTS = 16
N_GROUPS = 4
EXPERTS_PER_GROUP = 4
D_EXPERT = 256
LN_EPS = 1e-5
RMS_EPS = 1e-6

LANES = 128
SUBLANES = 8
NEG_BIG = -1e30
LOG2E = math.log2(math.e)
INT_MIN = -(2 ** 31)
VMEM_LIMIT = 56 * 1024 * 1024

BF16 = jnp.bfloat16
F32 = jnp.float32

_NT = (((1,), (1,)), ((), ()))


def _dot(a, b):
    return jnp.dot(a, b, preferred_element_type=F32)


def _dot_nt(a, b):
    return lax.dot_general(a, b, _NT, preferred_element_type=F32)


N_ROPE_UNITS = 10


def _inproj_kernel(x_ref, w_ref, ch_ref, s1h_ref, s2h_ref, cr_ref, s1r_ref, s2r_ref, gq_ref, gkv_ref,
                   wuq_ref, wukv_ref, wt_ref,
                   aq_ref, ak_ref, bq_ref, bk_ref, ik_ref, iq_ref, avt_ref, vt_ref, iwt_ref, qc_ref, kc_ref, vc_ref,
                   *, b_scale, c_scale):
    xb = x_ref[...].astype(BF16)
    tm = xb.shape[0]
    lane = lax.broadcasted_iota(jnp.int32, (tm, LANES), 1)
    ch, s1h, s2h = ch_ref[...], s1h_ref[...], s2h_ref[...]
    cr, s1r, s2r = cr_ref[...], s1r_ref[...], s2r_ref[...]
    low = lane < 64

    def rope_h(u):
        return u * ch + pltpu.roll(u, 96, 1) * s1h + pltpu.roll(u, 32, 1) * s2h

    def rope_r(u):
        return u * cr + pltpu.roll(u, 112, 1) * s1r + pltpu.roll(u, 16, 1) * s2r

    def split(u):
        return jnp.where(low, u, 0.0), jnp.where(low, pltpu.roll(u, 64, 1), 0.0)

    def put(ref, k, u):
        ref[:, k * LANES:(k + 1) * LANES] = u.astype(BF16)

    for g in range(N_ROPE_UNITS // 2):
        hg = _dot(xb, w_ref[:, g * 256:(g + 1) * 256])
        for half in range(2):
            unit = 2 * g + half
            u = rope_h(hg[:, half * LANES:(half + 1) * LANES])
            if unit < 4:
                put(aq_ref, unit, u * LOG2E)
            elif unit == 4:
                swapped = pltpu.roll(u, 64, 1)
                put(ak_ref, 0, jnp.where(low, u, 0.0))
                put(ak_ref, 1, jnp.where(low, 0.0, swapped))
                put(ak_ref, 2, jnp.where(low, swapped, 0.0))
                put(ak_ref, 3, jnp.where(low, 0.0, u))
            elif unit < 7:
                h0, h1 = split(u * b_scale)
                put(bq_ref, 2 * (unit - 5), h0)
                put(bq_ref, 2 * (unit - 5) + 1, h1)
            elif unit == 7:
                h0, h1 = split(u)
                put(bk_ref, 0, h0)
                put(ik_ref, 0, h1)
            else:
                h0, h1 = split(u)
                put(iq_ref, 2 * (unit - 8), h0)
                put(iq_ref, 2 * (unit - 8) + 1, h1)

    hb = _dot_nt(wt_ref[...], xb)
    sub = lax.broadcasted_iota(jnp.int32, (LANES, tm), 0)
    with_ones = lambda piece: jnp.where(sub == 64, 1.0, hb[piece * LANES:(piece + 1) * LANES]).astype(BF16)
    vt = with_ones(0)
    for j in range(tm // ATT_KC):
        vt_ref[j] = vt[:, j * ATT_KC:(j + 1) * ATT_KC]
    iwt_ref[...] = hb[LANES:LANES + 8]
    for g in range(A_KV_HEADS):
        avt_ref[g] = with_ones(2 + g)

    cq = _dot(xb, w_ref[:, 1280:1536])
    cqn = cq * lax.rsqrt(jnp.mean(cq * cq, axis=1, keepdims=True) + RMS_EPS) * gq_ref[...]
    qc = _dot(cqn.astype(BF16), wuq_ref[...])
    for h in range(C_HEADS):
        u = qc[:, h * LANES:(h + 1) * LANES]
        qc_ref[:, h * LANES:(h + 1) * LANES] = (rope_r(u) * c_scale).astype(BF16)

    ckv = _dot(xb, w_ref[:, 1536:1792])
    ckv_lat = ckv[:, :LANES]
    kr = rope_r(ckv[:, LANES:])
    ckvn = ckv_lat * lax.rsqrt(jnp.mean(ckv_lat * ckv_lat, axis=1, keepdims=True) + RMS_EPS) * gkv_ref[...]
    kv = _dot(ckvn.astype(BF16), wukv_ref[...])
    for h in range(C_HEADS):
        kc_ref[:, h * LANES:(h + 1) * LANES] = (kv[:, h * LANES:(h + 1) * LANES] + kr).astype(BF16)
        v = kv[:, (C_HEADS + h) * LANES:(C_HEADS + h + 1) * LANES]
        vc_ref[:, h * LANES:(h + 1) * LANES] = jnp.where(lane == 64, 1.0, v).astype(BF16)


def _pack_w_in(w):
    d = w.shape[0]
    z = lambda n: jnp.zeros((d, n), F32)
    o = np.cumsum([0, 512, 128, 128, 256, 64, 64, 256, 64, 4, 256, 128, 32]).tolist()
    a_q, a_k, a_v, b_q, b_k, b_v, i_q, i_k, i_w, c_q, c_kv, c_kr = [w[:, o[j]:o[j + 1]] for j in range(12)]
    qs = HEAD_DIM ** -0.5
    ws = (IDX_HEADS * IDX_DIM) ** -0.5
    cols = [a_q * qs, a_k, b_q, b_k, i_k, i_q,
            c_q, c_kv, z(64), c_kr, z(32)]
    rows_t = jnp.concatenate([b_v, z(64), i_w * ws, z(LANES - IDX_HEADS),
                              a_v[:, :64], z(64), a_v[:, 64:], z(64)], axis=1).T
    return jnp.concatenate(cols, axis=1).astype(BF16), rows_t.astype(BF16)


def _pack_mla_w(w_uq, w_ukv):
    r = w_uq.shape[0]
    q = w_uq.reshape(r, C_HEADS, C_NOPE + C_ROPE)
    q = jnp.pad(q, ((0, 0), (0, 0), (0, LANES - C_NOPE - C_ROPE))).reshape(r, C_HEADS * LANES)
    r2 = w_ukv.shape[0]
    kv = w_ukv.reshape(r2, C_HEADS, C_NOPE + C_V)
    k = jnp.pad(kv[:, :, :C_NOPE], ((0, 0), (0, 0), (0, LANES - C_NOPE))).reshape(r2, C_HEADS * LANES)
    v = jnp.pad(kv[:, :, C_NOPE:], ((0, 0), (0, 0), (0, LANES - C_V))).reshape(r2, C_HEADS * LANES)
    return q.astype(BF16), jnp.concatenate([k, v], axis=1).astype(BF16)


def _rope_tables(s):
    pos = jnp.arange(s, dtype=F32)[:, None]
    lane = np.arange(LANES)
    inv_h = 1.0 / (ROPE_THETA ** (jnp.arange(0, HEAD_DIM, 2, dtype=F32) / HEAD_DIM))
    ang = pos * inv_h[None, :]
    cos, sin = jnp.cos(ang), jnp.sin(ang)
    j = lane % 32
    lo = jnp.asarray((lane % 64) < 32)
    ch = cos[:, j]
    s1h = jnp.where(lo, -sin[:, j], 0.0)
    s2h = jnp.where(lo, 0.0, sin[:, j])
    inv_r = 1.0 / (ROPE_THETA ** (jnp.arange(0, C_ROPE, 2, dtype=F32) / C_ROPE))
    angr = pos * inv_r[None, :]
    cosr, sinr = jnp.cos(angr), jnp.sin(angr)
    jr = lane % 16
    in_rope = jnp.asarray((lane >= 64) & (lane < 96))
    first = jnp.asarray((lane >= 64) & (lane < 80))
    second = jnp.asarray((lane >= 80) & (lane < 96))
    cr = jnp.where(in_rope, cosr[:, jr], 1.0)
    s1r = jnp.where(first, -sinr[:, jr], 0.0)
    s2r = jnp.where(second, sinr[:, jr], 0.0)
    return ch, s1h, s2h, cr, s1r, s2r


def _inproj(x2, w_p, w_t, tables, gq, gkv, wuq_p, wukv_p, s, tm):
    n, d = x2.shape
    nt = n // tm
    spt = s // tm
    cpt = tm // ATT_KC
    tab_spec = pl.BlockSpec((tm, LANES), lambda t: (t % spt, 0))
    full = lambda a: pl.BlockSpec(a.shape, lambda t: (0,) * a.ndim)
    row = lambda c: pl.BlockSpec((tm, c), lambda t: (t, 0))
    cols_a, cols_c = [512, 512, 512, 128, 128, 512], [512, 512, 512]
    out_specs = ([row(c) for c in cols_a]
                 + [pl.BlockSpec((A_KV_HEADS, LANES, tm), lambda t: (0, 0, t)),
                    pl.BlockSpec((cpt, LANES, ATT_KC), lambda t: (t, 0, 0)), pl.BlockSpec((8, tm), lambda t: (0, t))]
                 + [row(c) for c in cols_c])
    out_shape = ([jax.ShapeDtypeStruct((n, c), BF16) for c in cols_a]
                 + [jax.ShapeDtypeStruct((A_KV_HEADS, LANES, n), BF16),
                    jax.ShapeDtypeStruct((n // ATT_KC, LANES, ATT_KC), BF16), jax.ShapeDtypeStruct((8, n), F32)]
                 + [jax.ShapeDtypeStruct((n, c), BF16) for c in cols_c])
    return pl.pallas_call(
        functools.partial(_inproj_kernel, b_scale=HEAD_DIM ** -0.5 * LOG2E,
                          c_scale=(C_NOPE + C_ROPE) ** -0.5 * LOG2E),
        grid=(nt,),
        in_specs=[row(d), full(w_p)] + [tab_spec] * 6 + [full(gq), full(gkv), full(wuq_p), full(wukv_p), full(w_t)],
        out_specs=out_specs,
        out_shape=out_shape,
        compiler_params=pltpu.CompilerParams(dimension_semantics=("parallel",), vmem_limit_bytes=VMEM_LIMIT),
    )(x2, w_p, *tables, gq, gkv, wuq_p, wukv_p, w_t)


def _swa_kernel(sink_ref, q_ref, kc_ref, kp_ref, vc_ref, vp_ref, o_ref):
    i = pl.program_id(1)
    w = A_WINDOW
    qb = q_ref.shape[0]
    kj = lax.broadcasted_iota(jnp.int32, (2 * w, w), 0)
    qi = lax.broadcasted_iota(jnp.int32, (2 * w, w), 1)
    diff = qi + w - kj
    in_window = (diff >= 0) & (diff < w)
    groups = [slice(g * SUBLANES, (g + 1) * SUBLANES) for g in range(2 * w // SUBLANES)]
    lane_head = lax.broadcasted_iota(jnp.int32, (1, 4 * w), 1) // w
    per_group = A_HEADS // A_KV_HEADS
    for win in range(qb // w):
        rows = slice(win * w, (win + 1) * w)
        ok = in_window if win > 0 else in_window & ((kj >= w) | (i > 0))
        bias1 = jnp.where(ok, 0.0, NEG_BIG)
        bias = jnp.concatenate([bias1] * per_group, axis=1)
        for g in range(A_KV_HEADS):
            def keys(unit):
                cols = slice(unit * LANES, (unit + 1) * LANES)
                if win == 0:
                    return jnp.concatenate([kp_ref[:, cols], kc_ref[0:w, cols]], axis=0)
                return kc_ref[(win - 1) * w:(win + 1) * w, cols]
            if win == 0:
                vt = jnp.concatenate([vp_ref[g], vc_ref[g, :, 0:w]], axis=1)
            else:
                vt = vc_ref[g, :, (win - 1) * w:(win + 1) * w]
            q2 = jnp.concatenate([q_ref[rows, (2 * g + u) * LANES:(2 * g + u + 1) * LANES] for u in range(2)], axis=0)
            s = jnp.concatenate([_dot_nt(keys(2 * g), q2), _dot_nt(keys(2 * g + 1), q2)], axis=1) + bias
            order = [per_group * g, per_group * g + 2, per_group * g + 1, per_group * g + 3]
            sink = jnp.zeros((1, per_group * w), F32)
            for slot, h in enumerate(order):
                sink = jnp.where(lane_head == slot, sink_ref[h] * LOG2E, sink)
            parts = [s[gg] for gg in groups]
            while len(parts) > 1:
                parts = [jnp.maximum(parts[j], parts[j + 1]) for j in range(0, len(parts), 2)]
            m = jnp.maximum(jnp.max(parts[0], axis=0, keepdims=True), sink)
            p = jnp.exp2(s - m)
            o = _dot(vt, p.astype(BF16))
            o = (o / (o[64:65, :] + jnp.exp2(sink - m))).T
            for slot, h in enumerate(order):
                o_ref[rows, h * 64:(h + 1) * 64] = o[slot * w:(slot + 1) * w, :64].astype(BF16)


def _swa(sinks, aq, ak, avt, b, s):
    n = aq.shape[0]
    w = A_WINDOW
    qb = _att_qb(s)
    nb = s // qb
    wpb = qb // w
    cur = lambda bb, i: (bb * nb + i, 0)
    prev = lambda bb, i: ((bb * nb + i) * wpb - jnp.minimum(i, 1), 0)
    cur_t = lambda bb, i: (0, 0, bb * nb + i)
    prev_t = lambda bb, i: (0, 0, (bb * nb + i) * wpb - jnp.minimum(i, 1))
    return pl.pallas_call(
        _swa_kernel,
        grid=(b, nb),
        in_specs=[pl.BlockSpec(memory_space=pltpu.SMEM),
                  pl.BlockSpec((qb, 512), cur),
                  pl.BlockSpec((qb, 512), cur), pl.BlockSpec((w, 512), prev),
                  pl.BlockSpec((A_KV_HEADS, LANES, qb), cur_t), pl.BlockSpec((A_KV_HEADS, LANES, w), prev_t)],
        out_specs=pl.BlockSpec((qb, 512), cur),
        out_shape=jax.ShapeDtypeStruct((n, 512), BF16),
        compiler_params=pltpu.CompilerParams(dimension_semantics=("parallel", "parallel"),
                                             vmem_limit_bytes=VMEM_LIMIT),
    )(sinks, aq, ak, ak, avt, avt)


ATT_KC = 256


def _att_qb(s):
    return min(512, s)


def _fold_max(s):
    return jnp.maximum(s[:, :LANES], s[:, LANES:])


def _unrolled(n, body, carry):
    for c in range(n):
        carry = body(c, carry)
    return carry


def _dsa_kernel(bq_ref, iq_ref, iwt_ref, bk_ref, ik_ref, vt_ref, o_ref,
                sc_ref, s_ref, tau_ref, j_ref, ist_ref, qst_ref, m_ref, acc_ref, *, blk, topk, idx_bits):
    i = blk
    qb = bq_ref.shape[0]
    kc = ATT_KC
    per = qb // kc
    n_full = per * i
    n_chunks = n_full + per
    kidx = lax.broadcasted_iota(jnp.int32, (kc, qb), 0)
    rpos = lax.broadcasted_iota(jnp.int32, (kc, qb), 1)
    neg_inf = float("-inf")
    kstart = lambda c: c * kc
    heads = range(B_HEADS)
    hrows = lambda h: slice(h * qb, (h + 1) * qb)
    groups = [slice(g * SUBLANES, (g + 1) * SUBLANES) for g in range(kc // SUBLANES)]

    def fold(x, op, ways=4):
        parts = [x[g] for g in groups[:ways]]
        for j, g in enumerate(groups[ways:]):
            parts[j % ways] = op(parts[j % ways], x[g])
        while len(parts) > 1:
            parts = [op(parts[j], parts[j + 1]) for j in range(0, len(parts), 2)]
        return parts[0]

    for h in range(IDX_HEADS):
        ist_ref[hrows(h), :] = iq_ref[:, h * LANES:(h + 1) * LANES]

    def score(c):
        lg = _dot_nt(ik_ref[pl.ds(kstart(c), kc), :], ist_ref[...])
        acc = None
        for h in range(IDX_HEADS):
            t = iwt_ref[h:h + 1, :] * jnp.maximum(lg[:, hrows(h)], 0.0)
            acc = t if acc is None else acc + t
        return acc

    rw = qb // per

    def put_scores(c, sc):
        for r in range(per):
            sc_ref[c, r] = sc[:, r * rw:(r + 1) * rw]

    def idx_body(c, carry):
        put_scores(c, score(c))
        return carry

    _unrolled(n_full, idx_body, 0)
    for d in range(per):
        put_scores(n_full + d, jnp.where(d * kc + kidx <= rpos, score(n_full + d), neg_inf))

    kidx_r = lax.broadcasted_iota(jnp.int32, (kc, rw), 0)
    for r in range(per):
        rows = slice(r * rw, (r + 1) * rw)
        tpos = i * qb + r * rw + lax.broadcasted_iota(jnp.int32, (1, rw), 1)
        n_r = n_full + r + 1

        def count(pred_fn):
            def cbody(c, acc):
                k = sc_ref[c, r]
                return acc + fold(jnp.where(pred_fn(k, c), 1.0, 0.0), jnp.add, ways=1)
            acc = _unrolled(n_r, cbody, jnp.zeros((SUBLANES, rw), F32))
            return jnp.sum(acc, axis=0, keepdims=True)

        def key_to_f32(key_u):
            key = key_u ^ INT_MIN
            return pltpu.bitcast(key ^ ((key >> 31) & 0x7FFFFFFF), F32)

        def bit_body(step, prefix):
            cand_u = prefix | lax.shift_left(jnp.int32(1), 31 - step)
            cand = key_to_f32(cand_u)
            cnt = count(lambda k, c: k >= cand)
            return jnp.where(cnt >= topk, cand_u, prefix)

        prefix = lax.fori_loop(0, 32, bit_body, jnp.zeros((1, rw), jnp.int32))
        tau = key_to_f32(prefix)
        tau = jnp.where(tau != tau, neg_inf, tau)
        c_gt = count(lambda k, c: k > tau)
        c_eq = count(lambda k, c: k == tau)
        need = topk - c_gt

        def tie_search(_):
            def jbody(step, q):
                cand = q | lax.shift_left(jnp.int32(1), idx_bits - 1 - step)
                cnt = count(lambda k, c: (k == tau) & (c * kc + kidx_r < cand))
                return jnp.where(cnt < need, cand, q)
            return lax.fori_loop(0, idx_bits, jbody, jnp.zeros((1, rw), jnp.int32))

        any_split = jnp.max(c_eq - need) > 0.0
        jcut = lax.cond(any_split, tie_search, lambda _: jnp.full((1, rw), 2 ** idx_bits, jnp.int32), 0)
        few = tpos < int(topk)
        tau_ref[:, rows] = jnp.broadcast_to(jnp.where(few, neg_inf, tau), (SUBLANES, rw))
        j_ref[:, rows] = jnp.broadcast_to(jnp.where(few, tpos, jcut), (SUBLANES, rw))

    for h in heads:
        qst_ref[hrows(h), :] = bq_ref[:, h * LANES:(h + 1) * LANES]
    m_ref[...] = jnp.full(m_ref.shape, NEG_BIG, F32)

    def max_body(c, carry):
        s = _dot_nt(bk_ref[pl.ds(kstart(c), kc), :], qst_ref[...])
        k = jnp.concatenate([sc_ref[c, r] for r in range(per)], axis=1)
        tau = tau_ref[0:1, :]
        sel = (k > tau) | ((k == tau) & (c * kc + kidx <= j_ref[0:1, :]))
        bias = jnp.where(sel, 0.0, NEG_BIG)
        for h in heads:
            sm = s[:, hrows(h)] + bias
            s_ref[c, :, hrows(h)] = sm
            m_ref[:, hrows(h)] = jnp.maximum(m_ref[:, hrows(h)], fold(sm, jnp.maximum, ways=1))
        return carry

    _unrolled(n_chunks, max_body, 0)
    m_ref[...] = jnp.broadcast_to(jnp.max(m_ref[...], axis=0, keepdims=True), m_ref.shape)

    acc_ref[...] = jnp.zeros(acc_ref.shape, F32)

    def pv_body(c, carry):
        p = jnp.exp2(s_ref[c] - m_ref[0:1, :]).astype(BF16)
        acc_ref[...] += _dot(vt_ref[c], p)
        return carry

    _unrolled(n_chunks, pv_body, 0)
    for h in heads:
        acc = acc_ref[:, hrows(h)]
        o_ref[:, h * 64:(h + 1) * 64] = (acc / acc[64:65, :]).T[:, :64].astype(BF16)


def _dsa(bq, iq, iwt, bk, ik, vt, b, s):
    n = bq.shape[0]
    qb = _att_qb(s)
    nb = s // qb
    nkc = s // ATT_KC
    topk = min(IDX_TOPK_MAX, s // 4)
    idx_bits = max(1, int(math.ceil(math.log2(s))))
    seq = lambda bb: (bb, 0)
    outs = []
    for i in range(nb):
        blk = lambda bb, i=i: (bb * nb + i, 0)
        n_kc = (i + 1) * (qb // ATT_KC)
        outs.append(pl.pallas_call(
            functools.partial(_dsa_kernel, blk=i, topk=float(topk), idx_bits=idx_bits),
            grid=(b,),
            in_specs=[pl.BlockSpec((qb, 512), blk), pl.BlockSpec((qb, 512), blk),
                      pl.BlockSpec((8, qb), lambda bb, i=i: (0, bb * nb + i)),
                      pl.BlockSpec((s, 128), seq), pl.BlockSpec((s, 128), seq),
                      pl.BlockSpec((nkc, LANES, ATT_KC), lambda bb: (bb, 0, 0))],
            out_specs=pl.BlockSpec((qb, 256), lambda bb: (bb, 0)),
            out_shape=jax.ShapeDtypeStruct((b * qb, 256), BF16),
            scratch_shapes=[pltpu.VMEM((n_kc, qb // ATT_KC, ATT_KC, ATT_KC), F32),
                            pltpu.VMEM((n_kc, ATT_KC, B_HEADS * qb), F32),
                            pltpu.VMEM((SUBLANES, qb), F32), pltpu.VMEM((SUBLANES, qb), jnp.int32),
                            pltpu.VMEM((IDX_HEADS * qb, LANES), BF16), pltpu.VMEM((B_HEADS * qb, LANES), BF16),
                            pltpu.VMEM((SUBLANES, B_HEADS * qb), F32), pltpu.VMEM((LANES, B_HEADS * qb), F32)],
            compiler_params=pltpu.CompilerParams(dimension_semantics=("parallel",), vmem_limit_bytes=VMEM_LIMIT),
        )(bq, iq, iwt, bk, ik, vt))
    return jnp.stack([o.reshape(b, qb, 256) for o in outs], axis=1).reshape(n, 256)


def _mla_kernel(q_ref, k_ref, v_ref, o_ref, s_ref, m_ref, acc_ref, *, blk):
    qb = q_ref.shape[0]
    kc = ATT_KC
    per = qb // kc
    n_full = per * blk
    n_chunks = n_full + per
    row = lax.broadcasted_iota(jnp.int32, (qb, kc), 0)
    col = lax.broadcasted_iota(jnp.int32, (qb, kc), 1)
    heads = range(C_HEADS)
    hs = lambda h: slice(h * LANES, (h + 1) * LANES)
    keys = lambda c: slice(c * kc, (c + 1) * kc)
    n_halves = kc // LANES
    m_ref[...] = jnp.full(m_ref.shape, NEG_BIG, F32)

    for c in range(n_chunks):
        diag = c - n_full
        for h in heads:
            s = _dot_nt(q_ref[:, hs(h)], k_ref[keys(c), hs(h)])
            if diag >= 0:
                s = s + jnp.where(diag * kc + col <= row, 0.0, NEG_BIG)
            s_ref[h, c] = s
            m_ref[h] = jnp.maximum(m_ref[h], _fold_max(s))
    for h in heads:
        m_ref[h] = jnp.broadcast_to(jnp.max(m_ref[h], axis=1, keepdims=True), (qb, LANES))
    acc_ref[...] = jnp.zeros(acc_ref.shape, F32)

    for c in range(n_chunks):
        for h in heads:
            p = jnp.exp2(s_ref[h, c] - jnp.concatenate([m_ref[h]] * n_halves, axis=1))
            acc_ref[h] += _dot(p.astype(BF16), v_ref[keys(c), hs(h)])
    for h in heads:
        acc = acc_ref[h]
        o_ref[:, h * 64:(h + 1) * 64] = (acc[:, :64] / acc[:, 64:65]).astype(BF16)


def _mla(qc, kc, vc, b, s):
    n = qc.shape[0]
    qb = _att_qb(s)
    nb = s // qb
    seq = lambda bb: (bb, 0)
    outs = []
    for i in range(nb):
        n_kc = (i + 1) * (qb // ATT_KC)
        outs.append(pl.pallas_call(
            functools.partial(_mla_kernel, blk=i),
            grid=(b,),
            in_specs=[pl.BlockSpec((qb, 512), lambda bb, i=i: (bb * nb + i, 0)),
                      pl.BlockSpec((s, 512), seq), pl.BlockSpec((s, 512), seq)],
            out_specs=pl.BlockSpec((qb, 256), lambda bb: (bb, 0)),
            out_shape=jax.ShapeDtypeStruct((b * qb, 256), BF16),
            scratch_shapes=[pltpu.VMEM((C_HEADS, n_kc, qb, ATT_KC), F32), pltpu.VMEM((C_HEADS, qb, LANES), F32),
                            pltpu.VMEM((C_HEADS, qb, LANES), F32)],
            compiler_params=pltpu.CompilerParams(dimension_semantics=("parallel",), vmem_limit_bytes=VMEM_LIMIT),
        )(qc, kc, vc))
    return jnp.stack([o.reshape(b, qb, 256) for o in outs], axis=1).reshape(n, 256)


def _layer_norm(z, g, b):
    mu = jnp.mean(z, axis=1, keepdims=True)
    zc = z - mu
    var = jnp.mean(zc * zc, axis=1, keepdims=True)
    return zc * lax.rsqrt(var + LN_EPS) * g + b


def _outproj_kernel(x_ref, oa_ref, ob_ref, oc_ref, wo_ref, g_ref, b_ref, wr_ref, rb_ref, x1_ref, gates_ref,
                    *, alpha):
    y = _dot(oa_ref[...], wo_ref[0:512, :]) + _dot(ob_ref[...], wo_ref[512:768, :]) \
        + _dot(oc_ref[...], wo_ref[768:1024, :])
    x1 = _layer_norm(alpha * x_ref[...] + y, g_ref[...], b_ref[...])
    x1_ref[...] = x1

    x1h = x1.astype(BF16)
    x1l = (x1 - x1h.astype(F32)).astype(BF16)
    wr = wr_ref[...]
    wrh = wr.astype(BF16)
    wrl = (wr - wrh.astype(F32)).astype(BF16)
    hi = _dot_nt(jnp.concatenate([wrh, wrl], axis=0), x1h)
    logits = hi[:N_EXPERTS] + hi[N_EXPERTS:] + _dot_nt(wrh, x1l)
    scores = jax.nn.sigmoid(logits)
    biased = scores + rb_ref[...]
    rows = [biased[e:e + 1, :] for e in range(N_EXPERTS)]
    gscore = []
    for g in range(N_GROUPS):
        v = rows[g * 4:(g + 1) * 4]
        best = None
        for a in range(4):
            for c in range(a + 1, 4):
                pair = v[a] + v[c]
                best = pair if best is None else jnp.maximum(best, pair)
        gscore.append(best)
    gmax = jnp.maximum(jnp.maximum(gscore[0], gscore[1]), jnp.maximum(gscore[2], gscore[3]))
    taken = jnp.zeros_like(gmax) > 1.0
    sel_rows = []
    for g in range(N_GROUPS):
        g_sel = (gscore[g] == gmax) & jnp.logical_not(taken)
        taken = taken | g_sel
        v = rows[g * 4:(g + 1) * 4]
        for e in range(4):
            rank = jnp.zeros_like(gmax)
            for j in range(4):
                if j == e:
                    continue
                ahead = (v[j] > v[e]) | ((v[j] == v[e]) & (j < e))
                rank = rank + jnp.where(ahead, 1.0, 0.0)
            sel_rows.append(g_sel & (rank < 2.0))
    wsel = [jnp.where(sel_rows[e], scores[e:e + 1, :], 0.0) for e in range(N_EXPERTS)]
    total = wsel[0]
    for e in range(1, N_EXPERTS):
        total = total + wsel[e]
    gates_ref[...] = jnp.concatenate(wsel, axis=0) / total


def _outproj(x2, oa, ob, oc, wo, g, bb, wr_t, rbias, alpha, tm):
    n, d = x2.shape
    nt = n // tm
    full = lambda a: pl.BlockSpec(a.shape, lambda t: (0,) * a.ndim)
    row = lambda c: pl.BlockSpec((tm, c), lambda t: (t, 0))
    return pl.pallas_call(
        functools.partial(_outproj_kernel, alpha=alpha),
        grid=(nt,),
        in_specs=[row(d), row(512), row(256), row(256), full(wo), full(g), full(bb), full(wr_t), full(rbias)],
        out_specs=[row(d), pl.BlockSpec((N_EXPERTS, tm), lambda t: (0, t))],
        out_shape=[jax.ShapeDtypeStruct((n, d), F32), jax.ShapeDtypeStruct((N_EXPERTS, n), F32)],
        compiler_params=pltpu.CompilerParams(dimension_semantics=("parallel",), vmem_limit_bytes=VMEM_LIMIT),
    )(x2, oa, ob, oc, wo, g, bb, wr_t, rbias)


def _moe_kernel(x_ref, gates_ref, wg_ref, wu_ref, wd_ref, g_ref, b_ref, o_ref, xb_ref, acc_ref, *, alpha):
    grp = pl.program_id(1)

    @pl.when(grp == 0)
    def _():
        xb_ref[...] = x_ref[...].astype(BF16)

    xb = xb_ref[...]
    hs = []
    for j in range(EXPERTS_PER_GROUP):
        hg = _dot(xb, wg_ref[j])
        hu = _dot(xb, wu_ref[j])
        hs.append((hg * jax.nn.sigmoid(hg) * hu * gates_ref[0, :, j:j + 1]).astype(BF16))
    wd = wd_ref[...]
    y = _dot(jnp.concatenate(hs, axis=1), wd.reshape(wd.shape[0] * wd.shape[1], wd.shape[2]))

    @pl.when(grp == 0)
    def _():
        acc_ref[...] = y

    @pl.when(grp > 0)
    def _():
        acc_ref[...] += y

    @pl.when(grp == N_GROUPS - 1)
    def _():
        o_ref[...] = _layer_norm(alpha * x_ref[...] + acc_ref[...], g_ref[...], b_ref[...])


def _moe(x1, gates_g, wg, wu, wd, g, bb, alpha, tm):
    n, d = x1.shape
    nt = n // tm
    f = wg.shape[2]
    epg = EXPERTS_PER_GROUP
    full = lambda a: pl.BlockSpec(a.shape, lambda t, e: (0,) * a.ndim)
    return pl.pallas_call(
        functools.partial(_moe_kernel, alpha=alpha),
        grid=(nt, N_GROUPS),
        in_specs=[pl.BlockSpec((tm, d), lambda t, e: (t, 0)), pl.BlockSpec((1, tm, epg), lambda t, e: (e, t, 0)),
                  pl.BlockSpec((epg, d, f), lambda t, e: (e, 0, 0)), pl.BlockSpec((epg, d, f), lambda t, e: (e, 0, 0)),
                  pl.BlockSpec((epg, f, d), lambda t, e: (e, 0, 0)), full(g), full(bb)],
        out_specs=pl.BlockSpec((tm, d), lambda t, e: (t, 0)),
        out_shape=jax.ShapeDtypeStruct((n, d), F32),
        scratch_shapes=[pltpu.VMEM((tm, d), BF16), pltpu.VMEM((tm, d), F32)],
        compiler_params=pltpu.CompilerParams(dimension_semantics=("parallel", "arbitrary"),
                                             vmem_limit_bytes=VMEM_LIMIT),
    )(x1, gates_g, wg, wu, wd, g, bb)


def kernel(x, w_in, attn_sinks, c_q_norm_g, c_kv_norm_g, w_uq, w_ukv, w_out, ln1_g, ln1_b, w_router, router_bias,
           w_gate, w_up, w_down, ln2_g, ln2_b):
    b, s, d = x.shape
    depth = w_in.shape[0]
    n = b * s
    alpha = (2 * depth) ** 0.25
    tm = min(512, s)
    tables = _rope_tables(s)
    wr_t = w_router.T
    rbias = router_bias.reshape(N_EXPERTS, 1)
    x2 = x.reshape(n, d)
    for l in range(depth):
        w_p, w_t = _pack_w_in(w_in[l])
        wuq_p, wukv_p = _pack_mla_w(w_uq[l], w_ukv[l])
        aq, ak, bq, bk, ik, iq, avt, vt, iwt, qc, kc, vc = _inproj(
            x2, w_p, w_t, tables, c_q_norm_g[l].reshape(1, -1), c_kv_norm_g[l].reshape(1, -1), wuq_p, wukv_p, s, tm)
        oa = _swa(attn_sinks[l], aq, ak, avt, b, s)
        ob = _dsa(bq, iq, iwt, bk, ik, vt, b, s)
        oc = _mla(qc, kc, vc, b, s)
        x1, gates_t = _outproj(x2, oa, ob, oc, w_out[l].astype(BF16), ln1_g[l].reshape(1, d), ln1_b[l].reshape(1, d),
                               wr_t, rbias, alpha, tm)
        gates_g = gates_t.reshape(N_GROUPS, EXPERTS_PER_GROUP, n).transpose(0, 2, 1)
        x2 = _moe(x1, gates_g, w_gate[l].astype(BF16), w_up[l].astype(BF16), w_down[l].astype(BF16),
                  ln2_g[l].reshape(1, d), ln2_b[l].reshape(1, d), alpha, min(1024, n))
    return x2.reshape(b, s, d)
```

```python
import functools
import math

import jax
import jax.numpy as jnp
import numpy as np
from jax import lax
from jax.experimental import pallas as pl
from jax.experimental.pallas import tpu as pltpu

HEAD_DIM = 64
ROPE_THETA = 10000.0
A_HEADS = 8
A_KV_HEADS = 2
A_WINDOW = 128
B_HEADS = 4
IDX_HEADS = 4
IDX_DIM = 64
IDX_TOPK_MAX = 256
C_HEADS = 4
C_NOPE = 64
C_ROPE = 32
C_V = 64
C_Q_RANK = 256
C_KV_RANK = 128
N_EXPERTS = 16
N_GROUPS = 4
EXPERTS_PER_GROUP = 4
D_EXPERT = 256
LN_EPS = 1e-5
RMS_EPS = 1e-6

LANES = 128
SUBLANES = 8
NEG_BIG = -1e30
LOG2E = math.log2(math.e)
INT_MIN = -(2 ** 31)
VMEM_LIMIT = 56 * 1024 * 1024

BF16 = jnp.bfloat16
F32 = jnp.float32

_NT = (((1,), (1,)), ((), ()))


def _dot(a, b):
    return jnp.dot(a, b, preferred_element_type=F32)


def _dot_nt(a, b):
    return lax.dot_general(a, b, _NT, preferred_element_type=F32)


N_ROPE_UNITS = 10


def _inproj_kernel(x_ref, w_ref, ch_ref, s1h_ref, s2h_ref, cr_ref, s1r_ref, s2r_ref, gq_ref, gkv_ref,
                   wuq_ref, wukv_ref, wt_ref,
                   aq_ref, ak_ref, bq_ref, bk_ref, ik_ref, iq_ref, avt_ref, vt_ref, iwt_ref, qc_ref, kc_ref, vc_ref,
                   *, b_scale, c_scale):
    xb = x_ref[...].astype(BF16)
    tm = xb.shape[0]
    lane = lax.broadcasted_iota(jnp.int32, (tm, LANES), 1)
    ch, s1h, s2h = ch_ref[...], s1h_ref[...], s2h_ref[...]
    cr, s1r, s2r = cr_ref[...], s1r_ref[...], s2r_ref[...]
    low = lane < 64

    def rope_h(u):
        return u * ch + pltpu.roll(u, 96, 1) * s1h + pltpu.roll(u, 32, 1) * s2h

    def rope_r(u):
        return u * cr + pltpu.roll(u, 112, 1) * s1r + pltpu.roll(u, 16, 1) * s2r

    def split(u):
        return jnp.where(low, u, 0.0), jnp.where(low, pltpu.roll(u, 64, 1), 0.0)

    def put(ref, k, u):
        ref[:, k * LANES:(k + 1) * LANES] = u.astype(BF16)

    for g in range(N_ROPE_UNITS // 2):
        hg = _dot(xb, w_ref[:, g * 256:(g + 1) * 256])
        for half in range(2):
            unit = 2 * g + half
            u = rope_h(hg[:, half * LANES:(half + 1) * LANES])
            if unit < 4:
                put(aq_ref, unit, u * LOG2E)
            elif unit == 4:
                swapped = pltpu.roll(u, 64, 1)
                put(ak_ref, 0, jnp.where(low, u, 0.0))
                put(ak_ref, 1, jnp.where(low, 0.0, swapped))
                put(ak_ref, 2, jnp.where(low, swapped, 0.0))
                put(ak_ref, 3, jnp.where(low, 0.0, u))
            elif unit < 7:
                h0, h1 = split(u * b_scale)
                put(bq_ref, 2 * (unit - 5), h0)
                put(bq_ref, 2 * (unit - 5) + 1, h1)
            elif unit == 7:
                h0, h1 = split(u)
                put(bk_ref, 0, h0)
                put(ik_ref, 0, h1)
            else:
                h0, h1 = split(u)
                put(iq_ref, 2 * (unit - 8), h0)
                put(iq_ref, 2 * (unit - 8) + 1, h1)

    hb = _dot_nt(wt_ref[...], xb)
    sub = lax.broadcasted_iota(jnp.int32, (LANES, tm), 0)
    with_ones = lambda piece: jnp.where(sub == 64, 1.0, hb[piece * LANES:(piece + 1) * LANES]).astype(BF16)
    vt = with_ones(0)
    for j in range(tm // ATT_KC):
        vt_ref[j] = vt[:, j * ATT_KC:(j + 1) * ATT_KC]
    iwt_ref[...] = hb[LANES:LANES + 8]
    for g in range(A_KV_HEADS):
        avt_ref[g] = with_ones(2 + g)

    cq = _dot(xb, w_ref[:, 1280:1536])
    cqn = cq * lax.rsqrt(jnp.mean(cq * cq, axis=1, keepdims=True) + RMS_EPS) * gq_ref[...]
    qc = _dot(cqn.astype(BF16), wuq_ref[...])
    for h in range(C_HEADS):
        u = qc[:, h * LANES:(h + 1) * LANES]
        qc_ref[:, h * LANES:(h + 1) * LANES] = (rope_r(u) * c_scale).astype(BF16)

    ckv = _dot(xb, w_ref[:, 1536:1792])
    ckv_lat = ckv[:, :LANES]
    kr = rope_r(ckv[:, LANES:])
    ckvn = ckv_lat * lax.rsqrt(jnp.mean(ckv_lat * ckv_lat, axis=1, keepdims=True) + RMS_EPS) * gkv_ref[...]
    kv = _dot(ckvn.astype(BF16), wukv_ref[...])
    for h in range(C_HEADS):
        kc_ref[:, h * LANES:(h + 1) * LANES] = (kv[:, h * LANES:(h + 1) * LANES] + kr).astype(BF16)
        v = kv[:, (C_HEADS + h) * LANES:(C_HEADS + h + 1) * LANES]
        vc_ref[:, h * LANES:(h + 1) * LANES] = jnp.where(lane == 64, 1.0, v).astype(BF16)


def _pack_w_in(w):
    d = w.shape[0]
    z = lambda n: jnp.zeros((d, n), F32)
    o = np.cumsum([0, 512, 128, 128, 256, 64, 64, 256, 64, 4, 256, 128, 32]).tolist()
    a_q, a_k, a_v, b_q, b_k, b_v, i_q, i_k, i_w, c_q, c_kv, c_kr = [w[:, o[j]:o[j + 1]] for j in range(12)]
    qs = HEAD_DIM ** -0.5
    ws = (IDX_HEADS * IDX_DIM) ** -0.5
    cols = [a_q * qs, a_k, b_q, b_k, i_k, i_q,
            c_q, c_kv, z(64), c_kr, z(32)]
    rows_t = jnp.concatenate([b_v, z(64), i_w * ws, z(LANES - IDX_HEADS),
                              a_v[:, :64], z(64), a_v[:, 64:], z(64)], axis=1).T
    return jnp.concatenate(cols, axis=1).astype(BF16), rows_t.astype(BF16)


def _pack_mla_w(w_uq, w_ukv):
    r = w_uq.shape[0]
    q = w_uq.reshape(r, C_HEADS, C_NOPE + C_ROPE)
    q = jnp.pad(q, ((0, 0), (0, 0), (0, LANES - C_NOPE - C_ROPE))).reshape(r, C_HEADS * LANES)
    r2 = w_ukv.shape[0]
    kv = w_ukv.reshape(r2, C_HEADS, C_NOPE + C_V)
    k = jnp.pad(kv[:, :, :C_NOPE], ((0, 0), (0, 0), (0, LANES - C_NOPE))).reshape(r2, C_HEADS * LANES)
    v = jnp.pad(kv[:, :, C_NOPE:], ((0, 0), (0, 0), (0, LANES - C_V))).reshape(r2, C_HEADS * LANES)
    return q.astype(BF16), jnp.concatenate([k, v], axis=1).astype(BF16)


def _rope_tables(s):
    pos = jnp.arange(s, dtype=F32)[:, None]
    lane = np.arange(LANES)
    inv_h = 1.0 / (ROPE_THETA ** (jnp.arange(0, HEAD_DIM, 2, dtype=F32) / HEAD_DIM))
    ang = pos * inv_h[None, :]
    cos, sin = jnp.cos(ang), jnp.sin(ang)
    j = lane % 32
    lo = jnp.asarray((lane % 64) < 32)
    ch = cos[:, j]
    s1h = jnp.where(lo, -sin[:, j], 0.0)
    s2h = jnp.where(lo, 0.0, sin[:, j])
    inv_r = 1.0 / (ROPE_THETA ** (jnp.arange(0, C_ROPE, 2, dtype=F32) / C_ROPE))
    angr = pos * inv_r[None, :]
    cosr, sinr = jnp.cos(angr), jnp.sin(angr)
    jr = lane % 16
    in_rope = jnp.asarray((lane >= 64) & (lane < 96))
    first = jnp.asarray((lane >= 64) & (lane < 80))
    second = jnp.asarray((lane >= 80) & (lane < 96))
    cr = jnp.where(in_rope, cosr[:, jr], 1.0)
    s1r = jnp.where(first, -sinr[:, jr], 0.0)
    s2r = jnp.where(second, sinr[:, jr], 0.0)
    return ch, s1h, s2h, cr, s1r, s2r


def _inproj(x2, w_p, w_t, tables, gq, gkv, wuq_p, wukv_p, s, tm):
    n, d = x2.shape
    nt = n // tm
    spt = s // tm
    cpt = tm // ATT_KC
    tab_spec = pl.BlockSpec((tm, LANES), lambda t: (t % spt, 0))
    full = lambda a: pl.BlockSpec(a.shape, lambda t: (0,) * a.ndim)
    row = lambda c: pl.BlockSpec((tm, c), lambda t: (t, 0))
    cols_a, cols_c = [512, 512, 512, 128, 128, 512], [512, 512, 512]
    out_specs = ([row(c) for c in cols_a]
                 + [pl.BlockSpec((A_KV_HEADS, LANES, tm), lambda t: (0, 0, t)),
                    pl.BlockSpec((cpt, LANES, ATT_KC), lambda t: (t, 0, 0)), pl.BlockSpec((8, tm), lambda t: (0, t))]
                 + [row(c) for c in cols_c])
    out_shape = ([jax.ShapeDtypeStruct((n, c), BF16) for c in cols_a]
                 + [jax.ShapeDtypeStruct((A_KV_HEADS, LANES, n), BF16),
                    jax.ShapeDtypeStruct((n // ATT_KC, LANES, ATT_KC), BF16), jax.ShapeDtypeStruct((8, n), F32)]
                 + [jax.ShapeDtypeStruct((n, c), BF16) for c in cols_c])
    return pl.pallas_call(
        functools.partial(_inproj_kernel, b_scale=HEAD_DIM ** -0.5 * LOG2E,
                          c_scale=(C_NOPE + C_ROPE) ** -0.5 * LOG2E),
        grid=(nt,),
        in_specs=[row(d), full(w_p)] + [tab_spec] * 6 + [full(gq), full(gkv), full(wuq_p), full(wukv_p), full(w_t)],
        out_specs=out_specs,
        out_shape=out_shape,
        compiler_params=pltpu.CompilerParams(dimension_semantics=("parallel",), vmem_limit_bytes=VMEM_LIMIT),
    )(x2, w_p, *tables, gq, gkv, wuq_p, wukv_p, w_t)


def _swa_kernel(sink_ref, q_ref, kc_ref, kp_ref, vc_ref, vp_ref, o_ref):
    i = pl.program_id(1)
    w = A_WINDOW
    qb = q_ref.shape[0]
    kj = lax.broadcasted_iota(jnp.int32, (2 * w, w), 0)
    qi = lax.broadcasted_iota(jnp.int32, (2 * w, w), 1)
    diff = qi + w - kj
    in_window = (diff >= 0) & (diff < w)
    groups = [slice(g * SUBLANES, (g + 1) * SUBLANES) for g in range(2 * w // SUBLANES)]
    lane_head = lax.broadcasted_iota(jnp.int32, (1, 4 * w), 1) // w
    per_group = A_HEADS // A_KV_HEADS
    for win in range(qb // w):
        rows = slice(win * w, (win + 1) * w)
        ok = in_window if win > 0 else in_window & ((kj >= w) | (i > 0))
        bias1 = jnp.where(ok, 0.0, NEG_BIG)
        bias = jnp.concatenate([bias1] * per_group, axis=1)
        for g in range(A_KV_HEADS):
            def keys(unit):
                cols = slice(unit * LANES, (unit + 1) * LANES)
                if win == 0:
                    return jnp.concatenate([kp_ref[:, cols], kc_ref[0:w, cols]], axis=0)
                return kc_ref[(win - 1) * w:(win + 1) * w, cols]
            if win == 0:
                vt = jnp.concatenate([vp_ref[g], vc_ref[g, :, 0:w]], axis=1)
            else:
                vt = vc_ref[g, :, (win - 1) * w:(win + 1) * w]
            q2 = jnp.concatenate([q_ref[rows, (2 * g + u) * LANES:(2 * g + u + 1) * LANES] for u in range(2)], axis=0)
            s = jnp.concatenate([_dot_nt(keys(2 * g), q2), _dot_nt(keys(2 * g + 1), q2)], axis=1) + bias
            order = [per_group * g, per_group * g + 2, per_group * g + 1, per_group * g + 3]
            sink = jnp.zeros((1, per_group * w), F32)
            for slot, h in enumerate(order):
                sink = jnp.where(lane_head == slot, sink_ref[h] * LOG2E, sink)
            parts = [s[gg] for gg in groups]
            while len(parts) > 1:
                parts = [jnp.maximum(parts[j], parts[j + 1]) for j in range(0, len(parts), 2)]
            m = jnp.maximum(jnp.max(parts[0], axis=0, keepdims=True), sink)
            p = jnp.exp2(s - m)
            o = _dot(vt, p.astype(BF16))
            o = (o / (o[64:65, :] + jnp.exp2(sink - m))).T
            for slot, h in enumerate(order):
                o_ref[rows, h * 64:(h + 1) * 64] = o[slot * w:(slot + 1) * w, :64].astype(BF16)


def _swa(sinks, aq, ak, avt, b, s):
    n = aq.shape[0]
    w = A_WINDOW
    qb = _att_qb(s)
    nb = s // qb
    wpb = qb // w
    cur = lambda bb, i: (bb * nb + i, 0)
    prev = lambda bb, i: ((bb * nb + i) * wpb - jnp.minimum(i, 1), 0)
    cur_t = lambda bb, i: (0, 0, bb * nb + i)
    prev_t = lambda bb, i: (0, 0, (bb * nb + i) * wpb - jnp.minimum(i, 1))
    return pl.pallas_call(
        _swa_kernel,
        grid=(b, nb),
        in_specs=[pl.BlockSpec(memory_space=pltpu.SMEM),
                  pl.BlockSpec((qb, 512), cur),
                  pl.BlockSpec((qb, 512), cur), pl.BlockSpec((w, 512), prev),
                  pl.BlockSpec((A_KV_HEADS, LANES, qb), cur_t), pl.BlockSpec((A_KV_HEADS, LANES, w), prev_t)],
        out_specs=pl.BlockSpec((qb, 512), cur),
        out_shape=jax.ShapeDtypeStruct((n, 512), BF16),
        compiler_params=pltpu.CompilerParams(dimension_semantics=("parallel", "parallel"),
                                             vmem_limit_bytes=VMEM_LIMIT),
    )(sinks, aq, ak, ak, avt, avt)


ATT_KC = 256


def _att_qb(s):
    return min(512, s)


def _fold_max(s):
    return jnp.maximum(s[:, :LANES], s[:, LANES:])


def _unrolled(n, body, carry):
    for c in range(n):
        carry = body(c, carry)
    return carry


def _dsa_kernel(bq_ref, iq_ref, iwt_ref, bk_ref, ik_ref, vt_ref, _aliased_out_ref, o_ref,
                sc_ref, s_ref, tau_ref, j_ref, ist_ref, qst_ref, m_ref, acc_ref, *, blk, topk, idx_bits):
    i = blk
    qb = bq_ref.shape[0]
    kc = ATT_KC
    per = qb // kc
    n_full = per * i
    n_chunks = n_full + per
    kidx = lax.broadcasted_iota(jnp.int32, (kc, qb), 0)
    rpos = lax.broadcasted_iota(jnp.int32, (kc, qb), 1)
    neg_inf = float("-inf")
    kstart = lambda c: c * kc
    heads = range(B_HEADS)
    hrows = lambda h: slice(h * qb, (h + 1) * qb)
    groups = [slice(g * SUBLANES, (g + 1) * SUBLANES) for g in range(kc // SUBLANES)]

    def fold(x, op, ways=4):
        parts = [x[g] for g in groups[:ways]]
        for j, g in enumerate(groups[ways:]):
            parts[j % ways] = op(parts[j % ways], x[g])
        while len(parts) > 1:
            parts = [op(parts[j], parts[j + 1]) for j in range(0, len(parts), 2)]
        return parts[0]

    for h in range(IDX_HEADS):
        ist_ref[hrows(h), :] = iq_ref[:, h * LANES:(h + 1) * LANES]

    def score(c):
        lg = _dot_nt(ik_ref[pl.ds(kstart(c), kc), :], ist_ref[...])
        acc = None
        for h in range(IDX_HEADS):
            t = iwt_ref[h:h + 1, :] * jnp.maximum(lg[:, hrows(h)], 0.0)
            acc = t if acc is None else acc + t
        return acc

    rw = qb // per

    def put_scores(c, sc):
        for r in range(per):
            sc_ref[c, r] = sc[:, r * rw:(r + 1) * rw]

    def idx_body(c, carry):
        put_scores(c, score(c))
        return carry

    _unrolled(n_full, idx_body, 0)
    for d in range(per):
        put_scores(n_full + d, jnp.where(d * kc + kidx <= rpos, score(n_full + d), neg_inf))

    kidx_r = lax.broadcasted_iota(jnp.int32, (kc, rw), 0)
    for r in range(per):
        rows = slice(r * rw, (r + 1) * rw)
        tpos = i * qb + r * rw + lax.broadcasted_iota(jnp.int32, (1, rw), 1)
        n_r = n_full + r + 1
        if i * qb + (r + 1) * rw <= int(topk):
            tau_ref[:, rows] = jnp.full((SUBLANES, rw), neg_inf, F32)
            j_ref[:, rows] = jnp.broadcast_to(tpos, (SUBLANES, rw))
            continue

        def count(pred_fn):
            def cbody(c, acc):
                k = sc_ref[c, r]
                return acc + fold(jnp.where(pred_fn(k, c), 1.0, 0.0), jnp.add, ways=1)
            acc = _unrolled(n_r, cbody, jnp.zeros((SUBLANES, rw), F32))
            return jnp.sum(acc, axis=0, keepdims=True)

        def key_to_f32(key_u):
            key = key_u ^ INT_MIN
            return pltpu.bitcast(key ^ ((key >> 31) & 0x7FFFFFFF), F32)

        def bit_body(step, prefix):
            cand_u = prefix | lax.shift_left(jnp.int32(1), 31 - step)
            cand = key_to_f32(cand_u)
            cnt = count(lambda k, c: k >= cand)
            return jnp.where(cnt >= topk, cand_u, prefix)

        prefix = lax.fori_loop(0, 32, bit_body, jnp.zeros((1, rw), jnp.int32))
        tau = key_to_f32(prefix)
        tau = jnp.where(tau != tau, neg_inf, tau)
        c_gt = count(lambda k, c: k > tau)
        c_eq = count(lambda k, c: k == tau)
        need = topk - c_gt

        def tie_search(_):
            def jbody(step, q):
                cand = q | lax.shift_left(jnp.int32(1), idx_bits - 1 - step)
                cnt = count(lambda k, c: (k == tau) & (c * kc + kidx_r < cand))
                return jnp.where(cnt < need, cand, q)
            return lax.fori_loop(0, idx_bits, jbody, jnp.zeros((1, rw), jnp.int32))

        any_split = jnp.max(c_eq - need) > 0.0
        jcut = lax.cond(any_split, tie_search, lambda _: jnp.full((1, rw), 2 ** idx_bits, jnp.int32), 0)
        few = tpos < int(topk)
        tau_ref[:, rows] = jnp.broadcast_to(jnp.where(few, neg_inf, tau), (SUBLANES, rw))
        j_ref[:, rows] = jnp.broadcast_to(jnp.where(few, tpos, jcut), (SUBLANES, rw))

    for h in heads:
        qst_ref[hrows(h), :] = bq_ref[:, h * LANES:(h + 1) * LANES]
    m_ref[...] = jnp.full(m_ref.shape, NEG_BIG, F32)

    def max_body(c, carry):
        s = _dot_nt(bk_ref[pl.ds(kstart(c), kc), :], qst_ref[...])
        k = jnp.concatenate([sc_ref[c, r] for r in range(per)], axis=1)
        tau = tau_ref[0:1, :]
        sel = (k > tau) | ((k == tau) & (c * kc + kidx <= j_ref[0:1, :]))
        bias = jnp.where(sel, 0.0, NEG_BIG)
        for h in heads:
            sm = s[:, hrows(h)] + bias
            s_ref[c, :, hrows(h)] = sm
            m_ref[:, hrows(h)] = jnp.maximum(m_ref[:, hrows(h)], fold(sm, jnp.maximum, ways=1))
        return carry

    _unrolled(n_chunks, max_body, 0)
    m_ref[...] = jnp.broadcast_to(jnp.max(m_ref[...], axis=0, keepdims=True), m_ref.shape)

    acc_ref[...] = jnp.zeros(acc_ref.shape, F32)

    def pv_body(c, carry):
        p = jnp.exp2(s_ref[c] - m_ref[0:1, :]).astype(BF16)
        acc_ref[...] += _dot(vt_ref[c], p)
        return carry

    _unrolled(n_chunks, pv_body, 0)
    for h in heads:
        acc = acc_ref[:, hrows(h)]
        o_ref[:, h * 64:(h + 1) * 64] = (acc / acc[64:65, :]).T[:, :64].astype(BF16)


def _dsa(bq, iq, iwt, bk, ik, vt, b, s):
    n = bq.shape[0]
    qb = _att_qb(s)
    nb = s // qb
    nkc = s // ATT_KC
    topk = min(IDX_TOPK_MAX, s // 4)
    idx_bits = max(1, int(math.ceil(math.log2(s))))
    seq = lambda bb: (bb, 0)
    out = jnp.zeros((n, 256), BF16)
    for i in range(nb):
        blk = lambda bb, i=i: (bb * nb + i, 0)
        n_kc = (i + 1) * (qb // ATT_KC)
        out = pl.pallas_call(
            functools.partial(_dsa_kernel, blk=i, topk=float(topk), idx_bits=idx_bits),
            grid=(b,),
            in_specs=[pl.BlockSpec((qb, 512), blk), pl.BlockSpec((qb, 512), blk),
                      pl.BlockSpec((8, qb), lambda bb, i=i: (0, bb * nb + i)),
                      pl.BlockSpec((s, 128), seq), pl.BlockSpec((s, 128), seq),
                      pl.BlockSpec((nkc, LANES, ATT_KC), lambda bb: (bb, 0, 0)),
                      pl.BlockSpec(memory_space=pl.ANY)],
            out_specs=pl.BlockSpec((qb, 256), blk),
            out_shape=jax.ShapeDtypeStruct((n, 256), BF16),
            input_output_aliases={6: 0},
            scratch_shapes=[pltpu.VMEM((n_kc, qb // ATT_KC, ATT_KC, ATT_KC), F32),
                            pltpu.VMEM((n_kc, ATT_KC, B_HEADS * qb), F32),
                            pltpu.VMEM((SUBLANES, qb), F32), pltpu.VMEM((SUBLANES, qb), jnp.int32),
                            pltpu.VMEM((IDX_HEADS * qb, LANES), BF16), pltpu.VMEM((B_HEADS * qb, LANES), BF16),
                            pltpu.VMEM((SUBLANES, B_HEADS * qb), F32), pltpu.VMEM((LANES, B_HEADS * qb), F32)],
            compiler_params=pltpu.CompilerParams(dimension_semantics=("parallel",), vmem_limit_bytes=VMEM_LIMIT),
        )(bq, iq, iwt, bk, ik, vt, out)
    return out


def _mla_kernel(q_ref, k_ref, v_ref, _aliased_out_ref, o_ref, s_ref, m_ref, acc_ref, *, blk):
    qb = q_ref.shape[0]
    kc = ATT_KC
    per = qb // kc
    n_full = per * blk
    n_chunks = n_full + per
    row = lax.broadcasted_iota(jnp.int32, (qb, kc), 0)
    col = lax.broadcasted_iota(jnp.int32, (qb, kc), 1)
    heads = range(C_HEADS)
    hs = lambda h: slice(h * LANES, (h + 1) * LANES)
    keys = lambda c: slice(c * kc, (c + 1) * kc)
    n_halves = kc // LANES
    m_ref[...] = jnp.full(m_ref.shape, NEG_BIG, F32)

    for c in range(n_chunks):
        diag = c - n_full
        for h in heads:
            s = _dot_nt(q_ref[:, hs(h)], k_ref[keys(c), hs(h)])
            if diag >= 0:
                s = s + jnp.where(diag * kc + col <= row, 0.0, NEG_BIG)
            s_ref[h, c] = s
            m_ref[h] = jnp.maximum(m_ref[h], _fold_max(s))
    for h in heads:
        m_ref[h] = jnp.broadcast_to(jnp.max(m_ref[h], axis=1, keepdims=True), (qb, LANES))
    acc_ref[...] = jnp.zeros(acc_ref.shape, F32)

    for c in range(n_chunks):
        for h in heads:
            p = jnp.exp2(s_ref[h, c] - jnp.concatenate([m_ref[h]] * n_halves, axis=1))
            acc_ref[h] += _dot(p.astype(BF16), v_ref[keys(c), hs(h)])
    for h in heads:
        acc = acc_ref[h]
        o_ref[:, h * 64:(h + 1) * 64] = (acc[:, :64] / acc[:, 64:65]).astype(BF16)


def _mla(qc, kc, vc, b, s):
    n = qc.shape[0]
    qb = _att_qb(s)
    nb = s // qb
    seq = lambda bb: (bb, 0)
    out = jnp.zeros((n, 256), BF16)
    for i in range(nb):
        blk = lambda bb, i=i: (bb * nb + i, 0)
        n_kc = (i + 1) * (qb // ATT_KC)
        out = pl.pallas_call(
            functools.partial(_mla_kernel, blk=i),
            grid=(b,),
            in_specs=[pl.BlockSpec((qb, 512), blk), pl.BlockSpec((s, 512), seq), pl.BlockSpec((s, 512), seq),
                      pl.BlockSpec(memory_space=pl.ANY)],
            out_specs=pl.BlockSpec((qb, 256), blk),
            out_shape=jax.ShapeDtypeStruct((n, 256), BF16),
            input_output_aliases={3: 0},
            scratch_shapes=[pltpu.VMEM((C_HEADS, n_kc, qb, ATT_KC), F32), pltpu.VMEM((C_HEADS, qb, LANES), F32),
                            pltpu.VMEM((C_HEADS, qb, LANES), F32)],
            compiler_params=pltpu.CompilerParams(dimension_semantics=("parallel",), vmem_limit_bytes=VMEM_LIMIT),
        )(qc, kc, vc, out)
    return out


def _layer_norm(z, g, b):
    mu = jnp.mean(z, axis=1, keepdims=True)
    zc = z - mu
    var = jnp.mean(zc * zc, axis=1, keepdims=True)
    return zc * lax.rsqrt(var + LN_EPS) * g + b


def _outproj_kernel(x_ref, oa_ref, ob_ref, oc_ref, wo_ref, g_ref, b_ref, wr_ref, rb_ref, x1_ref, gates_ref,
                    *, alpha):
    y = _dot(oa_ref[...], wo_ref[0:512, :]) + _dot(ob_ref[...], wo_ref[512:768, :]) \
        + _dot(oc_ref[...], wo_ref[768:1024, :])
    x1 = _layer_norm(alpha * x_ref[...] + y, g_ref[...], b_ref[...])
    x1_ref[...] = x1

    x1h = x1.astype(BF16)
    x1l = (x1 - x1h.astype(F32)).astype(BF16)
    wr = wr_ref[...]
    wrh = wr.astype(BF16)
    wrl = (wr - wrh.astype(F32)).astype(BF16)
    hi = _dot_nt(jnp.concatenate([wrh, wrl], axis=0), x1h)
    logits = hi[:N_EXPERTS] + hi[N_EXPERTS:] + _dot_nt(wrh, x1l)
    scores = jax.nn.sigmoid(logits)
    biased = scores + rb_ref[...]
    rows = [biased[e:e + 1, :] for e in range(N_EXPERTS)]
    gscore = []
    for g in range(N_GROUPS):
        v = rows[g * 4:(g + 1) * 4]
        best = None
        for a in range(4):
            for c in range(a + 1, 4):
                pair = v[a] + v[c]
                best = pair if best is None else jnp.maximum(best, pair)
        gscore.append(best)
    gmax = jnp.maximum(jnp.maximum(gscore[0], gscore[1]), jnp.maximum(gscore[2], gscore[3]))
    taken = jnp.zeros_like(gmax) > 1.0
    sel_rows = []
    for g in range(N_GROUPS):
        g_sel = (gscore[g] == gmax) & jnp.logical_not(taken)
        taken = taken | g_sel
        v = rows[g * 4:(g + 1) * 4]
        for e in range(4):
            rank = jnp.zeros_like(gmax)
            for j in range(4):
                if j == e:
                    continue
                ahead = (v[j] > v[e]) | ((v[j] == v[e]) & (j < e))
                rank = rank + jnp.where(ahead, 1.0, 0.0)
            sel_rows.append(g_sel & (rank < 2.0))
    wsel = [jnp.where(sel_rows[e], scores[e:e + 1, :], 0.0) for e in range(N_EXPERTS)]
    total = wsel[0]
    for e in range(1, N_EXPERTS):
        total = total + wsel[e]
    tm = total.shape[1]
    gates_t = jnp.concatenate(wsel + [jnp.zeros((LANES - N_EXPERTS, tm), F32)], axis=0) / total
    gates = gates_t.T
    for g in range(N_GROUPS):
        gates_ref[g] = gates[:, g * EXPERTS_PER_GROUP:(g + 1) * EXPERTS_PER_GROUP]


def _outproj(x2, oa, ob, oc, wo, g, bb, wr_t, rbias, alpha, tm):
    n, d = x2.shape
    nt = n // tm
    full = lambda a: pl.BlockSpec(a.shape, lambda t: (0,) * a.ndim)
    row = lambda c: pl.BlockSpec((tm, c), lambda t: (t, 0))
    return pl.pallas_call(
        functools.partial(_outproj_kernel, alpha=alpha),
        grid=(nt,),
        in_specs=[row(d), row(512), row(256), row(256), full(wo), full(g), full(bb), full(wr_t), full(rbias)],
        out_specs=[row(d), pl.BlockSpec((N_GROUPS, tm, EXPERTS_PER_GROUP), lambda t: (0, t, 0))],
        out_shape=[jax.ShapeDtypeStruct((n, d), F32),
                   jax.ShapeDtypeStruct((N_GROUPS, n, EXPERTS_PER_GROUP), F32)],
        compiler_params=pltpu.CompilerParams(dimension_semantics=("parallel",), vmem_limit_bytes=VMEM_LIMIT),
    )(x2, oa, ob, oc, wo, g, bb, wr_t, rbias)


def _moe_kernel(x_ref, gates_ref, wg_ref, wu_ref, wd_ref, g_ref, b_ref, o_ref, xb_ref, acc_ref, *, alpha):
    grp = pl.program_id(1)

    @pl.when(grp == 0)
    def _():
        xb_ref[...] = x_ref[...].astype(BF16)

    xb = xb_ref[...]
    hs = []
    for j in range(EXPERTS_PER_GROUP):
        hg = _dot(xb, wg_ref[j])
        hu = _dot(xb, wu_ref[j])
        hs.append((hg * jax.nn.sigmoid(hg) * hu * gates_ref[0, :, j:j + 1]).astype(BF16))
    wd = wd_ref[...]
    y = _dot(jnp.concatenate(hs, axis=1), wd.reshape(wd.shape[0] * wd.shape[1], wd.shape[2]))

    @pl.when(grp == 0)
    def _():
        acc_ref[...] = y

    @pl.when(grp > 0)
    def _():
        acc_ref[...] += y

    @pl.when(grp == N_GROUPS - 1)
    def _():
        o_ref[...] = _layer_norm(alpha * x_ref[...] + acc_ref[...], g_ref[...], b_ref[...])


def _moe(x1, gates_g, wg, wu, wd, g, bb, alpha, tm):
    n, d = x1.shape
    nt = n // tm
    f = wg.shape[2]
    epg = EXPERTS_PER_GROUP
    full = lambda a: pl.BlockSpec(a.shape, lambda t, e: (0,) * a.ndim)
    return pl.pallas_call(
        functools.partial(_moe_kernel, alpha=alpha),
        grid=(nt, N_GROUPS),
        in_specs=[pl.BlockSpec((tm, d), lambda t, e: (t, 0)), pl.BlockSpec((1, tm, epg), lambda t, e: (e, t, 0)),
                  pl.BlockSpec((epg, d, f), lambda t, e: (e, 0, 0)), pl.BlockSpec((epg, d, f), lambda t, e: (e, 0, 0)),
                  pl.BlockSpec((epg, f, d), lambda t, e: (e, 0, 0)), full(g), full(bb)],
        out_specs=pl.BlockSpec((tm, d), lambda t, e: (t, 0)),
        out_shape=jax.ShapeDtypeStruct((n, d), F32),
        scratch_shapes=[pltpu.VMEM((tm, d), BF16), pltpu.VMEM((tm, d), F32)],
        compiler_params=pltpu.CompilerParams(dimension_semantics=("parallel", "arbitrary"),
                                             vmem_limit_bytes=VMEM_LIMIT),
    )(x1, gates_g, wg, wu, wd, g, bb)


def kernel(x, w_in, attn_sinks, c_q_norm_g, c_kv_norm_g, w_uq, w_ukv, w_out, ln1_g, ln1_b, w_router, router_bias,
           w_gate, w_up, w_down, ln2_g, ln2_b):
    b, s, d = x.shape
    depth = w_in.shape[0]
    n = b * s
    alpha = (2 * depth) ** 0.25
    tm = min(512, s)
    tables = _rope_tables(s)
    wr_t = w_router.T
    rbias = router_bias.reshape(N_EXPERTS, 1)
    x2 = x.reshape(n, d)
    for l in range(depth):
        w_p, w_t = _pack_w_in(w_in[l])
        wuq_p, wukv_p = _pack_mla_w(w_uq[l], w_ukv[l])
        aq, ak, bq, bk, ik, iq, avt, vt, iwt, qc, kc, vc = _inproj(
            x2, w_p, w_t, tables, c_q_norm_g[l].reshape(1, -1), c_kv_norm_g[l].reshape(1, -1), wuq_p, wukv_p, s, tm)
        oa = _swa(attn_sinks[l], aq, ak, avt, b, s)
        ob = _dsa(bq, iq, iwt, bk, ik, vt, b, s)
        oc = _mla(qc, kc, vc, b, s)
        x1, gates_g = _outproj(x2, oa, ob, oc, w_out[l].astype(BF16), ln1_g[l].reshape(1, d), ln1_b[l].reshape(1, d),
                               wr_t, rbias, alpha, tm)
        x2 = _moe(x1, gates_g, w_gate[l].astype(BF16), w_up[l].astype(BF16), w_down[l].astype(BF16),
                  ln2_g[l].reshape(1, d), ln2_b[l].reshape(1, d), alpha, min(1024, n))
    return x2.reshape(b, s, d)
```

```python
import functools
import math

import jax
import jax.numpy as jnp
import numpy as np
from jax import lax
from jax.experimental import pallas as pl
from jax.experimental.pallas import tpu as pltpu

HEAD_DIM = 64
ROPE_THETA = 10000.0
A_HEADS = 8
A_KV_HEADS = 2
A_WINDOW = 128
B_HEADS = 4
IDX_HEADS = 4
IDX_DIM = 64
IDX_TOPK_MAX = 256
C_HEADS = 4
C_NOPE = 64
C_ROPE = 32
C_V = 64
C_Q_RANK = 256
C_KV_RANK = 128
N_EXPERTS = 16
N_GROUPS = 4
EXPERTS_PER_GROUP = 4
D_EXPERT = 256
LN_EPS = 1e-5
RMS_EPS = 1e-6

LANES = 128
SUBLANES = 8
PACKED_ROWS = 16
NEG_BIG = -1e30
LOG2E = math.log2(math.e)
INT_MIN = -(2 ** 31)
VMEM_LIMIT = 56 * 1024 * 1024

BF16 = jnp.bfloat16
F32 = jnp.float32

_NT = (((1,), (1,)), ((), ()))


def _dot(a, b):
    return jnp.dot(a, b, preferred_element_type=F32)


def _dot_nt(a, b):
    return lax.dot_general(a, b, _NT, preferred_element_type=F32)


N_ROPE_UNITS = 10


def _inproj_kernel(x_ref, w_ref, ch_ref, s1h_ref, s2h_ref, cr_ref, s1r_ref, s2r_ref, gq_ref, gkv_ref,
                   wuq_ref, wukv_ref, wt_ref,
                   aq_ref, ak_ref, bq_ref, bk_ref, ik_ref, iq_ref, avt_ref, vt_ref, iwt_ref, qc_ref, kc_ref, vc_ref,
                   *, b_scale, c_scale):
    xb = x_ref[...].astype(BF16)
    tm = xb.shape[0]
    lane = lax.broadcasted_iota(jnp.int32, (tm, LANES), 1)
    ch, s1h, s2h = ch_ref[...], s1h_ref[...], s2h_ref[...]
    cr, s1r, s2r = cr_ref[...], s1r_ref[...], s2r_ref[...]
    low = lane < 64

    def rope_h(u):
        return u * ch + pltpu.roll(u, 96, 1) * s1h + pltpu.roll(u, 32, 1) * s2h

    def rope_r(u):
        return u * cr + pltpu.roll(u, 112, 1) * s1r + pltpu.roll(u, 16, 1) * s2r

    def split(u):
        return jnp.where(low, u, 0.0), jnp.where(low, pltpu.roll(u, 64, 1), 0.0)

    def put(ref, k, u):
        ref[:, k * LANES:(k + 1) * LANES] = u.astype(BF16)

    for g in range(N_ROPE_UNITS // 2):
        hg = _dot(xb, w_ref[:, g * 256:(g + 1) * 256])
        for half in range(2):
            unit = 2 * g + half
            u = rope_h(hg[:, half * LANES:(half + 1) * LANES])
            if unit < 4:
                put(aq_ref, unit, u * LOG2E)
            elif unit == 4:
                swapped = pltpu.roll(u, 64, 1)
                put(ak_ref, 0, jnp.where(low, u, 0.0))
                put(ak_ref, 1, jnp.where(low, 0.0, swapped))
                put(ak_ref, 2, jnp.where(low, swapped, 0.0))
                put(ak_ref, 3, jnp.where(low, 0.0, u))
            elif unit < 7:
                h0, h1 = split(u * b_scale)
                put(bq_ref, 2 * (unit - 5), h0)
                put(bq_ref, 2 * (unit - 5) + 1, h1)
            elif unit == 7:
                h0, h1 = split(u)
                put(bk_ref, 0, h0)
                put(ik_ref, 0, h1)
            else:
                h0, h1 = split(u)
                put(iq_ref, 2 * (unit - 8), h0)
                put(iq_ref, 2 * (unit - 8) + 1, h1)

    hb = _dot_nt(wt_ref[...], xb)
    sub = lax.broadcasted_iota(jnp.int32, (LANES, tm), 0)
    with_ones = lambda piece: jnp.where(sub == 64, 1.0, hb[piece * LANES:(piece + 1) * LANES]).astype(BF16)
    vt = with_ones(0)
    for j in range(tm // ATT_KC):
        vt_ref[j] = vt[:, j * ATT_KC:(j + 1) * ATT_KC]
    iwt_ref[...] = hb[LANES:LANES + 8]
    for g in range(A_KV_HEADS):
        avt_ref[g] = with_ones(2 + g)

    cq = _dot(xb, w_ref[:, 1280:1536])
    cqn = cq * lax.rsqrt(jnp.mean(cq * cq, axis=1, keepdims=True) + RMS_EPS) * gq_ref[...]
    qc = _dot(cqn.astype(BF16), wuq_ref[...])
    for h in range(C_HEADS):
        u = qc[:, h * LANES:(h + 1) * LANES]
        qc_ref[:, h * LANES:(h + 1) * LANES] = (rope_r(u) * c_scale).astype(BF16)

    ckv = _dot(xb, w_ref[:, 1536:1792])
    ckv_lat = ckv[:, :LANES]
    kr = rope_r(ckv[:, LANES:])
    ckvn = ckv_lat * lax.rsqrt(jnp.mean(ckv_lat * ckv_lat, axis=1, keepdims=True) + RMS_EPS) * gkv_ref[...]
    kv = _dot(ckvn.astype(BF16), wukv_ref[...])
    for h in range(C_HEADS):
        kc_ref[:, h * LANES:(h + 1) * LANES] = (kv[:, h * LANES:(h + 1) * LANES] + kr).astype(BF16)
        v = kv[:, (C_HEADS + h) * LANES:(C_HEADS + h + 1) * LANES]
        vc_ref[:, h * LANES:(h + 1) * LANES] = jnp.where(lane == 64, 1.0, v).astype(BF16)


def _pack_w_in(w):
    d = w.shape[0]
    z = lambda n: jnp.zeros((d, n), F32)
    o = np.cumsum([0, 512, 128, 128, 256, 64, 64, 256, 64, 4, 256, 128, 32]).tolist()
    a_q, a_k, a_v, b_q, b_k, b_v, i_q, i_k, i_w, c_q, c_kv, c_kr = [w[:, o[j]:o[j + 1]] for j in range(12)]
    qs = HEAD_DIM ** -0.5
    ws = (IDX_HEADS * IDX_DIM) ** -0.5
    cols = [a_q * qs, a_k, b_q, b_k, i_k, i_q,
            c_q, c_kv, z(64), c_kr, z(32)]
    rows_t = jnp.concatenate([b_v, z(64), i_w * ws, z(LANES - IDX_HEADS),
                              a_v[:, :64], z(64), a_v[:, 64:], z(64)], axis=1).T
    return jnp.concatenate(cols, axis=1).astype(BF16), rows_t.astype(BF16)


def _pack_mla_w(w_uq, w_ukv):
    r = w_uq.shape[0]
    q = w_uq.reshape(r, C_HEADS, C_NOPE + C_ROPE)
    q = jnp.pad(q, ((0, 0), (0, 0), (0, LANES - C_NOPE - C_ROPE))).reshape(r, C_HEADS * LANES)
    r2 = w_ukv.shape[0]
    kv = w_ukv.reshape(r2, C_HEADS, C_NOPE + C_V)
    k = jnp.pad(kv[:, :, :C_NOPE], ((0, 0), (0, 0), (0, LANES - C_NOPE))).reshape(r2, C_HEADS * LANES)
    v = jnp.pad(kv[:, :, C_NOPE:], ((0, 0), (0, 0), (0, LANES - C_V))).reshape(r2, C_HEADS * LANES)
    return q.astype(BF16), jnp.concatenate([k, v], axis=1).astype(BF16)


def _rope_tables(s):
    pos = jnp.arange(s, dtype=F32)[:, None]
    lane = np.arange(LANES)
    inv_h = 1.0 / (ROPE_THETA ** (jnp.arange(0, HEAD_DIM, 2, dtype=F32) / HEAD_DIM))
    ang = pos * inv_h[None, :]
    cos, sin = jnp.cos(ang), jnp.sin(ang)
    j = lane % 32
    lo = jnp.asarray((lane % 64) < 32)
    ch = cos[:, j]
    s1h = jnp.where(lo, -sin[:, j], 0.0)
    s2h = jnp.where(lo, 0.0, sin[:, j])
    inv_r = 1.0 / (ROPE_THETA ** (jnp.arange(0, C_ROPE, 2, dtype=F32) / C_ROPE))
    angr = pos * inv_r[None, :]
    cosr, sinr = jnp.cos(angr), jnp.sin(angr)
    jr = lane % 16
    in_rope = jnp.asarray((lane >= 64) & (lane < 96))
    first = jnp.asarray((lane >= 64) & (lane < 80))
    second = jnp.asarray((lane >= 80) & (lane < 96))
    cr = jnp.where(in_rope, cosr[:, jr], 1.0)
    s1r = jnp.where(first, -sinr[:, jr], 0.0)
    s2r = jnp.where(second, sinr[:, jr], 0.0)
    return ch, s1h, s2h, cr, s1r, s2r


def _inproj(x2, w_p, w_t, tables, gq, gkv, wuq_p, wukv_p, s, tm):
    n, d = x2.shape
    nt = n // tm
    spt = s // tm
    cpt = tm // ATT_KC
    tab_spec = pl.BlockSpec((tm, LANES), lambda t: (t % spt, 0))
    full = lambda a: pl.BlockSpec(a.shape, lambda t: (0,) * a.ndim)
    row = lambda c: pl.BlockSpec((tm, c), lambda t: (t, 0))
    cols_a, cols_c = [512, 512, 512, 128, 128, 512], [512, 512, 512]
    out_specs = ([row(c) for c in cols_a]
                 + [pl.BlockSpec((A_KV_HEADS, LANES, tm), lambda t: (0, 0, t)),
                    pl.BlockSpec((cpt, LANES, ATT_KC), lambda t: (t, 0, 0)), pl.BlockSpec((8, tm), lambda t: (0, t))]
                 + [row(c) for c in cols_c])
    out_shape = ([jax.ShapeDtypeStruct((n, c), BF16) for c in cols_a]
                 + [jax.ShapeDtypeStruct((A_KV_HEADS, LANES, n), BF16),
                    jax.ShapeDtypeStruct((n // ATT_KC, LANES, ATT_KC), BF16), jax.ShapeDtypeStruct((8, n), F32)]
                 + [jax.ShapeDtypeStruct((n, c), BF16) for c in cols_c])
    return pl.pallas_call(
        functools.partial(_inproj_kernel, b_scale=HEAD_DIM ** -0.5 * LOG2E,
                          c_scale=(C_NOPE + C_ROPE) ** -0.5 * LOG2E),
        grid=(nt,),
        in_specs=[row(d), full(w_p)] + [tab_spec] * 6 + [full(gq), full(gkv), full(wuq_p), full(wukv_p), full(w_t)],
        out_specs=out_specs,
        out_shape=out_shape,
        compiler_params=pltpu.CompilerParams(dimension_semantics=("parallel",), vmem_limit_bytes=VMEM_LIMIT),
    )(x2, w_p, *tables, gq, gkv, wuq_p, wukv_p, w_t)


def _swa_kernel(sink_ref, q_ref, kc_ref, kp_ref, vc_ref, vp_ref, o_ref):
    i = pl.program_id(1)
    w = A_WINDOW
    qb = q_ref.shape[0]
    kj = lax.broadcasted_iota(jnp.int32, (2 * w, w), 0)
    qi = lax.broadcasted_iota(jnp.int32, (2 * w, w), 1)
    diff = qi + w - kj
    in_window = (diff >= 0) & (diff < w)
    groups = [slice(g * SUBLANES, (g + 1) * SUBLANES) for g in range(2 * w // SUBLANES)]
    lane_head = lax.broadcasted_iota(jnp.int32, (1, 4 * w), 1) // w
    per_group = A_HEADS // A_KV_HEADS
    for win in range(qb // w):
        rows = slice(win * w, (win + 1) * w)
        ok = in_window if win > 0 else in_window & ((kj >= w) | (i > 0))
        bias1 = jnp.where(ok, 0.0, NEG_BIG)
        bias = jnp.concatenate([bias1] * per_group, axis=1)
        for g in range(A_KV_HEADS):
            def keys(unit):
                cols = slice(unit * LANES, (unit + 1) * LANES)
                if win == 0:
                    return jnp.concatenate([kp_ref[:, cols], kc_ref[0:w, cols]], axis=0)
                return kc_ref[(win - 1) * w:(win + 1) * w, cols]
            if win == 0:
                vt = jnp.concatenate([vp_ref[g], vc_ref[g, :, 0:w]], axis=1)
            else:
                vt = vc_ref[g, :, (win - 1) * w:(win + 1) * w]
            q2 = jnp.concatenate([q_ref[rows, (2 * g + u) * LANES:(2 * g + u + 1) * LANES] for u in range(2)], axis=0)
            s = jnp.concatenate([_dot_nt(keys(2 * g), q2), _dot_nt(keys(2 * g + 1), q2)], axis=1) + bias
            order = [per_group * g, per_group * g + 2, per_group * g + 1, per_group * g + 3]
            sink = jnp.zeros((1, per_group * w), F32)
            for slot, h in enumerate(order):
                sink = jnp.where(lane_head == slot, sink_ref[h] * LOG2E, sink)
            parts = [s[gg] for gg in groups]
            while len(parts) > 1:
                parts = [jnp.maximum(parts[j], parts[j + 1]) for j in range(0, len(parts), 2)]
            m = jnp.maximum(jnp.max(parts[0], axis=0, keepdims=True), sink)
            p = jnp.exp2(s - m)
            o = _dot(vt, p.astype(BF16))
            o = (o / (o[64:65, :] + jnp.exp2(sink - m))).T
            for slot, h in enumerate(order):
                o_ref[rows, h * 64:(h + 1) * 64] = o[slot * w:(slot + 1) * w, :64].astype(BF16)


def _swa(sinks, aq, ak, avt, b, s):
    n = aq.shape[0]
    w = A_WINDOW
    qb = _att_qb(s)
    nb = s // qb
    wpb = qb // w
    cur = lambda bb, i: (bb * nb + i, 0)
    prev = lambda bb, i: ((bb * nb + i) * wpb - jnp.minimum(i, 1), 0)
    cur_t = lambda bb, i: (0, 0, bb * nb + i)
    prev_t = lambda bb, i: (0, 0, (bb * nb + i) * wpb - jnp.minimum(i, 1))
    return pl.pallas_call(
        _swa_kernel,
        grid=(b, nb),
        in_specs=[pl.BlockSpec(memory_space=pltpu.SMEM),
                  pl.BlockSpec((qb, 512), cur),
                  pl.BlockSpec((qb, 512), cur), pl.BlockSpec((w, 512), prev),
                  pl.BlockSpec((A_KV_HEADS, LANES, qb), cur_t), pl.BlockSpec((A_KV_HEADS, LANES, w), prev_t)],
        out_specs=pl.BlockSpec((qb, 512), cur),
        out_shape=jax.ShapeDtypeStruct((n, 512), BF16),
        compiler_params=pltpu.CompilerParams(dimension_semantics=("parallel", "parallel"),
                                             vmem_limit_bytes=VMEM_LIMIT),
    )(sinks, aq, ak, ak, avt, avt)


ATT_KC = 256
COARSE_CHAINS = 2


def _att_qb(s):
    return min(512, s)


def _fold_max(s):
    return jnp.maximum(s[:, :LANES], s[:, LANES:])


def _unrolled(n, body, carry):
    for c in range(n):
        carry = body(c, carry)
    return carry


def _dsa_kernel(bq_ref, iq_ref, iwt_ref, bk_ref, ik_ref, vt_ref, _aliased_out_ref, o_ref,
                sc_ref, sc16_ref, s_ref, tau_ref, j_ref, ist_ref, qst_ref, m_ref, acc_ref, *, blk, topk, idx_bits):
    i = blk
    qb = bq_ref.shape[0]
    kc = ATT_KC
    per = qb // kc
    n_full = per * i
    n_chunks = n_full + per
    kidx = lax.broadcasted_iota(jnp.int32, (kc, qb), 0)
    rpos = lax.broadcasted_iota(jnp.int32, (kc, qb), 1)
    neg_inf = float("-inf")
    kstart = lambda c: c * kc
    heads = range(B_HEADS)
    hrows = lambda h: slice(h * qb, (h + 1) * qb)
    groups = [slice(g * SUBLANES, (g + 1) * SUBLANES) for g in range(kc // SUBLANES)]

    def fold(x, op, ways=4):
        parts = [x[g] for g in groups[:ways]]
        for j, g in enumerate(groups[ways:]):
            parts[j % ways] = op(parts[j % ways], x[g])
        while len(parts) > 1:
            parts = [op(parts[j], parts[j + 1]) for j in range(0, len(parts), 2)]
        return parts[0]

    for h in range(IDX_HEADS):
        ist_ref[hrows(h), :] = iq_ref[:, h * LANES:(h + 1) * LANES]

    def score(c):
        lg = _dot_nt(ik_ref[pl.ds(kstart(c), kc), :], ist_ref[...])
        acc = None
        for h in range(IDX_HEADS):
            t = iwt_ref[h:h + 1, :] * jnp.maximum(lg[:, hrows(h)], 0.0)
            acc = t if acc is None else acc + t
        return acc

    rw = qb // per

    def put_scores(c, sc):
        for r in range(per):
            sc_ref[c, r] = sc[:, r * rw:(r + 1) * rw]
            sc16_ref[c, r] = sc[:, r * rw:(r + 1) * rw].astype(BF16)

    def idx_body(c, carry):
        put_scores(c, score(c))
        return carry

    _unrolled(n_full, idx_body, 0)
    for d in range(per):
        put_scores(n_full + d, jnp.where(d * kc + kidx <= rpos, score(n_full + d), neg_inf))

    kidx_r = lax.broadcasted_iota(jnp.int32, (kc, rw), 0)
    for r in range(per):
        rows = slice(r * rw, (r + 1) * rw)
        tpos = i * qb + r * rw + lax.broadcasted_iota(jnp.int32, (1, rw), 1)
        n_r = n_full + r + 1
        if i * qb + (r + 1) * rw <= int(topk):
            tau_ref[:, rows] = jnp.full((SUBLANES, rw), neg_inf, F32)
            j_ref[:, rows] = jnp.broadcast_to(tpos, (SUBLANES, rw))
            continue

        def count(pred_fn):
            def cbody(c, acc):
                k = sc_ref[c, r]
                return acc + fold(jnp.where(pred_fn(k, c), 1.0, 0.0), jnp.add, ways=1)
            acc = _unrolled(n_r, cbody, jnp.zeros((SUBLANES, rw), F32))
            return jnp.sum(acc, axis=0, keepdims=True)

        def key_to_f32(key_u):
            key = key_u ^ INT_MIN
            return pltpu.bitcast(key ^ ((key >> 31) & 0x7FFFFFFF), F32)

        def count16(cand16):
            def cbody(c, acc):
                k = sc16_ref[c, r]
                hit = jnp.where(k >= cand16, jnp.ones_like(k), jnp.zeros_like(k))
                acc = list(acc)
                for g in range(kc // PACKED_ROWS):
                    acc[g % COARSE_CHAINS] = acc[g % COARSE_CHAINS] + hit[g * PACKED_ROWS:(g + 1) * PACKED_ROWS]
                return tuple(acc)
            zero = jnp.zeros((PACKED_ROWS, rw), BF16)
            acc = _unrolled(n_r, cbody, (zero,) * COARSE_CHAINS)
            total = acc[0].astype(F32)
            for part in acc[1:]:
                total = total + part.astype(F32)
            return jnp.sum(total, axis=0, keepdims=True)

        def coarse_body(step, prefix):
            cand_u = prefix | lax.shift_left(jnp.int32(1), 31 - step)
            key = cand_u ^ INT_MIN
            cand = pltpu.bitcast(key ^ ((key >> 31) & 0x7FFF0000), F32).astype(BF16)
            return jnp.where(count16(cand) >= topk, cand_u, prefix)

        coarse = lax.fori_loop(0, 16, coarse_body, jnp.zeros((1, rw), jnp.int32))
        step16 = 1 << 16
        key_neg_inf = INT_MIN + 0x7FFFFF
        key_pos_inf = 0x7F800000
        key16 = coarse ^ INT_MIN
        key16 = key16 + ((key16 >> 31) & 0xFFFF)
        base = jnp.clip(key16, key_neg_inf + step16, key_pos_inf) - step16

        def fine_body(step, off):
            cand_off = off | lax.shift_left(jnp.int32(1), 16 - step)
            cand = key_to_f32((base + cand_off) ^ INT_MIN)
            cnt = count(lambda k, c: k >= cand)
            return jnp.where(cnt >= topk, cand_off, off)

        off = lax.fori_loop(0, 17, fine_body, jnp.zeros((1, rw), jnp.int32))
        tau = key_to_f32((base + off) ^ INT_MIN)
        tau = jnp.where(tau != tau, neg_inf, tau)
        c_gt = count(lambda k, c: k > tau)
        c_eq = count(lambda k, c: k == tau)
        need = topk - c_gt

        def tie_search(_):
            def jbody(step, q):
                cand = q | lax.shift_left(jnp.int32(1), idx_bits - 1 - step)
                cnt = count(lambda k, c: (k == tau) & (c * kc + kidx_r < cand))
                return jnp.where(cnt < need, cand, q)
            return lax.fori_loop(0, idx_bits, jbody, jnp.zeros((1, rw), jnp.int32))

        any_split = jnp.max(c_eq - need) > 0.0
        jcut = lax.cond(any_split, tie_search, lambda _: jnp.full((1, rw), 2 ** idx_bits, jnp.int32), 0)
        few = tpos < int(topk)
        tau_ref[:, rows] = jnp.broadcast_to(jnp.where(few, neg_inf, tau), (SUBLANES, rw))
        j_ref[:, rows] = jnp.broadcast_to(jnp.where(few, tpos, jcut), (SUBLANES, rw))

    for h in heads:
        qst_ref[hrows(h), :] = bq_ref[:, h * LANES:(h + 1) * LANES]
    m_ref[...] = jnp.full(m_ref.shape, NEG_BIG, F32)

    def max_body(c, carry):
        s = _dot_nt(bk_ref[pl.ds(kstart(c), kc), :], qst_ref[...])
        k = jnp.concatenate([sc_ref[c, r] for r in range(per)], axis=1)
        tau = tau_ref[0:1, :]
        sel = (k > tau) | ((k == tau) & (c * kc + kidx <= j_ref[0:1, :]))
        bias = jnp.where(sel, 0.0, NEG_BIG)
        for h in heads:
            sm = s[:, hrows(h)] + bias
            s_ref[c, :, hrows(h)] = sm
            m_ref[:, hrows(h)] = jnp.maximum(m_ref[:, hrows(h)], fold(sm, jnp.maximum, ways=1))
        return carry

    _unrolled(n_chunks, max_body, 0)
    m_ref[...] = jnp.broadcast_to(jnp.max(m_ref[...], axis=0, keepdims=True), m_ref.shape)

    acc_ref[...] = jnp.zeros(acc_ref.shape, F32)

    def pv_body(c, carry):
        p = jnp.exp2(s_ref[c] - m_ref[0:1, :]).astype(BF16)
        acc_ref[...] += _dot(vt_ref[c], p)
        return carry

    _unrolled(n_chunks, pv_body, 0)
    for h in heads:
        acc = acc_ref[:, hrows(h)]
        o_ref[:, h * 64:(h + 1) * 64] = (acc / acc[64:65, :]).T[:, :64].astype(BF16)


def _dsa(bq, iq, iwt, bk, ik, vt, b, s):
    n = bq.shape[0]
    qb = _att_qb(s)
    nb = s // qb
    nkc = s // ATT_KC
    topk = min(IDX_TOPK_MAX, s // 4)
    idx_bits = max(1, int(math.ceil(math.log2(s))))
    seq = lambda bb: (bb, 0)
    out = jnp.zeros((n, 256), BF16)
    for i in range(nb):
        blk = lambda bb, i=i: (bb * nb + i, 0)
        n_kc = (i + 1) * (qb // ATT_KC)
        out = pl.pallas_call(
            functools.partial(_dsa_kernel, blk=i, topk=float(topk), idx_bits=idx_bits),
            grid=(b,),
            in_specs=[pl.BlockSpec((qb, 512), blk), pl.BlockSpec((qb, 512), blk),
                      pl.BlockSpec((8, qb), lambda bb, i=i: (0, bb * nb + i)),
                      pl.BlockSpec((s, 128), seq), pl.BlockSpec((s, 128), seq),
                      pl.BlockSpec((nkc, LANES, ATT_KC), lambda bb: (bb, 0, 0)),
                      pl.BlockSpec(memory_space=pl.ANY)],
            out_specs=pl.BlockSpec((qb, 256), blk),
            out_shape=jax.ShapeDtypeStruct((n, 256), BF16),
            input_output_aliases={6: 0},
            scratch_shapes=[pltpu.VMEM((n_kc, qb // ATT_KC, ATT_KC, ATT_KC), F32),
                            pltpu.VMEM((n_kc, qb // ATT_KC, ATT_KC, ATT_KC), BF16),
                            pltpu.VMEM((n_kc, ATT_KC, B_HEADS * qb), F32),
                            pltpu.VMEM((SUBLANES, qb), F32), pltpu.VMEM((SUBLANES, qb), jnp.int32),
                            pltpu.VMEM((IDX_HEADS * qb, LANES), BF16), pltpu.VMEM((B_HEADS * qb, LANES), BF16),
                            pltpu.VMEM((SUBLANES, B_HEADS * qb), F32), pltpu.VMEM((LANES, B_HEADS * qb), F32)],
            compiler_params=pltpu.CompilerParams(dimension_semantics=("parallel",), vmem_limit_bytes=VMEM_LIMIT),
        )(bq, iq, iwt, bk, ik, vt, out)
    return out


def _mla_kernel(q_ref, k_ref, v_ref, _aliased_out_ref, o_ref, s_ref, m_ref, acc_ref, *, blk):
    qb = q_ref.shape[0]
    kc = ATT_KC
    per = qb // kc
    n_full = per * blk
    n_chunks = n_full + per
    row = lax.broadcasted_iota(jnp.int32, (qb, kc), 0)
    col = lax.broadcasted_iota(jnp.int32, (qb, kc), 1)
    heads = range(C_HEADS)
    hs = lambda h: slice(h * LANES, (h + 1) * LANES)
    keys = lambda c: slice(c * kc, (c + 1) * kc)
    n_halves = kc // LANES
    m_ref[...] = jnp.full(m_ref.shape, NEG_BIG, F32)

    for c in range(n_chunks):
        diag = c - n_full
        for h in heads:
            s = _dot_nt(q_ref[:, hs(h)], k_ref[keys(c), hs(h)])
            if diag >= 0:
                s = s + jnp.where(diag * kc + col <= row, 0.0, NEG_BIG)
            s_ref[h, c] = s
            m_ref[h] = jnp.maximum(m_ref[h], _fold_max(s))
    for h in heads:
        m_ref[h] = jnp.broadcast_to(jnp.max(m_ref[h], axis=1, keepdims=True), (qb, LANES))
    acc_ref[...] = jnp.zeros(acc_ref.shape, F32)

    for c in range(n_chunks):
        for h in heads:
            p = jnp.exp2(s_ref[h, c] - jnp.concatenate([m_ref[h]] * n_halves, axis=1))
            acc_ref[h] += _dot(p.astype(BF16), v_ref[keys(c), hs(h)])
    for h in heads:
        acc = acc_ref[h]
        o_ref[:, h * 64:(h + 1) * 64] = (acc[:, :64] / acc[:, 64:65]).astype(BF16)


def _mla(qc, kc, vc, b, s):
    n = qc.shape[0]
    qb = _att_qb(s)
    nb = s // qb
    seq = lambda bb: (bb, 0)
    out = jnp.zeros((n, 256), BF16)
    for i in range(nb):
        blk = lambda bb, i=i: (bb * nb + i, 0)
        n_kc = (i + 1) * (qb // ATT_KC)
        out = pl.pallas_call(
            functools.partial(_mla_kernel, blk=i),
            grid=(b,),
            in_specs=[pl.BlockSpec((qb, 512), blk), pl.BlockSpec((s, 512), seq), pl.BlockSpec((s, 512), seq),
                      pl.BlockSpec(memory_space=pl.ANY)],
            out_specs=pl.BlockSpec((qb, 256), blk),
            out_shape=jax.ShapeDtypeStruct((n, 256), BF16),
            input_output_aliases={3: 0},
            scratch_shapes=[pltpu.VMEM((C_HEADS, n_kc, qb, ATT_KC), F32), pltpu.VMEM((C_HEADS, qb, LANES), F32),
                            pltpu.VMEM((C_HEADS, qb, LANES), F32)],
            compiler_params=pltpu.CompilerParams(dimension_semantics=("parallel",), vmem_limit_bytes=VMEM_LIMIT),
        )(qc, kc, vc, out)
    return out


def _layer_norm(z, g, b):
    mu = jnp.mean(z, axis=1, keepdims=True)
    zc = z - mu
    var = jnp.mean(zc * zc, axis=1, keepdims=True)
    return zc * lax.rsqrt(var + LN_EPS) * g + b


def _outproj_kernel(x_ref, oa_ref, ob_ref, oc_ref, wo_ref, g_ref, b_ref, wr_ref, rb_ref, x1_ref, gates_ref,
                    *, alpha):
    y = _dot(oa_ref[...], wo_ref[0:512, :]) + _dot(ob_ref[...], wo_ref[512:768, :]) \
        + _dot(oc_ref[...], wo_ref[768:1024, :])
    x1 = _layer_norm(alpha * x_ref[...] + y, g_ref[...], b_ref[...])
    x1_ref[...] = x1

    x1h = x1.astype(BF16)
    x1l = (x1 - x1h.astype(F32)).astype(BF16)
    wr = wr_ref[...]
    wrh = wr.astype(BF16)
    wrl = (wr - wrh.astype(F32)).astype(BF16)
    hi = _dot_nt(jnp.concatenate([wrh, wrl], axis=0), x1h)
    logits = hi[:N_EXPERTS] + hi[N_EXPERTS:] + _dot_nt(wrh, x1l)
    scores = jax.nn.sigmoid(logits)
    biased = scores + rb_ref[...]
    rows = [biased[e:e + 1, :] for e in range(N_EXPERTS)]
    gscore = []
    for g in range(N_GROUPS):
        v = rows[g * 4:(g + 1) * 4]
        best = None
        for a in range(4):
            for c in range(a + 1, 4):
                pair = v[a] + v[c]
                best = pair if best is None else jnp.maximum(best, pair)
        gscore.append(best)
    gmax = jnp.maximum(jnp.maximum(gscore[0], gscore[1]), jnp.maximum(gscore[2], gscore[3]))
    taken = jnp.zeros_like(gmax) > 1.0
    sel_rows = []
    for g in range(N_GROUPS):
        g_sel = (gscore[g] == gmax) & jnp.logical_not(taken)
        taken = taken | g_sel
        v = rows[g * 4:(g + 1) * 4]
        for e in range(4):
            rank = jnp.zeros_like(gmax)
            for j in range(4):
                if j == e:
                    continue
                ahead = (v[j] > v[e]) | ((v[j] == v[e]) & (j < e))
                rank = rank + jnp.where(ahead, 1.0, 0.0)
            sel_rows.append(g_sel & (rank < 2.0))
    wsel = [jnp.where(sel_rows[e], scores[e:e + 1, :], 0.0) for e in range(N_EXPERTS)]
    total = wsel[0]
    for e in range(1, N_EXPERTS):
        total = total + wsel[e]
    tm = total.shape[1]
    gates_t = jnp.concatenate(wsel + [jnp.zeros((LANES - N_EXPERTS, tm), F32)], axis=0) / total
    gates = gates_t.T
    for g in range(N_GROUPS):
        gates_ref[g] = gates[:, g * EXPERTS_PER_GROUP:(g + 1) * EXPERTS_PER_GROUP]


def _outproj(x2, oa, ob, oc, wo, g, bb, wr_t, rbias, alpha, tm):
    n, d = x2.shape
    nt = n // tm
    full = lambda a: pl.BlockSpec(a.shape, lambda t: (0,) * a.ndim)
    row = lambda c: pl.BlockSpec((tm, c), lambda t: (t, 0))
    return pl.pallas_call(
        functools.partial(_outproj_kernel, alpha=alpha),
        grid=(nt,),
        in_specs=[row(d), row(512), row(256), row(256), full(wo), full(g), full(bb), full(wr_t), full(rbias)],
        out_specs=[row(d), pl.BlockSpec((N_GROUPS, tm, EXPERTS_PER_GROUP), lambda t: (0, t, 0))],
        out_shape=[jax.ShapeDtypeStruct((n, d), F32),
                   jax.ShapeDtypeStruct((N_GROUPS, n, EXPERTS_PER_GROUP), F32)],
        compiler_params=pltpu.CompilerParams(dimension_semantics=("parallel",), vmem_limit_bytes=VMEM_LIMIT),
    )(x2, oa, ob, oc, wo, g, bb, wr_t, rbias)


def _moe_kernel(x_ref, gates_ref, wg_ref, wu_ref, wd_ref, g_ref, b_ref, o_ref, xb_ref, acc_ref, *, alpha):
    grp = pl.program_id(1)

    @pl.when(grp == 0)
    def _():
        xb_ref[...] = x_ref[...].astype(BF16)

    xb = xb_ref[...]
    hs = []
    for j in range(EXPERTS_PER_GROUP):
        hg = _dot(xb, wg_ref[j])
        hu = _dot(xb, wu_ref[j])
        hs.append((hg * jax.nn.sigmoid(hg) * hu * gates_ref[0, :, j:j + 1]).astype(BF16))
    wd = wd_ref[...]
    y = _dot(jnp.concatenate(hs, axis=1), wd.reshape(wd.shape[0] * wd.shape[1], wd.shape[2]))

    @pl.when(grp == 0)
    def _():
        acc_ref[...] = y

    @pl.when(grp > 0)
    def _():
        acc_ref[...] += y

    @pl.when(grp == N_GROUPS - 1)
    def _():
        o_ref[...] = _layer_norm(alpha * x_ref[...] + acc_ref[...], g_ref[...], b_ref[...])


def _moe(x1, gates_g, wg, wu, wd, g, bb, alpha, tm):
    n, d = x1.shape
    nt = n // tm
    f = wg.shape[2]
    epg = EXPERTS_PER_GROUP
    full = lambda a: pl.BlockSpec(a.shape, lambda t, e: (0,) * a.ndim)
    return pl.pallas_call(
        functools.partial(_moe_kernel, alpha=alpha),
        grid=(nt, N_GROUPS),
        in_specs=[pl.BlockSpec((tm, d), lambda t, e: (t, 0)), pl.BlockSpec((1, tm, epg), lambda t, e: (e, t, 0)),
                  pl.BlockSpec((epg, d, f), lambda t, e: (e, 0, 0)), pl.BlockSpec((epg, d, f), lambda t, e: (e, 0, 0)),
                  pl.BlockSpec((epg, f, d), lambda t, e: (e, 0, 0)), full(g), full(bb)],
        out_specs=pl.BlockSpec((tm, d), lambda t, e: (t, 0)),
        out_shape=jax.ShapeDtypeStruct((n, d), F32),
        scratch_shapes=[pltpu.VMEM((tm, d), BF16), pltpu.VMEM((tm, d), F32)],
        compiler_params=pltpu.CompilerParams(dimension_semantics=("parallel", "arbitrary"),
                                             vmem_limit_bytes=VMEM_LIMIT),
    )(x1, gates_g, wg, wu, wd, g, bb)


def kernel(x, w_in, attn_sinks, c_q_norm_g, c_kv_norm_g, w_uq, w_ukv, w_out, ln1_g, ln1_b, w_router, router_bias,
           w_gate, w_up, w_down, ln2_g, ln2_b):
    b, s, d = x.shape
    depth = w_in.shape[0]
    n = b * s
    alpha = (2 * depth) ** 0.25
    tm = min(512, s)
    tables = _rope_tables(s)
    wr_t = w_router.T
    rbias = router_bias.reshape(N_EXPERTS, 1)
    x2 = x.reshape(n, d)
    for l in range(depth):
        w_p, w_t = _pack_w_in(w_in[l])
        wuq_p, wukv_p = _pack_mla_w(w_uq[l], w_ukv[l])
        aq, ak, bq, bk, ik, iq, avt, vt, iwt, qc, kc, vc = _inproj(
            x2, w_p, w_t, tables, c_q_norm_g[l].reshape(1, -1), c_kv_norm_g[l].reshape(1, -1), wuq_p, wukv_p, s, tm)
        oa = _swa(attn_sinks[l], aq, ak, avt, b, s)
        ob = _dsa(bq, iq, iwt, bk, ik, vt, b, s)
        oc = _mla(qc, kc, vc, b, s)
        x1, gates_g = _outproj(x2, oa, ob, oc, w_out[l].astype(BF16), ln1_g[l].reshape(1, d), ln1_b[l].reshape(1, d),
                               wr_t, rbias, alpha, tm)
        x2 = _moe(x1, gates_g, w_gate[l].astype(BF16), w_up[l].astype(BF16), w_down[l].astype(BF16),
                  ln2_g[l].reshape(1, d), ln2_b[l].reshape(1, d), alpha, min(1024, n))
    return x2.reshape(b, s, d)
```

```python
import functools
import math

import jax
import jax.numpy as jnp
import numpy as np
from jax import lax
from jax.experimental import pallas as pl
from jax.experimental.pallas import tpu as pltpu

HEAD_DIM = 64
ROPE_THETA = 10000.0
A_HEADS = 8
A_KV_HEADS = 2
A_WINDOW = 128
B_HEADS = 4
IDX_HEADS = 4
IDX_DIM = 64
IDX_TOPK_MAX = 256
C_HEADS = 4
C_NOPE = 64
C_ROPE = 32
C_V = 64
C_Q_RANK = 256
C_KV_RANK = 128
N_EXPERTS = 16
N_GROUPS = 4
EXPERTS_PER_GROUP = 4
D_EXPERT = 256
LN_EPS = 1e-5
RMS_EPS = 1e-6

LANES = 128
SUBLANES = 8
PACKED_ROWS = 16
NEG_BIG = -1e30
LOG2E = math.log2(math.e)
INT_MIN = -(2 ** 31)
VMEM_LIMIT = 56 * 1024 * 1024

BF16 = jnp.bfloat16
F32 = jnp.float32

_NT = (((1,), (1,)), ((), ()))


def _dot(a, b):
    return jnp.dot(a, b, preferred_element_type=F32)


def _dot_nt(a, b):
    return lax.dot_general(a, b, _NT, preferred_element_type=F32)


N_ROPE_UNITS = 10


def _inproj_kernel(x_ref, w_ref, ch_ref, s1h_ref, s2h_ref, cr_ref, s1r_ref, s2r_ref, gq_ref, gkv_ref,
                   wuq_ref, wukv_ref, wt_ref,
                   aq_ref, ak_ref, bq_ref, bk_ref, ik_ref, iq_ref, avt_ref, vt_ref, iwt_ref, qc_ref, kc_ref, vc_ref,
                   *, b_scale, c_scale):
    xb = x_ref[...].astype(BF16)
    tm = xb.shape[0]
    lane = lax.broadcasted_iota(jnp.int32, (tm, LANES), 1)
    ch, s1h, s2h = ch_ref[...], s1h_ref[...], s2h_ref[...]
    cr, s1r, s2r = cr_ref[...], s1r_ref[...], s2r_ref[...]
    low = lane < 64

    def rope_h(u):
        return u * ch + pltpu.roll(u, 96, 1) * s1h + pltpu.roll(u, 32, 1) * s2h

    def rope_r(u):
        return u * cr + pltpu.roll(u, 112, 1) * s1r + pltpu.roll(u, 16, 1) * s2r

    def split(u):
        return jnp.where(low, u, 0.0), jnp.where(low, pltpu.roll(u, 64, 1), 0.0)

    def put(ref, k, u):
        ref[:, k * LANES:(k + 1) * LANES] = u.astype(BF16)

    projected = [_dot(xb, w_ref[:, g * 256:(g + 1) * 256]) for g in range(N_ROPE_UNITS // 2)]
    for g, hg in enumerate(projected):
        for half in range(2):
            unit = 2 * g + half
            u = rope_h(hg[:, half * LANES:(half + 1) * LANES])
            if unit < 4:
                put(aq_ref, unit, u * LOG2E)
            elif unit == 4:
                swapped = pltpu.roll(u, 64, 1)
                put(ak_ref, 0, jnp.where(low, u, 0.0))
                put(ak_ref, 1, jnp.where(low, 0.0, swapped))
                put(ak_ref, 2, jnp.where(low, swapped, 0.0))
                put(ak_ref, 3, jnp.where(low, 0.0, u))
            elif unit < 7:
                h0, h1 = split(u * b_scale)
                put(bq_ref, 2 * (unit - 5), h0)
                put(bq_ref, 2 * (unit - 5) + 1, h1)
            elif unit == 7:
                h0, h1 = split(u)
                put(bk_ref, 0, h0)
                put(ik_ref, 0, h1)
            else:
                h0, h1 = split(u)
                put(iq_ref, 2 * (unit - 8), h0)
                put(iq_ref, 2 * (unit - 8) + 1, h1)

    hb = _dot_nt(wt_ref[...], xb)
    sub = lax.broadcasted_iota(jnp.int32, (LANES, tm), 0)
    with_ones = lambda piece: jnp.where(sub == 64, 1.0, hb[piece * LANES:(piece + 1) * LANES]).astype(BF16)
    vt = with_ones(0)
    for j in range(tm // ATT_KC):
        vt_ref[j] = vt[:, j * ATT_KC:(j + 1) * ATT_KC]
    iwt_ref[...] = hb[LANES:LANES + 8]
    for g in range(A_KV_HEADS):
        avt_ref[g] = with_ones(2 + g)

    cq = _dot(xb, w_ref[:, 1280:1536])
    cqn = cq * lax.rsqrt(jnp.mean(cq * cq, axis=1, keepdims=True) + RMS_EPS) * gq_ref[...]
    qc = _dot(cqn.astype(BF16), wuq_ref[...])
    for h in range(C_HEADS):
        u = qc[:, h * LANES:(h + 1) * LANES]
        qc_ref[:, h * LANES:(h + 1) * LANES] = (rope_r(u) * c_scale).astype(BF16)

    ckv = _dot(xb, w_ref[:, 1536:1792])
    ckv_lat = ckv[:, :LANES]
    kr = rope_r(ckv[:, LANES:])
    ckvn = ckv_lat * lax.rsqrt(jnp.mean(ckv_lat * ckv_lat, axis=1, keepdims=True) + RMS_EPS) * gkv_ref[...]
    kv = _dot(ckvn.astype(BF16), wukv_ref[...])
    for h in range(C_HEADS):
        kc_ref[:, h * LANES:(h + 1) * LANES] = (kv[:, h * LANES:(h + 1) * LANES] + kr).astype(BF16)
        v = kv[:, (C_HEADS + h) * LANES:(C_HEADS + h + 1) * LANES]
        vc_ref[:, h * LANES:(h + 1) * LANES] = jnp.where(lane == 64, 1.0, v).astype(BF16)


def _pack_w_in(w):
    d = w.shape[0]
    z = lambda n: jnp.zeros((d, n), F32)
    o = np.cumsum([0, 512, 128, 128, 256, 64, 64, 256, 64, 4, 256, 128, 32]).tolist()
    a_q, a_k, a_v, b_q, b_k, b_v, i_q, i_k, i_w, c_q, c_kv, c_kr = [w[:, o[j]:o[j + 1]] for j in range(12)]
    qs = HEAD_DIM ** -0.5
    ws = (IDX_HEADS * IDX_DIM) ** -0.5
    cols = [a_q * qs, a_k, b_q, b_k, i_k, i_q,
            c_q, c_kv, z(64), c_kr, z(32)]
    rows_t = jnp.concatenate([b_v, z(64), i_w * ws, z(LANES - IDX_HEADS),
                              a_v[:, :64], z(64), a_v[:, 64:], z(64)], axis=1).T
    return jnp.concatenate(cols, axis=1).astype(BF16), rows_t.astype(BF16)


def _pack_mla_w(w_uq, w_ukv):
    r = w_uq.shape[0]
    q = w_uq.reshape(r, C_HEADS, C_NOPE + C_ROPE)
    q = jnp.pad(q, ((0, 0), (0, 0), (0, LANES - C_NOPE - C_ROPE))).reshape(r, C_HEADS * LANES)
    r2 = w_ukv.shape[0]
    kv = w_ukv.reshape(r2, C_HEADS, C_NOPE + C_V)
    k = jnp.pad(kv[:, :, :C_NOPE], ((0, 0), (0, 0), (0, LANES - C_NOPE))).reshape(r2, C_HEADS * LANES)
    v = jnp.pad(kv[:, :, C_NOPE:], ((0, 0), (0, 0), (0, LANES - C_V))).reshape(r2, C_HEADS * LANES)
    return q.astype(BF16), jnp.concatenate([k, v], axis=1).astype(BF16)


def _rope_tables(s):
    pos = jnp.arange(s, dtype=F32)[:, None]
    lane = np.arange(LANES)
    inv_h = 1.0 / (ROPE_THETA ** (jnp.arange(0, HEAD_DIM, 2, dtype=F32) / HEAD_DIM))
    ang = pos * inv_h[None, :]
    cos, sin = jnp.cos(ang), jnp.sin(ang)
    j = lane % 32
    lo = jnp.asarray((lane % 64) < 32)
    ch = cos[:, j]
    s1h = jnp.where(lo, -sin[:, j], 0.0)
    s2h = jnp.where(lo, 0.0, sin[:, j])
    inv_r = 1.0 / (ROPE_THETA ** (jnp.arange(0, C_ROPE, 2, dtype=F32) / C_ROPE))
    angr = pos * inv_r[None, :]
    cosr, sinr = jnp.cos(angr), jnp.sin(angr)
    jr = lane % 16
    in_rope = jnp.asarray((lane >= 64) & (lane < 96))
    first = jnp.asarray((lane >= 64) & (lane < 80))
    second = jnp.asarray((lane >= 80) & (lane < 96))
    cr = jnp.where(in_rope, cosr[:, jr], 1.0)
    s1r = jnp.where(first, -sinr[:, jr], 0.0)
    s2r = jnp.where(second, sinr[:, jr], 0.0)
    return ch, s1h, s2h, cr, s1r, s2r


def _inproj(x2, w_p, w_t, tables, gq, gkv, wuq_p, wukv_p, s, tm):
    n, d = x2.shape
    nt = n // tm
    spt = s // tm
    cpt = tm // ATT_KC
    tab_spec = pl.BlockSpec((tm, LANES), lambda t: (t % spt, 0))
    full = lambda a: pl.BlockSpec(a.shape, lambda t: (0,) * a.ndim)
    row = lambda c: pl.BlockSpec((tm, c), lambda t: (t, 0))
    cols_a, cols_c = [512, 512, 512, 128, 128, 512], [512, 512, 512]
    out_specs = ([row(c) for c in cols_a]
                 + [pl.BlockSpec((A_KV_HEADS, LANES, tm), lambda t: (0, 0, t)),
                    pl.BlockSpec((cpt, LANES, ATT_KC), lambda t: (t, 0, 0)), pl.BlockSpec((8, tm), lambda t: (0, t))]
                 + [row(c) for c in cols_c])
    out_shape = ([jax.ShapeDtypeStruct((n, c), BF16) for c in cols_a]
                 + [jax.ShapeDtypeStruct((A_KV_HEADS, LANES, n), BF16),
                    jax.ShapeDtypeStruct((n // ATT_KC, LANES, ATT_KC), BF16), jax.ShapeDtypeStruct((8, n), F32)]
                 + [jax.ShapeDtypeStruct((n, c), BF16) for c in cols_c])
    return pl.pallas_call(
        functools.partial(_inproj_kernel, b_scale=HEAD_DIM ** -0.5 * LOG2E,
                          c_scale=(C_NOPE + C_ROPE) ** -0.5 * LOG2E),
        grid=(nt,),
        in_specs=[row(d), full(w_p)] + [tab_spec] * 6 + [full(gq), full(gkv), full(wuq_p), full(wukv_p), full(w_t)],
        out_specs=out_specs,
        out_shape=out_shape,
        compiler_params=pltpu.CompilerParams(dimension_semantics=("parallel",), vmem_limit_bytes=VMEM_LIMIT),
    )(x2, w_p, *tables, gq, gkv, wuq_p, wukv_p, w_t)


def _swa_kernel(sink_ref, q_ref, kc_ref, kp_ref, vc_ref, vp_ref, o_ref):
    i = pl.program_id(1)
    w = A_WINDOW
    qb = q_ref.shape[0]
    kj = lax.broadcasted_iota(jnp.int32, (2 * w, w), 0)
    qi = lax.broadcasted_iota(jnp.int32, (2 * w, w), 1)
    diff = qi + w - kj
    in_window = (diff >= 0) & (diff < w)
    groups = [slice(g * SUBLANES, (g + 1) * SUBLANES) for g in range(2 * w // SUBLANES)]
    lane_head = lax.broadcasted_iota(jnp.int32, (1, 4 * w), 1) // w
    per_group = A_HEADS // A_KV_HEADS
    tasks = [(win, g) for win in range(qb // w) for g in range(A_KV_HEADS)]
    head_order = lambda g: [per_group * g, per_group * g + 2, per_group * g + 1, per_group * g + 3]

    def keys(win, unit):
        cols = slice(unit * LANES, (unit + 1) * LANES)
        if win == 0:
            return jnp.concatenate([kp_ref[:, cols], kc_ref[0:w, cols]], axis=0)
        return kc_ref[(win - 1) * w:(win + 1) * w, cols]

    def values_t(win, g):
        if win == 0:
            return jnp.concatenate([vp_ref[g], vc_ref[g, :, 0:w]], axis=1)
        return vc_ref[g, :, (win - 1) * w:(win + 1) * w]

    scores = []
    for win, g in tasks:
        rows = slice(win * w, (win + 1) * w)
        ok = in_window if win > 0 else in_window & ((kj >= w) | (i > 0))
        bias = jnp.concatenate([jnp.where(ok, 0.0, NEG_BIG)] * per_group, axis=1)
        q2 = jnp.concatenate([q_ref[rows, (2 * g + u) * LANES:(2 * g + u + 1) * LANES] for u in range(2)], axis=0)
        scores.append(jnp.concatenate([_dot_nt(keys(win, 2 * g), q2), _dot_nt(keys(win, 2 * g + 1), q2)], axis=1)
                      + bias)
    probs = []
    for (win, g), s in zip(tasks, scores):
        sink = jnp.zeros((1, per_group * w), F32)
        for slot, h in enumerate(head_order(g)):
            sink = jnp.where(lane_head == slot, sink_ref[h] * LOG2E, sink)
        parts = [s[gg] for gg in groups]
        while len(parts) > 1:
            parts = [jnp.maximum(parts[j], parts[j + 1]) for j in range(0, len(parts), 2)]
        m = jnp.maximum(jnp.max(parts[0], axis=0, keepdims=True), sink)
        probs.append((jnp.exp2(s - m).astype(BF16), jnp.exp2(sink - m)))
    outs = [_dot(values_t(win, g), p) for (win, g), (p, _) in zip(tasks, probs)]
    for (win, g), o, (_, sink_p) in zip(tasks, outs, probs):
        rows = slice(win * w, (win + 1) * w)
        o = (o / (o[64:65, :] + sink_p)).T
        for slot, h in enumerate(head_order(g)):
            o_ref[rows, h * 64:(h + 1) * 64] = o[slot * w:(slot + 1) * w, :64].astype(BF16)


def _swa(sinks, aq, ak, avt, b, s):
    n = aq.shape[0]
    w = A_WINDOW
    qb = _att_qb(s)
    nb = s // qb
    wpb = qb // w
    cur = lambda bb, i: (bb * nb + i, 0)
    prev = lambda bb, i: ((bb * nb + i) * wpb - jnp.minimum(i, 1), 0)
    cur_t = lambda bb, i: (0, 0, bb * nb + i)
    prev_t = lambda bb, i: (0, 0, (bb * nb + i) * wpb - jnp.minimum(i, 1))
    return pl.pallas_call(
        _swa_kernel,
        grid=(b, nb),
        in_specs=[pl.BlockSpec(memory_space=pltpu.SMEM),
                  pl.BlockSpec((qb, 512), cur),
                  pl.BlockSpec((qb, 512), cur), pl.BlockSpec((w, 512), prev),
                  pl.BlockSpec((A_KV_HEADS, LANES, qb), cur_t), pl.BlockSpec((A_KV_HEADS, LANES, w), prev_t)],
        out_specs=pl.BlockSpec((qb, 512), cur),
        out_shape=jax.ShapeDtypeStruct((n, 512), BF16),
        compiler_params=pltpu.CompilerParams(dimension_semantics=("parallel", "parallel"),
                                             vmem_limit_bytes=VMEM_LIMIT),
    )(sinks, aq, ak, ak, avt, avt)


ATT_KC = 256
COARSE_CHAINS = 2


def _att_qb(s):
    return min(512, s)


def _fold_max(s):
    return jnp.maximum(s[:, :LANES], s[:, LANES:])


def _unrolled(n, body, carry):
    for c in range(n):
        carry = body(c, carry)
    return carry


def _dsa_kernel(bq_ref, iq_ref, iwt_ref, bk_ref, ik_ref, vt_ref, _aliased_out_ref, o_ref,
                sc_ref, sc16_ref, s_ref, tau_ref, j_ref, ist_ref, qst_ref, m_ref, *, blk, topk, idx_bits):
    i = blk
    qb = bq_ref.shape[0]
    kc = ATT_KC
    per = qb // kc
    n_full = per * i
    n_chunks = n_full + per
    kidx = lax.broadcasted_iota(jnp.int32, (kc, qb), 0)
    rpos = lax.broadcasted_iota(jnp.int32, (kc, qb), 1)
    neg_inf = float("-inf")
    kstart = lambda c: c * kc
    heads = range(B_HEADS)
    hrows = lambda h: slice(h * qb, (h + 1) * qb)
    groups = [slice(g * SUBLANES, (g + 1) * SUBLANES) for g in range(kc // SUBLANES)]

    def fold(x, op, ways=4):
        parts = [x[g] for g in groups[:ways]]
        for j, g in enumerate(groups[ways:]):
            parts[j % ways] = op(parts[j % ways], x[g])
        while len(parts) > 1:
            parts = [op(parts[j], parts[j + 1]) for j in range(0, len(parts), 2)]
        return parts[0]

    for h in range(IDX_HEADS):
        ist_ref[hrows(h), :] = iq_ref[:, h * LANES:(h + 1) * LANES]

    def score(c):
        lg = _dot_nt(ik_ref[pl.ds(kstart(c), kc), :], ist_ref[...])
        acc = None
        for h in range(IDX_HEADS):
            t = iwt_ref[h:h + 1, :] * jnp.maximum(lg[:, hrows(h)], 0.0)
            acc = t if acc is None else acc + t
        return acc

    rw = qb // per

    def put_scores(c, sc):
        for r in range(per):
            sc_ref[c, r] = sc[:, r * rw:(r + 1) * rw]
            sc16_ref[c, r] = sc[:, r * rw:(r + 1) * rw].astype(BF16)

    def idx_body(c, carry):
        put_scores(c, score(c))
        return carry

    _unrolled(n_full, idx_body, 0)
    for d in range(per):
        put_scores(n_full + d, jnp.where(d * kc + kidx <= rpos, score(n_full + d), neg_inf))

    kidx_r = lax.broadcasted_iota(jnp.int32, (kc, rw), 0)
    for r in range(per):
        rows = slice(r * rw, (r + 1) * rw)
        tpos = i * qb + r * rw + lax.broadcasted_iota(jnp.int32, (1, rw), 1)
        n_r = n_full + r + 1
        if i * qb + (r + 1) * rw <= int(topk):
            tau_ref[:, rows] = jnp.full((SUBLANES, rw), neg_inf, F32)
            j_ref[:, rows] = jnp.broadcast_to(tpos, (SUBLANES, rw))
            continue

        def count(pred_fn):
            def cbody(c, acc):
                k = sc_ref[c, r]
                return acc + fold(jnp.where(pred_fn(k, c), 1.0, 0.0), jnp.add, ways=1)
            acc = _unrolled(n_r, cbody, jnp.zeros((SUBLANES, rw), F32))
            return jnp.sum(acc, axis=0, keepdims=True)

        def key_to_f32(key_u):
            key = key_u ^ INT_MIN
            return pltpu.bitcast(key ^ ((key >> 31) & 0x7FFFFFFF), F32)

        def count16(cand16):
            def cbody(c, acc):
                k = sc16_ref[c, r]
                hit = jnp.where(k >= cand16, jnp.ones_like(k), jnp.zeros_like(k))
                acc = list(acc)
                for g in range(kc // PACKED_ROWS):
                    acc[g % COARSE_CHAINS] = acc[g % COARSE_CHAINS] + hit[g * PACKED_ROWS:(g + 1) * PACKED_ROWS]
                return tuple(acc)
            zero = jnp.zeros((PACKED_ROWS, rw), BF16)
            acc = _unrolled(n_r, cbody, (zero,) * COARSE_CHAINS)
            total = acc[0].astype(F32)
            for part in acc[1:]:
                total = total + part.astype(F32)
            return jnp.sum(total, axis=0, keepdims=True)

        def coarse_body(step, prefix):
            cand_u = prefix | lax.shift_left(jnp.int32(1), 31 - step)
            key = cand_u ^ INT_MIN
            cand = pltpu.bitcast(key ^ ((key >> 31) & 0x7FFF0000), F32).astype(BF16)
            return jnp.where(count16(cand) >= topk, cand_u, prefix)

        coarse = lax.fori_loop(0, 16, coarse_body, jnp.zeros((1, rw), jnp.int32))
        step16 = 1 << 16
        key_neg_inf = INT_MIN + 0x7FFFFF
        key_pos_inf = 0x7F800000
        key16 = coarse ^ INT_MIN
        key16 = key16 + ((key16 >> 31) & 0xFFFF)
        base = jnp.clip(key16, key_neg_inf + step16, key_pos_inf) - step16

        def fine_body(step, off):
            cand_off = off | lax.shift_left(jnp.int32(1), 16 - step)
            cand = key_to_f32((base + cand_off) ^ INT_MIN)
            cnt = count(lambda k, c: k >= cand)
            return jnp.where(cnt >= topk, cand_off, off)

        off = lax.fori_loop(0, 17, fine_body, jnp.zeros((1, rw), jnp.int32))
        tau = key_to_f32((base + off) ^ INT_MIN)
        tau = jnp.where(tau != tau, neg_inf, tau)
        c_gt = count(lambda k, c: k > tau)
        c_eq = count(lambda k, c: k == tau)
        need = topk - c_gt

        def tie_search(_):
            def jbody(step, q):
                cand = q | lax.shift_left(jnp.int32(1), idx_bits - 1 - step)
                cnt = count(lambda k, c: (k == tau) & (c * kc + kidx_r < cand))
                return jnp.where(cnt < need, cand, q)
            return lax.fori_loop(0, idx_bits, jbody, jnp.zeros((1, rw), jnp.int32))

        any_split = jnp.max(c_eq - need) > 0.0
        jcut = lax.cond(any_split, tie_search, lambda _: jnp.full((1, rw), 2 ** idx_bits, jnp.int32), 0)
        few = tpos < int(topk)
        tau_ref[:, rows] = jnp.broadcast_to(jnp.where(few, neg_inf, tau), (SUBLANES, rw))
        j_ref[:, rows] = jnp.broadcast_to(jnp.where(few, tpos, jcut), (SUBLANES, rw))

    for h in heads:
        qst_ref[hrows(h), :] = bq_ref[:, h * LANES:(h + 1) * LANES]
    m_ref[...] = jnp.full(m_ref.shape, NEG_BIG, F32)

    def max_body(c, carry):
        s = _dot_nt(bk_ref[pl.ds(kstart(c), kc), :], qst_ref[...])
        k = jnp.concatenate([sc_ref[c, r] for r in range(per)], axis=1)
        tau = tau_ref[0:1, :]
        sel = (k > tau) | ((k == tau) & (c * kc + kidx <= j_ref[0:1, :]))
        bias = jnp.where(sel, 0.0, NEG_BIG)
        for h in heads:
            sm = s[:, hrows(h)] + bias
            s_ref[c, :, hrows(h)] = sm
            m_ref[:, hrows(h)] = jnp.maximum(m_ref[:, hrows(h)], fold(sm, jnp.maximum, ways=1))
        return carry

    _unrolled(n_chunks, max_body, 0)
    m_ref[...] = jnp.broadcast_to(jnp.max(m_ref[...], axis=0, keepdims=True), m_ref.shape)

    probs = [jnp.exp2(s_ref[c] - m_ref[0:1, :]).astype(BF16) for c in range(n_chunks)]
    out_t = _dot(jnp.concatenate([vt_ref[c] for c in range(n_chunks)], axis=1), jnp.concatenate(probs, axis=0))
    for h in heads:
        acc = out_t[:, hrows(h)]
        o_ref[:, h * 64:(h + 1) * 64] = (acc / acc[64:65, :]).T[:, :64].astype(BF16)


def _dsa(bq, iq, iwt, bk, ik, vt, b, s):
    n = bq.shape[0]
    qb = _att_qb(s)
    nb = s // qb
    nkc = s // ATT_KC
    topk = min(IDX_TOPK_MAX, s // 4)
    idx_bits = max(1, int(math.ceil(math.log2(s))))
    seq = lambda bb: (bb, 0)
    out = jnp.zeros((n, 256), BF16)
    for i in range(nb):
        blk = lambda bb, i=i: (bb * nb + i, 0)
        n_kc = (i + 1) * (qb // ATT_KC)
        out = pl.pallas_call(
            functools.partial(_dsa_kernel, blk=i, topk=float(topk), idx_bits=idx_bits),
            grid=(b,),
            in_specs=[pl.BlockSpec((qb, 512), blk), pl.BlockSpec((qb, 512), blk),
                      pl.BlockSpec((8, qb), lambda bb, i=i: (0, bb * nb + i)),
                      pl.BlockSpec((s, 128), seq), pl.BlockSpec((s, 128), seq),
                      pl.BlockSpec((nkc, LANES, ATT_KC), lambda bb: (bb, 0, 0)),
                      pl.BlockSpec(memory_space=pl.ANY)],
            out_specs=pl.BlockSpec((qb, 256), blk),
            out_shape=jax.ShapeDtypeStruct((n, 256), BF16),
            input_output_aliases={6: 0},
            scratch_shapes=[pltpu.VMEM((n_kc, qb // ATT_KC, ATT_KC, ATT_KC), F32),
                            pltpu.VMEM((n_kc, qb // ATT_KC, ATT_KC, ATT_KC), BF16),
                            pltpu.VMEM((n_kc, ATT_KC, B_HEADS * qb), F32),
                            pltpu.VMEM((SUBLANES, qb), F32), pltpu.VMEM((SUBLANES, qb), jnp.int32),
                            pltpu.VMEM((IDX_HEADS * qb, LANES), BF16), pltpu.VMEM((B_HEADS * qb, LANES), BF16),
                            pltpu.VMEM((SUBLANES, B_HEADS * qb), F32)],
            compiler_params=pltpu.CompilerParams(dimension_semantics=("parallel",), vmem_limit_bytes=VMEM_LIMIT),
        )(bq, iq, iwt, bk, ik, vt, out)
    return out


def _mla_kernel(q_ref, k_ref, v_ref, _aliased_out_ref, o_ref, s_ref, m_ref, *, blk):
    qb = q_ref.shape[0]
    kc = ATT_KC
    per = qb // kc
    n_full = per * blk
    n_chunks = n_full + per
    row = lax.broadcasted_iota(jnp.int32, (qb, kc), 0)
    col = lax.broadcasted_iota(jnp.int32, (qb, kc), 1)
    heads = range(C_HEADS)
    hs = lambda h: slice(h * LANES, (h + 1) * LANES)
    keys = lambda c: slice(c * kc, (c + 1) * kc)
    n_halves = kc // LANES
    m_ref[...] = jnp.full(m_ref.shape, NEG_BIG, F32)

    for c in range(n_chunks):
        diag = c - n_full
        for h in heads:
            s = _dot_nt(q_ref[:, hs(h)], k_ref[keys(c), hs(h)])
            if diag >= 0:
                s = s + jnp.where(diag * kc + col <= row, 0.0, NEG_BIG)
            s_ref[h, c] = s
            m_ref[h] = jnp.maximum(m_ref[h], _fold_max(s))
    for h in heads:
        m_ref[h] = jnp.broadcast_to(jnp.max(m_ref[h], axis=1, keepdims=True), (qb, LANES))
    for h in heads:
        m = jnp.concatenate([m_ref[h]] * n_halves, axis=1)
        p = jnp.concatenate([jnp.exp2(s_ref[h, c] - m).astype(BF16) for c in range(n_chunks)], axis=1)
        acc = _dot(p, v_ref[0:n_chunks * kc, hs(h)])
        o_ref[:, h * 64:(h + 1) * 64] = (acc[:, :64] / acc[:, 64:65]).astype(BF16)


def _mla(qc, kc, vc, b, s):
    n = qc.shape[0]
    qb = _att_qb(s)
    nb = s // qb
    seq = lambda bb: (bb, 0)
    out = jnp.zeros((n, 256), BF16)
    for i in range(nb):
        blk = lambda bb, i=i: (bb * nb + i, 0)
        n_kc = (i + 1) * (qb // ATT_KC)
        out = pl.pallas_call(
            functools.partial(_mla_kernel, blk=i),
            grid=(b,),
            in_specs=[pl.BlockSpec((qb, 512), blk), pl.BlockSpec((s, 512), seq), pl.BlockSpec((s, 512), seq),
                      pl.BlockSpec(memory_space=pl.ANY)],
            out_specs=pl.BlockSpec((qb, 256), blk),
            out_shape=jax.ShapeDtypeStruct((n, 256), BF16),
            input_output_aliases={3: 0},
            scratch_shapes=[pltpu.VMEM((C_HEADS, n_kc, qb, ATT_KC), F32), pltpu.VMEM((C_HEADS, qb, LANES), F32)],
            compiler_params=pltpu.CompilerParams(dimension_semantics=("parallel",), vmem_limit_bytes=VMEM_LIMIT),
        )(qc, kc, vc, out)
    return out


def _layer_norm(z, g, b):
    mu = jnp.mean(z, axis=1, keepdims=True)
    zc = z - mu
    var = jnp.mean(zc * zc, axis=1, keepdims=True)
    return zc * lax.rsqrt(var + LN_EPS) * g + b


def _outproj_kernel(x_ref, oa_ref, ob_ref, oc_ref, wo_ref, g_ref, b_ref, wr_ref, rb_ref, x1_ref, gates_ref,
                    *, alpha):
    y = _dot(oa_ref[...], wo_ref[0:512, :]) + _dot(ob_ref[...], wo_ref[512:768, :]) \
        + _dot(oc_ref[...], wo_ref[768:1024, :])
    x1 = _layer_norm(alpha * x_ref[...] + y, g_ref[...], b_ref[...])
    x1_ref[...] = x1

    x1h = x1.astype(BF16)
    x1l = (x1 - x1h.astype(F32)).astype(BF16)
    wr = wr_ref[...]
    wrh = wr.astype(BF16)
    wrl = (wr - wrh.astype(F32)).astype(BF16)
    hi = _dot_nt(jnp.concatenate([wrh, wrl], axis=0), x1h)
    logits = hi[:N_EXPERTS] + hi[N_EXPERTS:] + _dot_nt(wrh, x1l)
    scores = jax.nn.sigmoid(logits)
    biased = scores + rb_ref[...]
    rows = [biased[e:e + 1, :] for e in range(N_EXPERTS)]
    gscore = []
    for g in range(N_GROUPS):
        v = rows[g * 4:(g + 1) * 4]
        best = None
        for a in range(4):
            for c in range(a + 1, 4):
                pair = v[a] + v[c]
                best = pair if best is None else jnp.maximum(best, pair)
        gscore.append(best)
    gmax = jnp.maximum(jnp.maximum(gscore[0], gscore[1]), jnp.maximum(gscore[2], gscore[3]))
    taken = jnp.zeros_like(gmax) > 1.0
    sel_rows = []
    for g in range(N_GROUPS):
        g_sel = (gscore[g] == gmax) & jnp.logical_not(taken)
        taken = taken | g_sel
        v = rows[g * 4:(g + 1) * 4]
        for e in range(4):
            rank = jnp.zeros_like(gmax)
            for j in range(4):
                if j == e:
                    continue
                ahead = (v[j] > v[e]) | ((v[j] == v[e]) & (j < e))
                rank = rank + jnp.where(ahead, 1.0, 0.0)
            sel_rows.append(g_sel & (rank < 2.0))
    wsel = [jnp.where(sel_rows[e], scores[e:e + 1, :], 0.0) for e in range(N_EXPERTS)]
    total = wsel[0]
    for e in range(1, N_EXPERTS):
        total = total + wsel[e]
    tm = total.shape[1]
    gates_t = jnp.concatenate(wsel + [jnp.zeros((LANES - N_EXPERTS, tm), F32)], axis=0) / total
    gates = gates_t.T
    for g in range(N_GROUPS):
        gates_ref[g] = gates[:, g * EXPERTS_PER_GROUP:(g + 1) * EXPERTS_PER_GROUP]


def _outproj(x2, oa, ob, oc, wo, g, bb, wr_t, rbias, alpha, tm):
    n, d = x2.shape
    nt = n // tm
    full = lambda a: pl.BlockSpec(a.shape, lambda t: (0,) * a.ndim)
    row = lambda c: pl.BlockSpec((tm, c), lambda t: (t, 0))
    return pl.pallas_call(
        functools.partial(_outproj_kernel, alpha=alpha),
        grid=(nt,),
        in_specs=[row(d), row(512), row(256), row(256), full(wo), full(g), full(bb), full(wr_t), full(rbias)],
        out_specs=[row(d), pl.BlockSpec((N_GROUPS, tm, EXPERTS_PER_GROUP), lambda t: (0, t, 0))],
        out_shape=[jax.ShapeDtypeStruct((n, d), F32),
                   jax.ShapeDtypeStruct((N_GROUPS, n, EXPERTS_PER_GROUP), F32)],
        compiler_params=pltpu.CompilerParams(dimension_semantics=("parallel",), vmem_limit_bytes=VMEM_LIMIT),
    )(x2, oa, ob, oc, wo, g, bb, wr_t, rbias)


def _moe_kernel(x_ref, gates_ref, wg_ref, wu_ref, wd_ref, g_ref, b_ref, o_ref, xb_ref, acc_ref, *, alpha):
    grp = pl.program_id(1)

    @pl.when(grp == 0)
    def _():
        xb_ref[...] = x_ref[...].astype(BF16)

    xb = xb_ref[...]
    hs = []
    for j in range(EXPERTS_PER_GROUP):
        hg = _dot(xb, wg_ref[j])
        hu = _dot(xb, wu_ref[j])
        hs.append((hg * jax.nn.sigmoid(hg) * hu * gates_ref[0, :, j:j + 1]).astype(BF16))
    wd = wd_ref[...]
    y = _dot(jnp.concatenate(hs, axis=1), wd.reshape(wd.shape[0] * wd.shape[1], wd.shape[2]))

    @pl.when(grp == 0)
    def _():
        acc_ref[...] = y

    @pl.when(grp > 0)
    def _():
        acc_ref[...] += y

    @pl.when(grp == N_GROUPS - 1)
    def _():
        o_ref[...] = _layer_norm(alpha * x_ref[...] + acc_ref[...], g_ref[...], b_ref[...])


def _moe(x1, gates_g, wg, wu, wd, g, bb, alpha, tm):
    n, d = x1.shape
    nt = n // tm
    f = wg.shape[2]
    epg = EXPERTS_PER_GROUP
    full = lambda a: pl.BlockSpec(a.shape, lambda t, e: (0,) * a.ndim)
    return pl.pallas_call(
        functools.partial(_moe_kernel, alpha=alpha),
        grid=(nt, N_GROUPS),
        in_specs=[pl.BlockSpec((tm, d), lambda t, e: (t, 0)), pl.BlockSpec((1, tm, epg), lambda t, e: (e, t, 0)),
                  pl.BlockSpec((epg, d, f), lambda t, e: (e, 0, 0)), pl.BlockSpec((epg, d, f), lambda t, e: (e, 0, 0)),
                  pl.BlockSpec((epg, f, d), lambda t, e: (e, 0, 0)), full(g), full(bb)],
        out_specs=pl.BlockSpec((tm, d), lambda t, e: (t, 0)),
        out_shape=jax.ShapeDtypeStruct((n, d), F32),
        scratch_shapes=[pltpu.VMEM((tm, d), BF16), pltpu.VMEM((tm, d), F32)],
        compiler_params=pltpu.CompilerParams(dimension_semantics=("parallel", "arbitrary"),
                                             vmem_limit_bytes=VMEM_LIMIT),
    )(x1, gates_g, wg, wu, wd, g, bb)


def kernel(x, w_in, attn_sinks, c_q_norm_g, c_kv_norm_g, w_uq, w_ukv, w_out, ln1_g, ln1_b, w_router, router_bias,
           w_gate, w_up, w_down, ln2_g, ln2_b):
    b, s, d = x.shape
    depth = w_in.shape[0]
    n = b * s
    alpha = (2 * depth) ** 0.25
    tm = min(512, s)
    tables = _rope_tables(s)
    wr_t = w_router.T
    rbias = router_bias.reshape(N_EXPERTS, 1)
    x2 = x.reshape(n, d)
    for l in range(depth):
        w_p, w_t = _pack_w_in(w_in[l])
        wuq_p, wukv_p = _pack_mla_w(w_uq[l], w_ukv[l])
        aq, ak, bq, bk, ik, iq, avt, vt, iwt, qc, kc, vc = _inproj(
            x2, w_p, w_t, tables, c_q_norm_g[l].reshape(1, -1), c_kv_norm_g[l].reshape(1, -1), wuq_p, wukv_p, s, tm)
        oa = _swa(attn_sinks[l], aq, ak, avt, b, s)
        ob = _dsa(bq, iq, iwt, bk, ik, vt, b, s)
        oc = _mla(qc, kc, vc, b, s)
        x1, gates_g = _outproj(x2, oa, ob, oc, w_out[l].astype(BF16), ln1_g[l].reshape(1, d), ln1_b[l].reshape(1, d),
                               wr_t, rbias, alpha, tm)
        x2 = _moe(x1, gates_g, w_gate[l].astype(BF16), w_up[l].astype(BF16), w_down[l].astype(BF16),
                  ln2_g[l].reshape(1, d), ln2_b[l].reshape(1, d), alpha, min(1024, n))
    return x2.reshape(b, s, d)
```

```python
import functools
import math

import jax
import jax.numpy as jnp
import numpy as np
from jax import lax
from jax.experimental import pallas as pl
from jax.experimental.pallas import tpu as pltpu

HEAD_DIM = 64
ROPE_THETA = 10000.0
A_HEADS = 8
A_KV_HEADS = 2
A_WINDOW = 128
B_HEADS = 4
IDX_HEADS = 4
IDX_DIM = 64
IDX_TOPK_MAX = 256
C_HEADS = 4
C_NOPE = 64
C_ROPE = 32
C_V = 64
N_EXPERTS = 16
N_GROUPS = 4
EXPERTS_PER_GROUP = 4
LN_EPS = 1e-5
RMS_EPS = 1e-6

LANES = 128
SUBLANES = 8
PACKED_ROWS = 16
ONES_AT = 64
NEG_BIG = -1e30
LOG2E = math.log2(math.e)
INT_MIN = -(2 ** 31)
VMEM_LIMIT = 56 * 1024 * 1024

BF16 = jnp.bfloat16
F32 = jnp.float32

_NT = (((1,), (1,)), ((), ()))


def _dot(a, b):
    return jnp.dot(a, b, preferred_element_type=F32)


def _dot_nt(a, b):
    return lax.dot_general(a, b, _NT, preferred_element_type=F32)


N_ROPE_UNITS = 10


def _inproj_kernel(x_ref, w_ref, ch_ref, s1h_ref, s2h_ref, cr_ref, s1r_ref, s2r_ref, gq_ref, gkv_ref,
                   wuq_ref, wukv_ref, wt_ref,
                   aq_ref, ak_ref, bq_ref, bk_ref, ik_ref, iq_ref, avt_ref, vt_ref, iwt_ref, qc_ref, kc_ref, vc_ref,
                   *, b_scale, c_scale):
    xb = x_ref[...].astype(BF16)
    tm = xb.shape[0]
    lane = lax.broadcasted_iota(jnp.int32, (tm, LANES), 1)
    ch, s1h, s2h = ch_ref[...], s1h_ref[...], s2h_ref[...]
    cr, s1r, s2r = cr_ref[...], s1r_ref[...], s2r_ref[...]
    low = lane < 64

    def rope_h(u):
        return u * ch + pltpu.roll(u, 96, 1) * s1h + pltpu.roll(u, 32, 1) * s2h

    def rope_r(u):
        return u * cr + pltpu.roll(u, 112, 1) * s1r + pltpu.roll(u, 16, 1) * s2r

    def split(u):
        return jnp.where(low, u, 0.0), jnp.where(low, pltpu.roll(u, 64, 1), 0.0)

    def put(ref, k, u):
        ref[:, k * LANES:(k + 1) * LANES] = u.astype(BF16)

    projected = [_dot(xb, w_ref[:, g * 256:(g + 1) * 256]) for g in range(N_ROPE_UNITS // 2)]
    for g, hg in enumerate(projected):
        for half in range(2):
            unit = 2 * g + half
            u = rope_h(hg[:, half * LANES:(half + 1) * LANES])
            if unit < 4:
                put(aq_ref, unit, u * LOG2E)
            elif unit == 4:
                swapped = pltpu.roll(u, 64, 1)
                put(ak_ref, 0, jnp.where(low, u, 0.0))
                put(ak_ref, 1, jnp.where(low, 0.0, swapped))
                put(ak_ref, 2, jnp.where(low, swapped, 0.0))
                put(ak_ref, 3, jnp.where(low, 0.0, u))
            elif unit < 7:
                h0, h1 = split(u * b_scale)
                put(bq_ref, 2 * (unit - 5), h0)
                put(bq_ref, 2 * (unit - 5) + 1, h1)
            elif unit == 7:
                h0, h1 = split(u)
                put(bk_ref, 0, h0)
                put(ik_ref, 0, h1)
            else:
                h0, h1 = split(u)
                put(iq_ref, 2 * (unit - 8), h0)
                put(iq_ref, 2 * (unit - 8) + 1, h1)

    hb = _dot_nt(wt_ref[...], xb)
    sub = lax.broadcasted_iota(jnp.int32, (LANES, tm), 0)
    with_ones = lambda piece: jnp.where(sub == ONES_AT, 1.0, hb[piece * LANES:(piece + 1) * LANES]).astype(BF16)
    vt = with_ones(0)
    for j in range(tm // ATT_KC):
        vt_ref[j] = vt[:, j * ATT_KC:(j + 1) * ATT_KC]
    iwt_ref[...] = hb[LANES:LANES + 8]
    for g in range(A_KV_HEADS):
        avt_ref[g] = with_ones(2 + g)

    cq = _dot(xb, w_ref[:, 1280:1536])
    cqn = cq * lax.rsqrt(jnp.mean(cq * cq, axis=1, keepdims=True) + RMS_EPS) * gq_ref[...]
    qc = _dot(cqn.astype(BF16), wuq_ref[...])
    for h in range(C_HEADS):
        u = qc[:, h * LANES:(h + 1) * LANES]
        qc_ref[:, h * LANES:(h + 1) * LANES] = (rope_r(u) * c_scale).astype(BF16)

    ckv = _dot(xb, w_ref[:, 1536:1792])
    ckv_lat = ckv[:, :LANES]
    kr = rope_r(ckv[:, LANES:])
    ckvn = ckv_lat * lax.rsqrt(jnp.mean(ckv_lat * ckv_lat, axis=1, keepdims=True) + RMS_EPS) * gkv_ref[...]
    kv = _dot(ckvn.astype(BF16), wukv_ref[...])
    for h in range(C_HEADS):
        kc_ref[:, h * LANES:(h + 1) * LANES] = (kv[:, h * LANES:(h + 1) * LANES] + kr).astype(BF16)
        v = kv[:, (C_HEADS + h) * LANES:(C_HEADS + h + 1) * LANES]
        vc_ref[:, h * LANES:(h + 1) * LANES] = jnp.where(lane == ONES_AT, 1.0, v).astype(BF16)


def _pack_w_in(w):
    d = w.shape[0]
    z = lambda n: jnp.zeros((d, n), F32)
    o = np.cumsum([0, 512, 128, 128, 256, 64, 64, 256, 64, 4, 256, 128, 32]).tolist()
    a_q, a_k, a_v, b_q, b_k, b_v, i_q, i_k, i_w, c_q, c_kv, c_kr = [w[:, o[j]:o[j + 1]] for j in range(12)]
    qs = HEAD_DIM ** -0.5
    ws = (IDX_HEADS * IDX_DIM) ** -0.5
    cols = [a_q * qs, a_k, b_q, b_k, i_k, i_q,
            c_q, c_kv, z(64), c_kr, z(32)]
    rows_t = jnp.concatenate([b_v, z(64), i_w * ws, z(LANES - IDX_HEADS),
                              a_v[:, :64], z(64), a_v[:, 64:], z(64)], axis=1).T
    return jnp.concatenate(cols, axis=1).astype(BF16), rows_t.astype(BF16)


def _pack_mla_w(w_uq, w_ukv):
    r = w_uq.shape[0]
    q = w_uq.reshape(r, C_HEADS, C_NOPE + C_ROPE)
    q = jnp.pad(q, ((0, 0), (0, 0), (0, LANES - C_NOPE - C_ROPE))).reshape(r, C_HEADS * LANES)
    r2 = w_ukv.shape[0]
    kv = w_ukv.reshape(r2, C_HEADS, C_NOPE + C_V)
    k = jnp.pad(kv[:, :, :C_NOPE], ((0, 0), (0, 0), (0, LANES - C_NOPE))).reshape(r2, C_HEADS * LANES)
    v = jnp.pad(kv[:, :, C_NOPE:], ((0, 0), (0, 0), (0, LANES - C_V))).reshape(r2, C_HEADS * LANES)
    return q.astype(BF16), jnp.concatenate([k, v], axis=1).astype(BF16)


def _rope_tables(s):
    pos = jnp.arange(s, dtype=F32)[:, None]
    lane = np.arange(LANES)
    inv_h = 1.0 / (ROPE_THETA ** (jnp.arange(0, HEAD_DIM, 2, dtype=F32) / HEAD_DIM))
    ang = pos * inv_h[None, :]
    cos, sin = jnp.cos(ang), jnp.sin(ang)
    j = lane % 32
    lo = jnp.asarray((lane % 64) < 32)
    ch = cos[:, j]
    s1h = jnp.where(lo, -sin[:, j], 0.0)
    s2h = jnp.where(lo, 0.0, sin[:, j])
    inv_r = 1.0 / (ROPE_THETA ** (jnp.arange(0, C_ROPE, 2, dtype=F32) / C_ROPE))
    angr = pos * inv_r[None, :]
    cosr, sinr = jnp.cos(angr), jnp.sin(angr)
    jr = lane % 16
    in_rope = jnp.asarray((lane >= 64) & (lane < 96))
    first = jnp.asarray((lane >= 64) & (lane < 80))
    second = jnp.asarray((lane >= 80) & (lane < 96))
    cr = jnp.where(in_rope, cosr[:, jr], 1.0)
    s1r = jnp.where(first, -sinr[:, jr], 0.0)
    s2r = jnp.where(second, sinr[:, jr], 0.0)
    return ch, s1h, s2h, cr, s1r, s2r


def _inproj(x2, w_p, w_t, tables, gq, gkv, wuq_p, wukv_p, s, tm):
    n, d = x2.shape
    nt = n // tm
    spt = s // tm
    cpt = tm // ATT_KC
    tab_spec = pl.BlockSpec((tm, LANES), lambda t: (t % spt, 0))
    full = lambda a: pl.BlockSpec(a.shape, lambda t: (0,) * a.ndim)
    row = lambda c: pl.BlockSpec((tm, c), lambda t: (t, 0))
    cols_a, cols_c = [512, 512, 512, 128, 128, 512], [512, 512, 512]
    out_specs = ([row(c) for c in cols_a]
                 + [pl.BlockSpec((A_KV_HEADS, LANES, tm), lambda t: (0, 0, t)),
                    pl.BlockSpec((cpt, LANES, ATT_KC), lambda t: (t, 0, 0)), pl.BlockSpec((8, tm), lambda t: (0, t))]
                 + [row(c) for c in cols_c])
    out_shape = ([jax.ShapeDtypeStruct((n, c), BF16) for c in cols_a]
                 + [jax.ShapeDtypeStruct((A_KV_HEADS, LANES, n), BF16),
                    jax.ShapeDtypeStruct((n // ATT_KC, LANES, ATT_KC), BF16), jax.ShapeDtypeStruct((8, n), F32)]
                 + [jax.ShapeDtypeStruct((n, c), BF16) for c in cols_c])
    return pl.pallas_call(
        functools.partial(_inproj_kernel, b_scale=HEAD_DIM ** -0.5 * LOG2E,
                          c_scale=(C_NOPE + C_ROPE) ** -0.5 * LOG2E),
        grid=(nt,),
        in_specs=[row(d), full(w_p)] + [tab_spec] * 6 + [full(gq), full(gkv), full(wuq_p), full(wukv_p), full(w_t)],
        out_specs=out_specs,
        out_shape=out_shape,
        compiler_params=pltpu.CompilerParams(dimension_semantics=("parallel",), vmem_limit_bytes=VMEM_LIMIT),
    )(x2, w_p, *tables, gq, gkv, wuq_p, wukv_p, w_t)


def _swa_kernel(sink_ref, q_ref, kc_ref, kp_ref, vc_ref, vp_ref, o_ref):
    i = pl.program_id(1)
    w = A_WINDOW
    qb = q_ref.shape[0]
    kj = lax.broadcasted_iota(jnp.int32, (2 * w, w), 0)
    qi = lax.broadcasted_iota(jnp.int32, (2 * w, w), 1)
    diff = qi + w - kj
    in_window = (diff >= 0) & (diff < w)
    groups = [slice(g * SUBLANES, (g + 1) * SUBLANES) for g in range(2 * w // SUBLANES)]
    lane_head = lax.broadcasted_iota(jnp.int32, (1, 4 * w), 1) // w
    per_group = A_HEADS // A_KV_HEADS
    tasks = [(win, g) for win in range(qb // w) for g in range(A_KV_HEADS)]
    head_order = lambda g: [per_group * g, per_group * g + 2, per_group * g + 1, per_group * g + 3]

    def keys(win, unit):
        cols = slice(unit * LANES, (unit + 1) * LANES)
        if win == 0:
            return jnp.concatenate([kp_ref[:, cols], kc_ref[0:w, cols]], axis=0)
        return kc_ref[(win - 1) * w:(win + 1) * w, cols]

    def values_t(win, g):
        if win == 0:
            return jnp.concatenate([vp_ref[g], vc_ref[g, :, 0:w]], axis=1)
        return vc_ref[g, :, (win - 1) * w:(win + 1) * w]

    scores = []
    for win, g in tasks:
        rows = slice(win * w, (win + 1) * w)
        ok = in_window if win > 0 else in_window & ((kj >= w) | (i > 0))
        bias = jnp.concatenate([jnp.where(ok, 0.0, NEG_BIG)] * per_group, axis=1)
        q2 = jnp.concatenate([q_ref[rows, (2 * g + u) * LANES:(2 * g + u + 1) * LANES] for u in range(2)], axis=0)
        scores.append(jnp.concatenate([_dot_nt(keys(win, 2 * g), q2), _dot_nt(keys(win, 2 * g + 1), q2)], axis=1)
                      + bias)
    probs = []
    for (win, g), s in zip(tasks, scores):
        sink = jnp.zeros((1, per_group * w), F32)
        for slot, h in enumerate(head_order(g)):
            sink = jnp.where(lane_head == slot, sink_ref[h] * LOG2E, sink)
        parts = [s[gg] for gg in groups]
        while len(parts) > 1:
            parts = [jnp.maximum(parts[j], parts[j + 1]) for j in range(0, len(parts), 2)]
        m = jnp.maximum(jnp.max(parts[0], axis=0, keepdims=True), sink)
        probs.append((jnp.exp2(s - m).astype(BF16), jnp.exp2(sink - m)))
    outs = [_dot(values_t(win, g), p) for (win, g), (p, _) in zip(tasks, probs)]
    for (win, g), o, (_, sink_p) in zip(tasks, outs, probs):
        rows = slice(win * w, (win + 1) * w)
        o = (o / (o[ONES_AT:ONES_AT + 1, :] + sink_p)).T
        for slot, h in enumerate(head_order(g)):
            o_ref[rows, h * HEAD_DIM:(h + 1) * HEAD_DIM] = o[slot * w:(slot + 1) * w, :HEAD_DIM].astype(BF16)


def _swa(sinks, aq, ak, avt, b, s):
    n = aq.shape[0]
    w = A_WINDOW
    qb = _att_qb(s)
    nb = s // qb
    wpb = qb // w
    cur = lambda bb, i: (bb * nb + i, 0)
    prev = lambda bb, i: ((bb * nb + i) * wpb - jnp.minimum(i, 1), 0)
    cur_t = lambda bb, i: (0, 0, bb * nb + i)
    prev_t = lambda bb, i: (0, 0, (bb * nb + i) * wpb - jnp.minimum(i, 1))
    return pl.pallas_call(
        _swa_kernel,
        grid=(b, nb),
        in_specs=[pl.BlockSpec(memory_space=pltpu.SMEM),
                  pl.BlockSpec((qb, 512), cur),
                  pl.BlockSpec((qb, 512), cur), pl.BlockSpec((w, 512), prev),
                  pl.BlockSpec((A_KV_HEADS, LANES, qb), cur_t), pl.BlockSpec((A_KV_HEADS, LANES, w), prev_t)],
        out_specs=pl.BlockSpec((qb, 512), cur),
        out_shape=jax.ShapeDtypeStruct((n, 512), BF16),
        compiler_params=pltpu.CompilerParams(dimension_semantics=("parallel", "parallel"),
                                             vmem_limit_bytes=VMEM_LIMIT),
    )(sinks, aq, ak, ak, avt, avt)


ATT_KC = 256
COARSE_CHAINS = 2


def _att_qb(s):
    return min(512, s)


def _fold_max(s):
    return jnp.maximum(s[:, :LANES], s[:, LANES:])


def _unrolled(n, body, carry):
    for c in range(n):
        carry = body(c, carry)
    return carry


def _dsa_kernel(bq_ref, iq_ref, iwt_ref, bk_ref, ik_ref, vt_ref, _aliased_out_ref, o_ref,
                sc_ref, sc16_ref, s_ref, tau_ref, j_ref, ist_ref, qst_ref, m_ref, *, blk, topk, idx_bits):
    i = blk
    qb = bq_ref.shape[0]
    kc = ATT_KC
    per = qb // kc
    n_full = per * i
    n_chunks = n_full + per
    kidx = lax.broadcasted_iota(jnp.int32, (kc, qb), 0)
    rpos = lax.broadcasted_iota(jnp.int32, (kc, qb), 1)
    neg_inf = float("-inf")
    kstart = lambda c: c * kc
    heads = range(B_HEADS)
    hrows = lambda h: slice(h * qb, (h + 1) * qb)
    groups = [slice(g * SUBLANES, (g + 1) * SUBLANES) for g in range(kc // SUBLANES)]

    def fold(x, op, ways=4):
        parts = [x[g] for g in groups[:ways]]
        for j, g in enumerate(groups[ways:]):
            parts[j % ways] = op(parts[j % ways], x[g])
        while len(parts) > 1:
            parts = [op(parts[j], parts[j + 1]) for j in range(0, len(parts), 2)]
        return parts[0]

    for h in range(IDX_HEADS):
        ist_ref[hrows(h), :] = iq_ref[:, h * LANES:(h + 1) * LANES]

    def score(c):
        lg = _dot_nt(ik_ref[pl.ds(kstart(c), kc), :], ist_ref[...])
        acc = None
        for h in range(IDX_HEADS):
            t = iwt_ref[h:h + 1, :] * jnp.maximum(lg[:, hrows(h)], 0.0)
            acc = t if acc is None else acc + t
        return acc

    rw = qb // per

    def put_scores(c, sc):
        for r in range(per):
            sc_ref[c, r] = sc[:, r * rw:(r + 1) * rw]
            sc16_ref[c, r] = sc[:, r * rw:(r + 1) * rw].astype(BF16)

    def idx_body(c, carry):
        put_scores(c, score(c))
        return carry

    _unrolled(n_full, idx_body, 0)
    for d in range(per):
        put_scores(n_full + d, jnp.where(d * kc + kidx <= rpos, score(n_full + d), neg_inf))

    kidx_r = lax.broadcasted_iota(jnp.int32, (kc, rw), 0)
    for r in range(per):
        rows = slice(r * rw, (r + 1) * rw)
        tpos = i * qb + r * rw + lax.broadcasted_iota(jnp.int32, (1, rw), 1)
        n_r = n_full + r + 1
        if i * qb + (r + 1) * rw <= int(topk):
            tau_ref[:, rows] = jnp.full((SUBLANES, rw), neg_inf, F32)
            j_ref[:, rows] = jnp.broadcast_to(tpos, (SUBLANES, rw))
            continue

        def count(pred_fn):
            def cbody(c, acc):
                k = sc_ref[c, r]
                return acc + fold(jnp.where(pred_fn(k, c), 1.0, 0.0), jnp.add, ways=1)
            acc = _unrolled(n_r, cbody, jnp.zeros((SUBLANES, rw), F32))
            return jnp.sum(acc, axis=0, keepdims=True)

        def key_to_f32(key_u):
            key = key_u ^ INT_MIN
            return pltpu.bitcast(key ^ ((key >> 31) & 0x7FFFFFFF), F32)

        def count16(cand16):
            def cbody(c, acc):
                k = sc16_ref[c, r]
                hit = jnp.where(k >= cand16, jnp.ones_like(k), jnp.zeros_like(k))
                acc = list(acc)
                for g in range(kc // PACKED_ROWS):
                    acc[g % COARSE_CHAINS] = acc[g % COARSE_CHAINS] + hit[g * PACKED_ROWS:(g + 1) * PACKED_ROWS]
                return tuple(acc)
            zero = jnp.zeros((PACKED_ROWS, rw), BF16)
            acc = _unrolled(n_r, cbody, (zero,) * COARSE_CHAINS)
            total = acc[0].astype(F32)
            for part in acc[1:]:
                total = total + part.astype(F32)
            return jnp.sum(total, axis=0, keepdims=True)

        def coarse_body(step, prefix):
            cand_u = prefix | lax.shift_left(jnp.int32(1), 31 - step)
            key = cand_u ^ INT_MIN
            cand = pltpu.bitcast(key ^ ((key >> 31) & 0x7FFF0000), F32).astype(BF16)
            return jnp.where(count16(cand) >= topk, cand_u, prefix)

        coarse = lax.fori_loop(0, 16, coarse_body, jnp.zeros((1, rw), jnp.int32))
        step16 = 1 << 16
        key_neg_inf = INT_MIN + 0x7FFFFF
        key_pos_inf = 0x7F800000
        key16 = coarse ^ INT_MIN
        key16 = key16 + ((key16 >> 31) & 0xFFFF)
        base = jnp.clip(key16, key_neg_inf + step16, key_pos_inf) - step16

        def fine_body(step, off):
            cand_off = off | lax.shift_left(jnp.int32(1), 16 - step)
            cand = key_to_f32((base + cand_off) ^ INT_MIN)
            cnt = count(lambda k, c: k >= cand)
            return jnp.where(cnt >= topk, cand_off, off)

        off = lax.fori_loop(0, 17, fine_body, jnp.zeros((1, rw), jnp.int32))
        tau = key_to_f32((base + off) ^ INT_MIN)
        tau = jnp.where(tau != tau, neg_inf, tau)
        c_gt = count(lambda k, c: k > tau)
        c_eq = count(lambda k, c: k == tau)
        need = topk - c_gt

        def tie_search(_):
            def jbody(step, q):
                cand = q | lax.shift_left(jnp.int32(1), idx_bits - 1 - step)
                cnt = count(lambda k, c: (k == tau) & (c * kc + kidx_r < cand))
                return jnp.where(cnt < need, cand, q)
            return lax.fori_loop(0, idx_bits, jbody, jnp.zeros((1, rw), jnp.int32))

        any_split = jnp.max(c_eq - need) > 0.0
        jcut = lax.cond(any_split, tie_search, lambda _: jnp.full((1, rw), 2 ** idx_bits, jnp.int32), 0)
        few = tpos < int(topk)
        tau_ref[:, rows] = jnp.broadcast_to(jnp.where(few, neg_inf, tau), (SUBLANES, rw))
        j_ref[:, rows] = jnp.broadcast_to(jnp.where(few, tpos, jcut), (SUBLANES, rw))

    for h in heads:
        qst_ref[hrows(h), :] = bq_ref[:, h * LANES:(h + 1) * LANES]
    m_ref[...] = jnp.full(m_ref.shape, NEG_BIG, F32)

    def max_body(c, carry):
        s = _dot_nt(bk_ref[pl.ds(kstart(c), kc), :], qst_ref[...])
        k = jnp.concatenate([sc_ref[c, r] for r in range(per)], axis=1)
        tau = tau_ref[0:1, :]
        sel = (k > tau) | ((k == tau) & (c * kc + kidx <= j_ref[0:1, :]))
        bias = jnp.where(sel, 0.0, NEG_BIG)
        for h in heads:
            sm = s[:, hrows(h)] + bias
            s_ref[c, :, hrows(h)] = sm
            m_ref[:, hrows(h)] = jnp.maximum(m_ref[:, hrows(h)], fold(sm, jnp.maximum, ways=1))
        return carry

    _unrolled(n_chunks, max_body, 0)
    m_ref[...] = jnp.broadcast_to(jnp.max(m_ref[...], axis=0, keepdims=True), m_ref.shape)

    probs = [jnp.exp2(s_ref[c] - m_ref[0:1, :]).astype(BF16) for c in range(n_chunks)]
    out_t = _dot(jnp.concatenate([vt_ref[c] for c in range(n_chunks)], axis=1), jnp.concatenate(probs, axis=0))
    for h in heads:
        acc = out_t[:, hrows(h)]
        o_ref[:, h * HEAD_DIM:(h + 1) * HEAD_DIM] = (acc / acc[ONES_AT:ONES_AT + 1, :]).T[:, :HEAD_DIM].astype(BF16)


def _dsa(bq, iq, iwt, bk, ik, vt, b, s):
    n = bq.shape[0]
    qb = _att_qb(s)
    nb = s // qb
    nkc = s // ATT_KC
    topk = min(IDX_TOPK_MAX, s // 4)
    idx_bits = max(1, int(math.ceil(math.log2(s))))
    seq = lambda bb: (bb, 0)
    out = jnp.zeros((n, 256), BF16)
    for i in range(nb):
        blk = lambda bb, i=i: (bb * nb + i, 0)
        n_kc = (i + 1) * (qb // ATT_KC)
        out = pl.pallas_call(
            functools.partial(_dsa_kernel, blk=i, topk=float(topk), idx_bits=idx_bits),
            grid=(b,),
            in_specs=[pl.BlockSpec((qb, 512), blk), pl.BlockSpec((qb, 512), blk),
                      pl.BlockSpec((8, qb), lambda bb, i=i: (0, bb * nb + i)),
                      pl.BlockSpec((s, 128), seq), pl.BlockSpec((s, 128), seq),
                      pl.BlockSpec((nkc, LANES, ATT_KC), lambda bb: (bb, 0, 0)),
                      pl.BlockSpec(memory_space=pl.ANY)],
            out_specs=pl.BlockSpec((qb, 256), blk),
            out_shape=jax.ShapeDtypeStruct((n, 256), BF16),
            input_output_aliases={6: 0},
            scratch_shapes=[pltpu.VMEM((n_kc, qb // ATT_KC, ATT_KC, ATT_KC), F32),
                            pltpu.VMEM((n_kc, qb // ATT_KC, ATT_KC, ATT_KC), BF16),
                            pltpu.VMEM((n_kc, ATT_KC, B_HEADS * qb), F32),
                            pltpu.VMEM((SUBLANES, qb), F32), pltpu.VMEM((SUBLANES, qb), jnp.int32),
                            pltpu.VMEM((IDX_HEADS * qb, LANES), BF16), pltpu.VMEM((B_HEADS * qb, LANES), BF16),
                            pltpu.VMEM((SUBLANES, B_HEADS * qb), F32)],
            compiler_params=pltpu.CompilerParams(dimension_semantics=("parallel",), vmem_limit_bytes=VMEM_LIMIT),
        )(bq, iq, iwt, bk, ik, vt, out)
    return out


def _mla_kernel(q_ref, k_ref, v_ref, _aliased_out_ref, o_ref, s_ref, m_ref, *, blk):
    qb = q_ref.shape[0]
    kc = ATT_KC
    per = qb // kc
    n_full = per * blk
    n_chunks = n_full + per
    row = lax.broadcasted_iota(jnp.int32, (qb, kc), 0)
    col = lax.broadcasted_iota(jnp.int32, (qb, kc), 1)
    heads = range(C_HEADS)
    hs = lambda h: slice(h * LANES, (h + 1) * LANES)
    keys = lambda c: slice(c * kc, (c + 1) * kc)
    n_halves = kc // LANES
    m_ref[...] = jnp.full(m_ref.shape, NEG_BIG, F32)

    for c in range(n_chunks):
        diag = c - n_full
        for h in heads:
            s = _dot_nt(q_ref[:, hs(h)], k_ref[keys(c), hs(h)])
            if diag >= 0:
                s = s + jnp.where(diag * kc + col <= row, 0.0, NEG_BIG)
            s_ref[h, c] = s
            m_ref[h] = jnp.maximum(m_ref[h], _fold_max(s))
    for h in heads:
        m_ref[h] = jnp.broadcast_to(jnp.max(m_ref[h], axis=1, keepdims=True), (qb, LANES))
    for h in heads:
        m = jnp.concatenate([m_ref[h]] * n_halves, axis=1)
        p = jnp.concatenate([jnp.exp2(s_ref[h, c] - m).astype(BF16) for c in range(n_chunks)], axis=1)
        acc = _dot(p, v_ref[0:n_chunks * kc, hs(h)])
        o_ref[:, h * C_V:(h + 1) * C_V] = (acc[:, :C_V] / acc[:, ONES_AT:ONES_AT + 1]).astype(BF16)


def _mla(qc, kc, vc, b, s):
    n = qc.shape[0]
    qb = _att_qb(s)
    nb = s // qb
    seq = lambda bb: (bb, 0)
    out = jnp.zeros((n, 256), BF16)
    for i in range(nb):
        blk = lambda bb, i=i: (bb * nb + i, 0)
        n_kc = (i + 1) * (qb // ATT_KC)
        out = pl.pallas_call(
            functools.partial(_mla_kernel, blk=i),
            grid=(b,),
            in_specs=[pl.BlockSpec((qb, 512), blk), pl.BlockSpec((s, 512), seq), pl.BlockSpec((s, 512), seq),
                      pl.BlockSpec(memory_space=pl.ANY)],
            out_specs=pl.BlockSpec((qb, 256), blk),
            out_shape=jax.ShapeDtypeStruct((n, 256), BF16),
            input_output_aliases={3: 0},
            scratch_shapes=[pltpu.VMEM((C_HEADS, n_kc, qb, ATT_KC), F32), pltpu.VMEM((C_HEADS, qb, LANES), F32)],
            compiler_params=pltpu.CompilerParams(dimension_semantics=("parallel",), vmem_limit_bytes=VMEM_LIMIT),
        )(qc, kc, vc, out)
    return out


def _layer_norm(z, g, b):
    mu = jnp.mean(z, axis=1, keepdims=True)
    zc = z - mu
    var = jnp.mean(zc * zc, axis=1, keepdims=True)
    return zc * lax.rsqrt(var + LN_EPS) * g + b


def _route(logits, rbias):
    scores = jax.nn.sigmoid(logits)
    biased = scores + rbias
    rows = [biased[e:e + 1, :] for e in range(N_EXPERTS)]
    gscore = []
    for g in range(N_GROUPS):
        v = rows[g * 4:(g + 1) * 4]
        best = None
        for a in range(4):
            for c in range(a + 1, 4):
                pair = v[a] + v[c]
                best = pair if best is None else jnp.maximum(best, pair)
        gscore.append(best)
    gmax = jnp.maximum(jnp.maximum(gscore[0], gscore[1]), jnp.maximum(gscore[2], gscore[3]))
    taken = jnp.zeros_like(gmax) > 1.0
    sel_rows = []
    for g in range(N_GROUPS):
        g_sel = (gscore[g] == gmax) & jnp.logical_not(taken)
        taken = taken | g_sel
        v = rows[g * 4:(g + 1) * 4]
        for e in range(4):
            rank = jnp.zeros_like(gmax)
            for j in range(4):
                if j == e:
                    continue
                ahead = (v[j] > v[e]) | ((v[j] == v[e]) & (j < e))
                rank = rank + jnp.where(ahead, 1.0, 0.0)
            sel_rows.append(g_sel & (rank < 2.0))
    wsel = [jnp.where(sel_rows[e], scores[e:e + 1, :], 0.0) for e in range(N_EXPERTS)]
    total = wsel[0]
    for e in range(1, N_EXPERTS):
        total = total + wsel[e]
    gates_t = jnp.concatenate(wsel + [jnp.zeros((LANES - N_EXPERTS, total.shape[1]), F32)], axis=0) / total
    return gates_t.T


OUT_SLAB = 256


def _outproj_kernel(x_ref, oa_ref, ob_ref, oc_ref, wo_ref, g_ref, b_ref, wr_ref, rb_ref, x1_ref, gates_ref,
                    *, alpha):
    tm = x_ref.shape[0]
    slabs = [slice(r, r + OUT_SLAB) for r in range(0, tm, OUT_SLAB)]
    ys = [_dot(oa_ref[rows, :], wo_ref[0:512, :]) + _dot(ob_ref[rows, :], wo_ref[512:768, :])
          + _dot(oc_ref[rows, :], wo_ref[768:1024, :]) for rows in slabs]
    x1s = [_layer_norm(alpha * x_ref[rows, :] + y, g_ref[...], b_ref[...]) for rows, y in zip(slabs, ys)]
    for rows, x1 in zip(slabs, x1s):
        x1_ref[rows, :] = x1

    wr = wr_ref[...]
    wrh = wr.astype(BF16)
    wrl = (wr - wrh.astype(F32)).astype(BF16)
    wr2 = jnp.concatenate([wrh, wrl], axis=0)
    for rows, x1 in zip(slabs, x1s):
        x1h = x1.astype(BF16)
        x1l = (x1 - x1h.astype(F32)).astype(BF16)
        hi = _dot_nt(wr2, x1h)
        gates = _route(hi[:N_EXPERTS] + hi[N_EXPERTS:] + _dot_nt(wrh, x1l), rb_ref[...])
        for g in range(N_GROUPS):
            gates_ref[g, rows, :] = gates[:, g * EXPERTS_PER_GROUP:(g + 1) * EXPERTS_PER_GROUP]


def _outproj(x2, oa, ob, oc, wo, g, bb, wr_t, rbias, alpha, tm):
    n, d = x2.shape
    nt = n // tm
    full = lambda a: pl.BlockSpec(a.shape, lambda t: (0,) * a.ndim)
    row = lambda c: pl.BlockSpec((tm, c), lambda t: (t, 0))
    return pl.pallas_call(
        functools.partial(_outproj_kernel, alpha=alpha),
        grid=(nt,),
        in_specs=[row(d), row(512), row(256), row(256), full(wo), full(g), full(bb), full(wr_t), full(rbias)],
        out_specs=[row(d), pl.BlockSpec((N_GROUPS, tm, EXPERTS_PER_GROUP), lambda t: (0, t, 0))],
        out_shape=[jax.ShapeDtypeStruct((n, d), F32),
                   jax.ShapeDtypeStruct((N_GROUPS, n, EXPERTS_PER_GROUP), F32)],
        compiler_params=pltpu.CompilerParams(dimension_semantics=("parallel",), vmem_limit_bytes=VMEM_LIMIT),
    )(x2, oa, ob, oc, wo, g, bb, wr_t, rbias)


def _moe_kernel(x_ref, gates_ref, wg_ref, wu_ref, wd_ref, g_ref, b_ref, o_ref, acc_ref, *, alpha):
    grp = pl.program_id(1)

    @pl.when(grp == 0)
    def _():
        acc_ref[...] = jnp.zeros(acc_ref.shape, F32)

    xb = x_ref[...].astype(BF16)
    hs = []
    for j in range(EXPERTS_PER_GROUP):
        hg = _dot(xb, wg_ref[j])
        hu = _dot(xb, wu_ref[j])
        hs.append((hg * jax.nn.sigmoid(hg) * hu * gates_ref[0, :, j:j + 1]).astype(BF16))
    wd = wd_ref[...]
    acc_ref[...] += _dot(jnp.concatenate(hs, axis=1), wd.reshape(wd.shape[0] * wd.shape[1], wd.shape[2]))

    @pl.when(grp == N_GROUPS - 1)
    def _():
        o_ref[...] = _layer_norm(alpha * x_ref[...] + acc_ref[...], g_ref[...], b_ref[...])


def _moe(x1, gates_g, wg, wu, wd, g, bb, alpha, tm):
    n, d = x1.shape
    nt = n // tm
    f = wg.shape[2]
    epg = EXPERTS_PER_GROUP
    full = lambda a: pl.BlockSpec(a.shape, lambda t, e: (0,) * a.ndim)
    return pl.pallas_call(
        functools.partial(_moe_kernel, alpha=alpha),
        grid=(nt, N_GROUPS),
        in_specs=[pl.BlockSpec((tm, d), lambda t, e: (t, 0)), pl.BlockSpec((1, tm, epg), lambda t, e: (e, t, 0)),
                  pl.BlockSpec((epg, d, f), lambda t, e: (e, 0, 0)), pl.BlockSpec((epg, d, f), lambda t, e: (e, 0, 0)),
                  pl.BlockSpec((epg, f, d), lambda t, e: (e, 0, 0)), full(g), full(bb)],
        out_specs=pl.BlockSpec((tm, d), lambda t, e: (t, 0)),
        out_shape=jax.ShapeDtypeStruct((n, d), F32),
        scratch_shapes=[pltpu.VMEM((tm, d), F32)],
        compiler_params=pltpu.CompilerParams(dimension_semantics=("parallel", "arbitrary"),
                                             vmem_limit_bytes=VMEM_LIMIT),
    )(x1, gates_g, wg, wu, wd, g, bb)


def kernel(x, w_in, attn_sinks, c_q_norm_g, c_kv_norm_g, w_uq, w_ukv, w_out, ln1_g, ln1_b, w_router, router_bias,
           w_gate, w_up, w_down, ln2_g, ln2_b):
    b, s, d = x.shape
    depth = w_in.shape[0]
    n = b * s
    alpha = (2 * depth) ** 0.25
    tm = min(512, s)
    tables = _rope_tables(s)
    wr_t = w_router.T
    rbias = router_bias.reshape(N_EXPERTS, 1)
    x2 = x.reshape(n, d)
    for l in range(depth):
        w_p, w_t = _pack_w_in(w_in[l])
        wuq_p, wukv_p = _pack_mla_w(w_uq[l], w_ukv[l])
        aq, ak, bq, bk, ik, iq, avt, vt, iwt, qc, kc, vc = _inproj(
            x2, w_p, w_t, tables, c_q_norm_g[l].reshape(1, -1), c_kv_norm_g[l].reshape(1, -1), wuq_p, wukv_p, s, tm)
        oa = _swa(attn_sinks[l], aq, ak, avt, b, s)
        ob = _dsa(bq, iq, iwt, bk, ik, vt, b, s)
        oc = _mla(qc, kc, vc, b, s)
        x1, gates_g = _outproj(x2, oa, ob, oc, w_out[l].astype(BF16), ln1_g[l].reshape(1, d), ln1_b[l].reshape(1, d),
                               wr_t, rbias, alpha, min(4 * OUT_SLAB, n))
        x2 = _moe(x1, gates_g, w_gate[l].astype(BF16), w_up[l].astype(BF16), w_down[l].astype(BF16),
                  ln2_g[l].reshape(1, d), ln2_b[l].reshape(1, d), alpha, min(1024, n))
    return x2.reshape(b, s, d)
```

```python
import functools
import math

import jax
import jax.numpy as jnp
import numpy as np
from jax import lax
from jax.experimental import pallas as pl
from jax.experimental.pallas import tpu as pltpu

HEAD_DIM = 64
ROPE_THETA = 10000.0
A_HEADS = 8
A_KV_HEADS = 2
A_WINDOW = 128
B_HEADS = 4
IDX_HEADS = 4
IDX_DIM = 64
IDX_TOPK_MAX = 256
C_HEADS = 4
C_NOPE = 64
C_ROPE = 32
C_V = 64
N_EXPERTS = 16
N_GROUPS = 4
EXPERTS_PER_GROUP = 4
LN_EPS = 1e-5
RMS_EPS = 1e-6

LANES = 128
SUBLANES = 8
PACKED_ROWS = 16
ONES_AT = 64
NEG_BIG = -1e30
LOG2E = math.log2(math.e)
INT_MIN = -(2 ** 31)
VMEM_LIMIT = 56 * 1024 * 1024

BF16 = jnp.bfloat16
F32 = jnp.float32

_NT = (((1,), (1,)), ((), ()))


def _dot(a, b):
    return jnp.dot(a, b, preferred_element_type=F32)


def _dot_nt(a, b):
    return lax.dot_general(a, b, _NT, preferred_element_type=F32)


N_ROPE_UNITS = 10


def _inproj_kernel(x_ref, w_ref, ch_ref, s1h_ref, s2h_ref, cr_ref, s1r_ref, s2r_ref, gq_ref, gkv_ref,
                   wuq_ref, wukv_ref, wt_ref,
                   aq_ref, ak_ref, bq_ref, bk_ref, ik_ref, iq_ref, avt_ref, vt_ref, iwt_ref, qc_ref, kc_ref, vc_ref,
                   *, b_scale, c_scale):
    xb = x_ref[...].astype(BF16)
    tm = xb.shape[0]
    lane = lax.broadcasted_iota(jnp.int32, (tm, LANES), 1)
    ch, s1h, s2h = ch_ref[...], s1h_ref[...], s2h_ref[...]
    cr, s1r, s2r = cr_ref[...], s1r_ref[...], s2r_ref[...]
    low = lane < 64

    def rope_h(u):
        return u * ch + pltpu.roll(u, 96, 1) * s1h + pltpu.roll(u, 32, 1) * s2h

    def rope_r(u):
        return u * cr + pltpu.roll(u, 112, 1) * s1r + pltpu.roll(u, 16, 1) * s2r

    def split(u):
        return jnp.where(low, u, 0.0), jnp.where(low, pltpu.roll(u, 64, 1), 0.0)

    def put(ref, k, u):
        ref[:, k * LANES:(k + 1) * LANES] = u.astype(BF16)

    projected = [_dot(xb, w_ref[:, g * 256:(g + 1) * 256]) for g in range(N_ROPE_UNITS // 2)]
    for g, hg in enumerate(projected):
        for half in range(2):
            unit = 2 * g + half
            u = rope_h(hg[:, half * LANES:(half + 1) * LANES])
            if unit < 4:
                put(aq_ref, unit, u * LOG2E)
            elif unit == 4:
                swapped = pltpu.roll(u, 64, 1)
                put(ak_ref, 0, jnp.where(low, u, 0.0))
                put(ak_ref, 1, jnp.where(low, 0.0, swapped))
                put(ak_ref, 2, jnp.where(low, swapped, 0.0))
                put(ak_ref, 3, jnp.where(low, 0.0, u))
            elif unit < 7:
                h0, h1 = split(u * b_scale)
                put(bq_ref, 2 * (unit - 5), h0)
                put(bq_ref, 2 * (unit - 5) + 1, h1)
            elif unit == 7:
                h0, h1 = split(u)
                put(bk_ref, 0, h0)
                put(ik_ref, 0, h1)
            else:
                h0, h1 = split(u)
                put(iq_ref, 2 * (unit - 8), h0)
                put(iq_ref, 2 * (unit - 8) + 1, h1)

    hb = _dot_nt(wt_ref[...], xb)
    sub = lax.broadcasted_iota(jnp.int32, (LANES, tm), 0)
    with_ones = lambda piece: jnp.where(sub == ONES_AT, 1.0, hb[piece * LANES:(piece + 1) * LANES]).astype(BF16)
    vt = with_ones(0)
    for j in range(tm // ATT_KC):
        vt_ref[j] = vt[:, j * ATT_KC:(j + 1) * ATT_KC]
    iwt_ref[...] = hb[LANES:LANES + 8]
    for g in range(A_KV_HEADS):
        avt_ref[g] = with_ones(2 + g)

    cq = _dot(xb, w_ref[:, 1280:1536])
    cqn = cq * lax.rsqrt(jnp.mean(cq * cq, axis=1, keepdims=True) + RMS_EPS) * gq_ref[...]
    qc = _dot(cqn.astype(BF16), wuq_ref[...])
    for h in range(C_HEADS):
        u = qc[:, h * LANES:(h + 1) * LANES]
        qc_ref[:, h * LANES:(h + 1) * LANES] = (rope_r(u) * c_scale).astype(BF16)

    ckv = _dot(xb, w_ref[:, 1536:1792])
    ckv_lat = ckv[:, :LANES]
    kr = rope_r(ckv[:, LANES:])
    ckvn = ckv_lat * lax.rsqrt(jnp.mean(ckv_lat * ckv_lat, axis=1, keepdims=True) + RMS_EPS) * gkv_ref[...]
    kv = _dot(ckvn.astype(BF16), wukv_ref[...])
    for h in range(C_HEADS):
        kc_ref[:, h * LANES:(h + 1) * LANES] = (kv[:, h * LANES:(h + 1) * LANES] + kr).astype(BF16)
        v = kv[:, (C_HEADS + h) * LANES:(C_HEADS + h + 1) * LANES]
        vc_ref[:, h * LANES:(h + 1) * LANES] = jnp.where(lane == ONES_AT, 1.0, v).astype(BF16)


def _pack_w_in(w):
    d = w.shape[0]
    z = lambda n: jnp.zeros((d, n), F32)
    o = np.cumsum([0, 512, 128, 128, 256, 64, 64, 256, 64, 4, 256, 128, 32]).tolist()
    a_q, a_k, a_v, b_q, b_k, b_v, i_q, i_k, i_w, c_q, c_kv, c_kr = [w[:, o[j]:o[j + 1]] for j in range(12)]
    qs = HEAD_DIM ** -0.5
    ws = (IDX_HEADS * IDX_DIM) ** -0.5
    cols = [a_q * qs, a_k, b_q, b_k, i_k, i_q,
            c_q, c_kv, z(64), c_kr, z(32)]
    rows_t = jnp.concatenate([b_v, z(64), i_w * ws, z(LANES - IDX_HEADS),
                              a_v[:, :64], z(64), a_v[:, 64:], z(64)], axis=1).T
    return jnp.concatenate(cols, axis=1).astype(BF16), rows_t.astype(BF16)


def _pack_mla_w(w_uq, w_ukv):
    r = w_uq.shape[0]
    q = w_uq.reshape(r, C_HEADS, C_NOPE + C_ROPE)
    q = jnp.pad(q, ((0, 0), (0, 0), (0, LANES - C_NOPE - C_ROPE))).reshape(r, C_HEADS * LANES)
    r2 = w_ukv.shape[0]
    kv = w_ukv.reshape(r2, C_HEADS, C_NOPE + C_V)
    k = jnp.pad(kv[:, :, :C_NOPE], ((0, 0), (0, 0), (0, LANES - C_NOPE))).reshape(r2, C_HEADS * LANES)
    v = jnp.pad(kv[:, :, C_NOPE:], ((0, 0), (0, 0), (0, LANES - C_V))).reshape(r2, C_HEADS * LANES)
    return q.astype(BF16), jnp.concatenate([k, v], axis=1).astype(BF16)


def _rope_tables(s):
    pos = jnp.arange(s, dtype=F32)[:, None]
    lane = np.arange(LANES)
    inv_h = 1.0 / (ROPE_THETA ** (jnp.arange(0, HEAD_DIM, 2, dtype=F32) / HEAD_DIM))
    ang = pos * inv_h[None, :]
    cos, sin = jnp.cos(ang), jnp.sin(ang)
    j = lane % 32
    lo = jnp.asarray((lane % 64) < 32)
    ch = cos[:, j]
    s1h = jnp.where(lo, -sin[:, j], 0.0)
    s2h = jnp.where(lo, 0.0, sin[:, j])
    inv_r = 1.0 / (ROPE_THETA ** (jnp.arange(0, C_ROPE, 2, dtype=F32) / C_ROPE))
    angr = pos * inv_r[None, :]
    cosr, sinr = jnp.cos(angr), jnp.sin(angr)
    jr = lane % 16
    in_rope = jnp.asarray((lane >= 64) & (lane < 96))
    first = jnp.asarray((lane >= 64) & (lane < 80))
    second = jnp.asarray((lane >= 80) & (lane < 96))
    cr = jnp.where(in_rope, cosr[:, jr], 1.0)
    s1r = jnp.where(first, -sinr[:, jr], 0.0)
    s2r = jnp.where(second, sinr[:, jr], 0.0)
    return ch, s1h, s2h, cr, s1r, s2r


def _inproj(x2, w_p, w_t, tables, gq, gkv, wuq_p, wukv_p, s, tm):
    n, d = x2.shape
    nt = n // tm
    spt = s // tm
    cpt = tm // ATT_KC
    tab_spec = pl.BlockSpec((tm, LANES), lambda t: (t % spt, 0))
    full = lambda a: pl.BlockSpec(a.shape, lambda t: (0,) * a.ndim)
    row = lambda c: pl.BlockSpec((tm, c), lambda t: (t, 0))
    cols_a, cols_c = [512, 512, 512, 128, 128, 512], [512, 512, 512]
    out_specs = ([row(c) for c in cols_a]
                 + [pl.BlockSpec((A_KV_HEADS, LANES, tm), lambda t: (0, 0, t)),
                    pl.BlockSpec((cpt, LANES, ATT_KC), lambda t: (t, 0, 0)), pl.BlockSpec((8, tm), lambda t: (0, t))]
                 + [row(c) for c in cols_c])
    out_shape = ([jax.ShapeDtypeStruct((n, c), BF16) for c in cols_a]
                 + [jax.ShapeDtypeStruct((A_KV_HEADS, LANES, n), BF16),
                    jax.ShapeDtypeStruct((n // ATT_KC, LANES, ATT_KC), BF16), jax.ShapeDtypeStruct((8, n), F32)]
                 + [jax.ShapeDtypeStruct((n, c), BF16) for c in cols_c])
    return pl.pallas_call(
        functools.partial(_inproj_kernel, b_scale=HEAD_DIM ** -0.5 * LOG2E,
                          c_scale=(C_NOPE + C_ROPE) ** -0.5 * LOG2E),
        grid=(nt,),
        in_specs=[row(d), full(w_p)] + [tab_spec] * 6 + [full(gq), full(gkv), full(wuq_p), full(wukv_p), full(w_t)],
        out_specs=out_specs,
        out_shape=out_shape,
        compiler_params=pltpu.CompilerParams(dimension_semantics=("parallel",), vmem_limit_bytes=VMEM_LIMIT),
    )(x2, w_p, *tables, gq, gkv, wuq_p, wukv_p, w_t)


def _swa_kernel(sink_ref, q_ref, kc_ref, kp_ref, vc_ref, vp_ref, o_ref):
    i = pl.program_id(1)
    w = A_WINDOW
    qb = q_ref.shape[0]
    kj = lax.broadcasted_iota(jnp.int32, (2 * w, w), 0)
    qi = lax.broadcasted_iota(jnp.int32, (2 * w, w), 1)
    diff = qi + w - kj
    in_window = (diff >= 0) & (diff < w)
    groups = [slice(g * SUBLANES, (g + 1) * SUBLANES) for g in range(2 * w // SUBLANES)]
    lane_head = lax.broadcasted_iota(jnp.int32, (1, 4 * w), 1) // w
    per_group = A_HEADS // A_KV_HEADS
    tasks = [(win, g) for win in range(qb // w) for g in range(A_KV_HEADS)]
    head_order = lambda g: [per_group * g, per_group * g + 2, per_group * g + 1, per_group * g + 3]

    def keys(win, unit):
        cols = slice(unit * LANES, (unit + 1) * LANES)
        if win == 0:
            return jnp.concatenate([kp_ref[:, cols], kc_ref[0:w, cols]], axis=0)
        return kc_ref[(win - 1) * w:(win + 1) * w, cols]

    def values_t(win, g):
        if win == 0:
            return jnp.concatenate([vp_ref[g], vc_ref[g, :, 0:w]], axis=1)
        return vc_ref[g, :, (win - 1) * w:(win + 1) * w]

    scores = []
    for win, g in tasks:
        rows = slice(win * w, (win + 1) * w)
        ok = in_window if win > 0 else in_window & ((kj >= w) | (i > 0))
        bias = jnp.concatenate([jnp.where(ok, 0.0, NEG_BIG)] * per_group, axis=1)
        q2 = jnp.concatenate([q_ref[rows, (2 * g + u) * LANES:(2 * g + u + 1) * LANES] for u in range(2)], axis=0)
        scores.append(jnp.concatenate([_dot_nt(keys(win, 2 * g), q2), _dot_nt(keys(win, 2 * g + 1), q2)], axis=1)
                      + bias)
    probs = []
    for (win, g), s in zip(tasks, scores):
        sink = jnp.zeros((1, per_group * w), F32)
        for slot, h in enumerate(head_order(g)):
            sink = jnp.where(lane_head == slot, sink_ref[h] * LOG2E, sink)
        parts = [s[gg] for gg in groups]
        while len(parts) > 1:
            parts = [jnp.maximum(parts[j], parts[j + 1]) for j in range(0, len(parts), 2)]
        m = jnp.maximum(jnp.max(parts[0], axis=0, keepdims=True), sink)
        probs.append((jnp.exp2(s - m).astype(BF16), jnp.exp2(sink - m)))
    outs = [_dot(values_t(win, g), p) for (win, g), (p, _) in zip(tasks, probs)]
    for (win, g), o, (_, sink_p) in zip(tasks, outs, probs):
        rows = slice(win * w, (win + 1) * w)
        o = (o / (o[ONES_AT:ONES_AT + 1, :] + sink_p)).T
        for slot, h in enumerate(head_order(g)):
            o_ref[rows, h * HEAD_DIM:(h + 1) * HEAD_DIM] = o[slot * w:(slot + 1) * w, :HEAD_DIM].astype(BF16)


def _swa(sinks, aq, ak, avt, b, s):
    n = aq.shape[0]
    w = A_WINDOW
    qb = _att_qb(s)
    nb = s // qb
    wpb = qb // w
    cur = lambda bb, i: (bb * nb + i, 0)
    prev = lambda bb, i: ((bb * nb + i) * wpb - jnp.minimum(i, 1), 0)
    cur_t = lambda bb, i: (0, 0, bb * nb + i)
    prev_t = lambda bb, i: (0, 0, (bb * nb + i) * wpb - jnp.minimum(i, 1))
    return pl.pallas_call(
        _swa_kernel,
        grid=(b, nb),
        in_specs=[pl.BlockSpec(memory_space=pltpu.SMEM),
                  pl.BlockSpec((qb, 512), cur),
                  pl.BlockSpec((qb, 512), cur), pl.BlockSpec((w, 512), prev),
                  pl.BlockSpec((A_KV_HEADS, LANES, qb), cur_t), pl.BlockSpec((A_KV_HEADS, LANES, w), prev_t)],
        out_specs=pl.BlockSpec((qb, 512), cur),
        out_shape=jax.ShapeDtypeStruct((n, 512), BF16),
        compiler_params=pltpu.CompilerParams(dimension_semantics=("parallel", "parallel"),
                                             vmem_limit_bytes=VMEM_LIMIT),
    )(sinks, aq, ak, ak, avt, avt)


ATT_KC = 256
COARSE_CHAINS = 2


def _att_qb(s):
    return min(512, s)


def _causal_key_block(n_keys, s, width):
    rows = n_keys if s % n_keys == 0 else s
    return pl.BlockSpec((rows, width), lambda bb: (bb * (s // rows), 0))


def _fold_max(s):
    return jnp.maximum(s[:, :LANES], s[:, LANES:])


def _unrolled(n, body, carry):
    for c in range(n):
        carry = body(c, carry)
    return carry


def _dsa_kernel(bq_ref, iq_ref, iwt_ref, bk_ref, ik_ref, vt_ref, _aliased_out_ref, o_ref,
                sc_ref, sc16_ref, s_ref, tau_ref, j_ref, ist_ref, qst_ref, m_ref, *, blk, topk, idx_bits):
    i = blk
    qb = bq_ref.shape[0]
    kc = ATT_KC
    per = qb // kc
    n_full = per * i
    n_chunks = n_full + per
    kidx = lax.broadcasted_iota(jnp.int32, (kc, qb), 0)
    rpos = lax.broadcasted_iota(jnp.int32, (kc, qb), 1)
    neg_inf = float("-inf")
    kstart = lambda c: c * kc
    heads = range(B_HEADS)
    hrows = lambda h: slice(h * qb, (h + 1) * qb)
    groups = [slice(g * SUBLANES, (g + 1) * SUBLANES) for g in range(kc // SUBLANES)]

    def fold(x, op, ways=4):
        parts = [x[g] for g in groups[:ways]]
        for j, g in enumerate(groups[ways:]):
            parts[j % ways] = op(parts[j % ways], x[g])
        while len(parts) > 1:
            parts = [op(parts[j], parts[j + 1]) for j in range(0, len(parts), 2)]
        return parts[0]

    for h in range(IDX_HEADS):
        ist_ref[hrows(h), :] = iq_ref[:, h * LANES:(h + 1) * LANES]

    def score(c):
        lg = _dot_nt(ik_ref[pl.ds(kstart(c), kc), :], ist_ref[...])
        acc = None
        for h in range(IDX_HEADS):
            t = iwt_ref[h:h + 1, :] * jnp.maximum(lg[:, hrows(h)], 0.0)
            acc = t if acc is None else acc + t
        return acc

    rw = qb // per

    def put_scores(c, sc):
        for r in range(per):
            sc_ref[c, r] = sc[:, r * rw:(r + 1) * rw]
            sc16_ref[c, r] = sc[:, r * rw:(r + 1) * rw].astype(BF16)

    def idx_body(c, carry):
        put_scores(c, score(c))
        return carry

    _unrolled(n_full, idx_body, 0)
    for d in range(per):
        put_scores(n_full + d, jnp.where(d * kc + kidx <= rpos, score(n_full + d), neg_inf))

    kidx_r = lax.broadcasted_iota(jnp.int32, (kc, rw), 0)
    for r in range(per):
        rows = slice(r * rw, (r + 1) * rw)
        tpos = i * qb + r * rw + lax.broadcasted_iota(jnp.int32, (1, rw), 1)
        n_r = n_full + r + 1
        if i * qb + (r + 1) * rw <= int(topk):
            tau_ref[:, rows] = jnp.full((SUBLANES, rw), neg_inf, F32)
            j_ref[:, rows] = jnp.broadcast_to(tpos, (SUBLANES, rw))
            continue

        def count(pred_fn):
            def cbody(c, acc):
                k = sc_ref[c, r]
                return acc + fold(jnp.where(pred_fn(k, c), 1.0, 0.0), jnp.add, ways=1)
            acc = _unrolled(n_r, cbody, jnp.zeros((SUBLANES, rw), F32))
            return jnp.sum(acc, axis=0, keepdims=True)

        def key_to_f32(key_u):
            key = key_u ^ INT_MIN
            return pltpu.bitcast(key ^ ((key >> 31) & 0x7FFFFFFF), F32)

        def count16(cand16):
            def cbody(c, acc):
                k = sc16_ref[c, r]
                hit = jnp.where(k >= cand16, jnp.ones_like(k), jnp.zeros_like(k))
                acc = list(acc)
                for g in range(kc // PACKED_ROWS):
                    acc[g % COARSE_CHAINS] = acc[g % COARSE_CHAINS] + hit[g * PACKED_ROWS:(g + 1) * PACKED_ROWS]
                return tuple(acc)
            zero = jnp.zeros((PACKED_ROWS, rw), BF16)
            acc = _unrolled(n_r, cbody, (zero,) * COARSE_CHAINS)
            total = acc[0].astype(F32)
            for part in acc[1:]:
                total = total + part.astype(F32)
            return jnp.sum(total, axis=0, keepdims=True)

        def coarse_body(step, prefix):
            cand_u = prefix | lax.shift_left(jnp.int32(1), 31 - step)
            key = cand_u ^ INT_MIN
            cand = pltpu.bitcast(key ^ ((key >> 31) & 0x7FFF0000), F32).astype(BF16)
            return jnp.where(count16(cand) >= topk, cand_u, prefix)

        coarse = lax.fori_loop(0, 16, coarse_body, jnp.zeros((1, rw), jnp.int32))
        step16 = 1 << 16
        key_neg_inf = INT_MIN + 0x7FFFFF
        key_pos_inf = 0x7F800000
        key16 = coarse ^ INT_MIN
        key16 = key16 + ((key16 >> 31) & 0xFFFF)
        base = jnp.clip(key16, key_neg_inf + step16, key_pos_inf) - step16

        def fine_body(step, off):
            cand_off = off | lax.shift_left(jnp.int32(1), 16 - step)
            cand = key_to_f32((base + cand_off) ^ INT_MIN)
            cnt = count(lambda k, c: k >= cand)
            return jnp.where(cnt >= topk, cand_off, off)

        off = lax.fori_loop(0, 17, fine_body, jnp.zeros((1, rw), jnp.int32))
        tau = key_to_f32((base + off) ^ INT_MIN)
        tau = jnp.where(tau != tau, neg_inf, tau)
        c_gt = count(lambda k, c: k > tau)
        c_eq = count(lambda k, c: k == tau)
        need = topk - c_gt

        def tie_search(_):
            def jbody(step, q):
                cand = q | lax.shift_left(jnp.int32(1), idx_bits - 1 - step)
                cnt = count(lambda k, c: (k == tau) & (c * kc + kidx_r < cand))
                return jnp.where(cnt < need, cand, q)
            return lax.fori_loop(0, idx_bits, jbody, jnp.zeros((1, rw), jnp.int32))

        any_split = jnp.max(c_eq - need) > 0.0
        jcut = lax.cond(any_split, tie_search, lambda _: jnp.full((1, rw), 2 ** idx_bits, jnp.int32), 0)
        few = tpos < int(topk)
        tau_ref[:, rows] = jnp.broadcast_to(jnp.where(few, neg_inf, tau), (SUBLANES, rw))
        j_ref[:, rows] = jnp.broadcast_to(jnp.where(few, tpos, jcut), (SUBLANES, rw))

    for h in heads:
        qst_ref[hrows(h), :] = bq_ref[:, h * LANES:(h + 1) * LANES]
    m_ref[...] = jnp.full(m_ref.shape, NEG_BIG, F32)

    def max_body(c, carry):
        s = _dot_nt(bk_ref[pl.ds(kstart(c), kc), :], qst_ref[...])
        k = jnp.concatenate([sc_ref[c, r] for r in range(per)], axis=1)
        tau = tau_ref[0:1, :]
        sel = (k > tau) | ((k == tau) & (c * kc + kidx <= j_ref[0:1, :]))
        bias = jnp.where(sel, 0.0, NEG_BIG)
        for h in heads:
            sm = s[:, hrows(h)] + bias
            s_ref[c, :, hrows(h)] = sm
            m_ref[:, hrows(h)] = jnp.maximum(m_ref[:, hrows(h)], fold(sm, jnp.maximum, ways=1))
        return carry

    _unrolled(n_chunks, max_body, 0)
    m_ref[...] = jnp.broadcast_to(jnp.max(m_ref[...], axis=0, keepdims=True), m_ref.shape)

    probs = [jnp.exp2(s_ref[c] - m_ref[0:1, :]).astype(BF16) for c in range(n_chunks)]
    out_t = _dot(jnp.concatenate([vt_ref[c] for c in range(n_chunks)], axis=1), jnp.concatenate(probs, axis=0))
    for h in heads:
        acc = out_t[:, hrows(h)]
        o_ref[:, h * HEAD_DIM:(h + 1) * HEAD_DIM] = (acc / acc[ONES_AT:ONES_AT + 1, :]).T[:, :HEAD_DIM].astype(BF16)


def _dsa(bq, iq, iwt, bk, ik, vt, b, s):
    n = bq.shape[0]
    qb = _att_qb(s)
    nb = s // qb
    nkc = s // ATT_KC
    topk = min(IDX_TOPK_MAX, s // 4)
    idx_bits = max(1, int(math.ceil(math.log2(s))))
    seq = lambda bb: (bb, 0)
    out = jnp.zeros((n, 256), BF16)
    for i in range(nb):
        blk = lambda bb, i=i: (bb * nb + i, 0)
        n_kc = (i + 1) * (qb // ATT_KC)
        out = pl.pallas_call(
            functools.partial(_dsa_kernel, blk=i, topk=float(topk), idx_bits=idx_bits),
            grid=(b,),
            in_specs=[pl.BlockSpec((qb, 512), blk), pl.BlockSpec((qb, 512), blk),
                      pl.BlockSpec((8, qb), lambda bb, i=i: (0, bb * nb + i)),
                      pl.BlockSpec((s, 128), seq), pl.BlockSpec((s, 128), seq),
                      pl.BlockSpec((nkc, LANES, ATT_KC), lambda bb: (bb, 0, 0)),
                      pl.BlockSpec(memory_space=pl.ANY)],
            out_specs=pl.BlockSpec((qb, 256), blk),
            out_shape=jax.ShapeDtypeStruct((n, 256), BF16),
            input_output_aliases={6: 0},
            scratch_shapes=[pltpu.VMEM((n_kc, qb // ATT_KC, ATT_KC, ATT_KC), F32),
                            pltpu.VMEM((n_kc, qb // ATT_KC, ATT_KC, ATT_KC), BF16),
                            pltpu.VMEM((n_kc, ATT_KC, B_HEADS * qb), F32),
                            pltpu.VMEM((SUBLANES, qb), F32), pltpu.VMEM((SUBLANES, qb), jnp.int32),
                            pltpu.VMEM((IDX_HEADS * qb, LANES), BF16), pltpu.VMEM((B_HEADS * qb, LANES), BF16),
                            pltpu.VMEM((SUBLANES, B_HEADS * qb), F32)],
            compiler_params=pltpu.CompilerParams(dimension_semantics=("parallel",), vmem_limit_bytes=VMEM_LIMIT),
        )(bq, iq, iwt, bk, ik, vt, out)
    return out


def _mla_kernel(q_ref, k_ref, v_ref, _aliased_out_ref, o_ref, s_ref, m_ref, *, blk):
    qb = q_ref.shape[0]
    kc = ATT_KC
    per = qb // kc
    n_full = per * blk
    n_chunks = n_full + per
    row = lax.broadcasted_iota(jnp.int32, (qb, kc), 0)
    col = lax.broadcasted_iota(jnp.int32, (qb, kc), 1)
    heads = range(C_HEADS)
    hs = lambda h: slice(h * LANES, (h + 1) * LANES)
    keys = lambda c: slice(c * kc, (c + 1) * kc)
    n_halves = kc // LANES
    m_ref[...] = jnp.full(m_ref.shape, NEG_BIG, F32)

    for c in range(n_chunks):
        diag = c - n_full
        for h in heads:
            s = _dot_nt(q_ref[:, hs(h)], k_ref[keys(c), hs(h)])
            if diag >= 0:
                s = s + jnp.where(diag * kc + col <= row, 0.0, NEG_BIG)
            s_ref[h, c] = s
            m_ref[h] = jnp.maximum(m_ref[h], _fold_max(s))
    for h in heads:
        m_ref[h] = jnp.broadcast_to(jnp.max(m_ref[h], axis=1, keepdims=True), (qb, LANES))
    for h in heads:
        m = jnp.concatenate([m_ref[h]] * n_halves, axis=1)
        p = jnp.concatenate([jnp.exp2(s_ref[h, c] - m).astype(BF16) for c in range(n_chunks)], axis=1)
        acc = _dot(p, v_ref[0:n_chunks * kc, hs(h)])
        o_ref[:, h * C_V:(h + 1) * C_V] = (acc[:, :C_V] / acc[:, ONES_AT:ONES_AT + 1]).astype(BF16)


def _mla(qc, kc, vc, b, s):
    n = qc.shape[0]
    qb = _att_qb(s)
    nb = s // qb
    seq = lambda bb: (bb, 0)
    out = jnp.zeros((n, 256), BF16)
    for i in range(nb):
        blk = lambda bb, i=i: (bb * nb + i, 0)
        n_kc = (i + 1) * (qb // ATT_KC)
        out = pl.pallas_call(
            functools.partial(_mla_kernel, blk=i),
            grid=(b,),
            in_specs=[pl.BlockSpec((qb, 512), blk),
                      _causal_key_block(n_kc * ATT_KC, s, 512), _causal_key_block(n_kc * ATT_KC, s, 512),
                      pl.BlockSpec(memory_space=pl.ANY)],
            out_specs=pl.BlockSpec((qb, 256), blk),
            out_shape=jax.ShapeDtypeStruct((n, 256), BF16),
            input_output_aliases={3: 0},
            scratch_shapes=[pltpu.VMEM((C_HEADS, n_kc, qb, ATT_KC), F32), pltpu.VMEM((C_HEADS, qb, LANES), F32)],
            compiler_params=pltpu.CompilerParams(dimension_semantics=("parallel",), vmem_limit_bytes=VMEM_LIMIT),
        )(qc, kc, vc, out)
    return out


def _layer_norm(z, g, b):
    mu = jnp.mean(z, axis=1, keepdims=True)
    zc = z - mu
    var = jnp.mean(zc * zc, axis=1, keepdims=True)
    return zc * lax.rsqrt(var + LN_EPS) * g + b


def _route(logits, rbias):
    scores = jax.nn.sigmoid(logits)
    biased = scores + rbias
    rows = [biased[e:e + 1, :] for e in range(N_EXPERTS)]
    gscore = []
    for g in range(N_GROUPS):
        v = rows[g * 4:(g + 1) * 4]
        best = None
        for a in range(4):
            for c in range(a + 1, 4):
                pair = v[a] + v[c]
                best = pair if best is None else jnp.maximum(best, pair)
        gscore.append(best)
    gmax = jnp.maximum(jnp.maximum(gscore[0], gscore[1]), jnp.maximum(gscore[2], gscore[3]))
    taken = jnp.zeros_like(gmax) > 1.0
    sel_rows = []
    for g in range(N_GROUPS):
        g_sel = (gscore[g] == gmax) & jnp.logical_not(taken)
        taken = taken | g_sel
        v = rows[g * 4:(g + 1) * 4]
        for e in range(4):
            rank = jnp.zeros_like(gmax)
            for j in range(4):
                if j == e:
                    continue
                ahead = (v[j] > v[e]) | ((v[j] == v[e]) & (j < e))
                rank = rank + jnp.where(ahead, 1.0, 0.0)
            sel_rows.append(g_sel & (rank < 2.0))
    wsel = [jnp.where(sel_rows[e], scores[e:e + 1, :], 0.0) for e in range(N_EXPERTS)]
    total = wsel[0]
    for e in range(1, N_EXPERTS):
        total = total + wsel[e]
    gates_t = jnp.concatenate(wsel + [jnp.zeros((LANES - N_EXPERTS, total.shape[1]), F32)], axis=0) / total
    return gates_t.T


OUT_SLAB = 256


def _outproj_kernel(x_ref, oa_ref, ob_ref, oc_ref, wo_ref, g_ref, b_ref, wr_ref, rb_ref, x1_ref, gates_ref,
                    *, alpha):
    tm = x_ref.shape[0]
    slabs = [slice(r, r + OUT_SLAB) for r in range(0, tm, OUT_SLAB)]
    ys = [_dot(oa_ref[rows, :], wo_ref[0:512, :]) + _dot(ob_ref[rows, :], wo_ref[512:768, :])
          + _dot(oc_ref[rows, :], wo_ref[768:1024, :]) for rows in slabs]
    x1s = [_layer_norm(alpha * x_ref[rows, :] + y, g_ref[...], b_ref[...]) for rows, y in zip(slabs, ys)]
    for rows, x1 in zip(slabs, x1s):
        x1_ref[rows, :] = x1

    wr = wr_ref[...]
    wrh = wr.astype(BF16)
    wrl = (wr - wrh.astype(F32)).astype(BF16)
    wr2 = jnp.concatenate([wrh, wrl], axis=0)
    for rows, x1 in zip(slabs, x1s):
        x1h = x1.astype(BF16)
        x1l = (x1 - x1h.astype(F32)).astype(BF16)
        hi = _dot_nt(wr2, x1h)
        gates = _route(hi[:N_EXPERTS] + hi[N_EXPERTS:] + _dot_nt(wrh, x1l), rb_ref[...])
        for g in range(N_GROUPS):
            gates_ref[g, rows, :] = gates[:, g * EXPERTS_PER_GROUP:(g + 1) * EXPERTS_PER_GROUP]


def _outproj(x2, oa, ob, oc, wo, g, bb, wr_t, rbias, alpha, tm):
    n, d = x2.shape
    nt = n // tm
    full = lambda a: pl.BlockSpec(a.shape, lambda t: (0,) * a.ndim)
    row = lambda c: pl.BlockSpec((tm, c), lambda t: (t, 0))
    return pl.pallas_call(
        functools.partial(_outproj_kernel, alpha=alpha),
        grid=(nt,),
        in_specs=[row(d), row(512), row(256), row(256), full(wo), full(g), full(bb), full(wr_t), full(rbias)],
        out_specs=[row(d), pl.BlockSpec((N_GROUPS, tm, EXPERTS_PER_GROUP), lambda t: (0, t, 0))],
        out_shape=[jax.ShapeDtypeStruct((n, d), F32),
                   jax.ShapeDtypeStruct((N_GROUPS, n, EXPERTS_PER_GROUP), F32)],
        compiler_params=pltpu.CompilerParams(dimension_semantics=("parallel",), vmem_limit_bytes=VMEM_LIMIT),
    )(x2, oa, ob, oc, wo, g, bb, wr_t, rbias)


def _moe_kernel(x_ref, gates_ref, wg_ref, wu_ref, wd_ref, g_ref, b_ref, o_ref, acc_ref, *, alpha):
    grp = pl.program_id(1)

    @pl.when(grp == 0)
    def _():
        acc_ref[...] = jnp.zeros(acc_ref.shape, F32)

    xb = x_ref[...].astype(BF16)
    hs = []
    for j in range(EXPERTS_PER_GROUP):
        hg = _dot(xb, wg_ref[j])
        hu = _dot(xb, wu_ref[j])
        hs.append((hg * jax.nn.sigmoid(hg) * hu * gates_ref[0, :, j:j + 1]).astype(BF16))
    wd = wd_ref[...]
    acc_ref[...] += _dot(jnp.concatenate(hs, axis=1), wd.reshape(wd.shape[0] * wd.shape[1], wd.shape[2]))

    @pl.when(grp == N_GROUPS - 1)
    def _():
        o_ref[...] = _layer_norm(alpha * x_ref[...] + acc_ref[...], g_ref[...], b_ref[...])


def _moe(x1, gates_g, wg, wu, wd, g, bb, alpha, tm):
    n, d = x1.shape
    nt = n // tm
    f = wg.shape[2]
    epg = EXPERTS_PER_GROUP
    full = lambda a: pl.BlockSpec(a.shape, lambda t, e: (0,) * a.ndim)
    return pl.pallas_call(
        functools.partial(_moe_kernel, alpha=alpha),
        grid=(nt, N_GROUPS),
        in_specs=[pl.BlockSpec((tm, d), lambda t, e: (t, 0)), pl.BlockSpec((1, tm, epg), lambda t, e: (e, t, 0)),
                  pl.BlockSpec((epg, d, f), lambda t, e: (e, 0, 0)), pl.BlockSpec((epg, d, f), lambda t, e: (e, 0, 0)),
                  pl.BlockSpec((epg, f, d), lambda t, e: (e, 0, 0)), full(g), full(bb)],
        out_specs=pl.BlockSpec((tm, d), lambda t, e: (t, 0)),
        out_shape=jax.ShapeDtypeStruct((n, d), F32),
        scratch_shapes=[pltpu.VMEM((tm, d), F32)],
        compiler_params=pltpu.CompilerParams(dimension_semantics=("parallel", "arbitrary"),
                                             vmem_limit_bytes=VMEM_LIMIT),
    )(x1, gates_g, wg, wu, wd, g, bb)


def kernel(x, w_in, attn_sinks, c_q_norm_g, c_kv_norm_g, w_uq, w_ukv, w_out, ln1_g, ln1_b, w_router, router_bias,
           w_gate, w_up, w_down, ln2_g, ln2_b):
    b, s, d = x.shape
    depth = w_in.shape[0]
    n = b * s
    alpha = (2 * depth) ** 0.25
    tm = min(512, s)
    tables = _rope_tables(s)
    wr_t = w_router.T
    rbias = router_bias.reshape(N_EXPERTS, 1)
    x2 = x.reshape(n, d)
    for l in range(depth):
        w_p, w_t = _pack_w_in(w_in[l])
        wuq_p, wukv_p = _pack_mla_w(w_uq[l], w_ukv[l])
        aq, ak, bq, bk, ik, iq, avt, vt, iwt, qc, kc, vc = _inproj(
            x2, w_p, w_t, tables, c_q_norm_g[l].reshape(1, -1), c_kv_norm_g[l].reshape(1, -1), wuq_p, wukv_p, s, tm)
        oa = _swa(attn_sinks[l], aq, ak, avt, b, s)
        ob = _dsa(bq, iq, iwt, bk, ik, vt, b, s)
        oc = _mla(qc, kc, vc, b, s)
        x1, gates_g = _outproj(x2, oa, ob, oc, w_out[l].astype(BF16), ln1_g[l].reshape(1, d), ln1_b[l].reshape(1, d),
                               wr_t, rbias, alpha, min(4 * OUT_SLAB, n))
        x2 = _moe(x1, gates_g, w_gate[l].astype(BF16), w_up[l].astype(BF16), w_down[l].astype(BF16),
                  ln2_g[l].reshape(1, d), ln2_b[l].reshape(1, d), alpha, min(1024, n))
    return x2.reshape(b, s, d)
```

```python
import functools
import math

import jax
import jax.numpy as jnp
import numpy as np
from jax import lax
from jax.experimental import pallas as pl
from jax.experimental.pallas import tpu as pltpu

HEAD_DIM = 64
ROPE_THETA = 10000.0
A_HEADS = 8
A_KV_HEADS = 2
A_WINDOW = 128
B_HEADS = 4
IDX_HEADS = 4
IDX_DIM = 64
IDX_TOPK_MAX = 256
C_HEADS = 4
C_NOPE = 64
C_ROPE = 32
C_V = 64
N_EXPERTS = 16
N_GROUPS = 4
EXPERTS_PER_GROUP = 4
LN_EPS = 1e-5
RMS_EPS = 1e-6

LANES = 128
SUBLANES = 8
PACKED_ROWS = 16
ONES_AT = 64
NEG_BIG = -1e30
LOG2E = math.log2(math.e)
INT_MIN = -(2 ** 31)
VMEM_LIMIT = 56 * 1024 * 1024

BF16 = jnp.bfloat16
F32 = jnp.float32

_NT = (((1,), (1,)), ((), ()))


def _dot(a, b):
    return jnp.dot(a, b, preferred_element_type=F32)


def _dot_nt(a, b):
    return lax.dot_general(a, b, _NT, preferred_element_type=F32)


N_ROPE_UNITS = 10


def _inproj_kernel(x_ref, w_ref, ch_ref, s1h_ref, s2h_ref, cr_ref, s1r_ref, s2r_ref, gq_ref, gkv_ref,
                   wuq_ref, wukv_ref, wt_ref,
                   aq_ref, ak_ref, bq_ref, bk_ref, ik_ref, iq_ref, avt_ref, vt_ref, iwt_ref, qc_ref, kc_ref, vc_ref,
                   *, b_scale, c_scale):
    xb = x_ref[...].astype(BF16)
    tm = xb.shape[0]
    lane = lax.broadcasted_iota(jnp.int32, (tm, LANES), 1)
    ch, s1h, s2h = ch_ref[...], s1h_ref[...], s2h_ref[...]
    cr, s1r, s2r = cr_ref[...], s1r_ref[...], s2r_ref[...]
    low = lane < 64

    def rope_h(u):
        return u * ch + pltpu.roll(u, 96, 1) * s1h + pltpu.roll(u, 32, 1) * s2h

    def rope_r(u):
        return u * cr + pltpu.roll(u, 112, 1) * s1r + pltpu.roll(u, 16, 1) * s2r

    def split(u):
        return jnp.where(low, u, 0.0), jnp.where(low, pltpu.roll(u, 64, 1), 0.0)

    def put(ref, k, u):
        ref[:, k * LANES:(k + 1) * LANES] = u.astype(BF16)

    projected = [_dot(xb, w_ref[:, g * 256:(g + 1) * 256]) for g in range(N_ROPE_UNITS // 2)]
    for g, hg in enumerate(projected):
        for half in range(2):
            unit = 2 * g + half
            u = rope_h(hg[:, half * LANES:(half + 1) * LANES])
            if unit < 4:
                put(aq_ref, unit, u * LOG2E)
            elif unit == 4:
                swapped = pltpu.roll(u, 64, 1)
                put(ak_ref, 0, jnp.where(low, u, 0.0))
                put(ak_ref, 1, jnp.where(low, 0.0, swapped))
                put(ak_ref, 2, jnp.where(low, swapped, 0.0))
                put(ak_ref, 3, jnp.where(low, 0.0, u))
            elif unit < 7:
                h0, h1 = split(u * b_scale)
                put(bq_ref, 2 * (unit - 5), h0)
                put(bq_ref, 2 * (unit - 5) + 1, h1)
            elif unit == 7:
                h0, h1 = split(u)
                put(bk_ref, 0, h0)
                put(ik_ref, 0, h1)
            else:
                h0, h1 = split(u)
                put(iq_ref, 2 * (unit - 8), h0)
                put(iq_ref, 2 * (unit - 8) + 1, h1)

    hb = _dot_nt(wt_ref[...], xb)
    sub = lax.broadcasted_iota(jnp.int32, (LANES, tm), 0)
    with_ones = lambda piece: jnp.where(sub == ONES_AT, 1.0, hb[piece * LANES:(piece + 1) * LANES]).astype(BF16)
    vt = with_ones(0)
    for j in range(tm // ATT_KC):
        vt_ref[j] = vt[:, j * ATT_KC:(j + 1) * ATT_KC]
    iwt_ref[...] = hb[LANES:LANES + 8]
    for g in range(A_KV_HEADS):
        avt_ref[g] = with_ones(2 + g)

    cq = _dot(xb, w_ref[:, 1280:1536])
    cqn = cq * lax.rsqrt(jnp.mean(cq * cq, axis=1, keepdims=True) + RMS_EPS) * gq_ref[...]
    qc = _dot(cqn.astype(BF16), wuq_ref[...])
    for h in range(C_HEADS):
        u = qc[:, h * LANES:(h + 1) * LANES]
        qc_ref[:, h * LANES:(h + 1) * LANES] = (rope_r(u) * c_scale).astype(BF16)

    ckv = _dot(xb, w_ref[:, 1536:1792])
    ckv_lat = ckv[:, :LANES]
    kr = rope_r(ckv[:, LANES:])
    ckvn = ckv_lat * lax.rsqrt(jnp.mean(ckv_lat * ckv_lat, axis=1, keepdims=True) + RMS_EPS) * gkv_ref[...]
    kv = _dot(ckvn.astype(BF16), wukv_ref[...])
    for h in range(C_HEADS):
        kc_ref[:, h * LANES:(h + 1) * LANES] = (kv[:, h * LANES:(h + 1) * LANES] + kr).astype(BF16)
        v = kv[:, (C_HEADS + h) * LANES:(C_HEADS + h + 1) * LANES]
        vc_ref[:, h * LANES:(h + 1) * LANES] = jnp.where(lane == ONES_AT, 1.0, v).astype(BF16)


def _pack_w_in(w):
    d = w.shape[0]
    z = lambda n: jnp.zeros((d, n), F32)
    o = np.cumsum([0, 512, 128, 128, 256, 64, 64, 256, 64, 4, 256, 128, 32]).tolist()
    a_q, a_k, a_v, b_q, b_k, b_v, i_q, i_k, i_w, c_q, c_kv, c_kr = [w[:, o[j]:o[j + 1]] for j in range(12)]
    qs = HEAD_DIM ** -0.5
    ws = (IDX_HEADS * IDX_DIM) ** -0.5
    cols = [a_q * qs, a_k, b_q, b_k, i_k, i_q,
            c_q, c_kv, z(64), c_kr, z(32)]
    rows_t = jnp.concatenate([b_v, z(64), i_w * ws, z(LANES - IDX_HEADS),
                              a_v[:, :64], z(64), a_v[:, 64:], z(64)], axis=1).T
    return jnp.concatenate(cols, axis=1).astype(BF16), rows_t.astype(BF16)


def _pack_mla_w(w_uq, w_ukv):
    r = w_uq.shape[0]
    q = w_uq.reshape(r, C_HEADS, C_NOPE + C_ROPE)
    q = jnp.pad(q, ((0, 0), (0, 0), (0, LANES - C_NOPE - C_ROPE))).reshape(r, C_HEADS * LANES)
    r2 = w_ukv.shape[0]
    kv = w_ukv.reshape(r2, C_HEADS, C_NOPE + C_V)
    k = jnp.pad(kv[:, :, :C_NOPE], ((0, 0), (0, 0), (0, LANES - C_NOPE))).reshape(r2, C_HEADS * LANES)
    v = jnp.pad(kv[:, :, C_NOPE:], ((0, 0), (0, 0), (0, LANES - C_V))).reshape(r2, C_HEADS * LANES)
    return q.astype(BF16), jnp.concatenate([k, v], axis=1).astype(BF16)


def _rope_tables(s):
    pos = jnp.arange(s, dtype=F32)[:, None]
    lane = np.arange(LANES)
    inv_h = 1.0 / (ROPE_THETA ** (jnp.arange(0, HEAD_DIM, 2, dtype=F32) / HEAD_DIM))
    ang = pos * inv_h[None, :]
    cos, sin = jnp.cos(ang), jnp.sin(ang)
    j = lane % 32
    lo = jnp.asarray((lane % 64) < 32)
    ch = cos[:, j]
    s1h = jnp.where(lo, -sin[:, j], 0.0)
    s2h = jnp.where(lo, 0.0, sin[:, j])
    inv_r = 1.0 / (ROPE_THETA ** (jnp.arange(0, C_ROPE, 2, dtype=F32) / C_ROPE))
    angr = pos * inv_r[None, :]
    cosr, sinr = jnp.cos(angr), jnp.sin(angr)
    jr = lane % 16
    in_rope = jnp.asarray((lane >= 64) & (lane < 96))
    first = jnp.asarray((lane >= 64) & (lane < 80))
    second = jnp.asarray((lane >= 80) & (lane < 96))
    cr = jnp.where(in_rope, cosr[:, jr], 1.0)
    s1r = jnp.where(first, -sinr[:, jr], 0.0)
    s2r = jnp.where(second, sinr[:, jr], 0.0)
    return ch, s1h, s2h, cr, s1r, s2r


def _inproj(x2, w_p, w_t, tables, gq, gkv, wuq_p, wukv_p, s, tm):
    n, d = x2.shape
    nt = n // tm
    spt = s // tm
    cpt = tm // ATT_KC
    tab_spec = pl.BlockSpec((tm, LANES), lambda t: (t % spt, 0))
    full = lambda a: pl.BlockSpec(a.shape, lambda t: (0,) * a.ndim)
    row = lambda c: pl.BlockSpec((tm, c), lambda t: (t, 0))
    cols_a, cols_c = [512, 512, 512, 128, 128, 512], [512, 512, 512]
    out_specs = ([row(c) for c in cols_a]
                 + [pl.BlockSpec((A_KV_HEADS, LANES, tm), lambda t: (0, 0, t)),
                    pl.BlockSpec((cpt, LANES, ATT_KC), lambda t: (t, 0, 0)), pl.BlockSpec((8, tm), lambda t: (0, t))]
                 + [row(c) for c in cols_c])
    out_shape = ([jax.ShapeDtypeStruct((n, c), BF16) for c in cols_a]
                 + [jax.ShapeDtypeStruct((A_KV_HEADS, LANES, n), BF16),
                    jax.ShapeDtypeStruct((n // ATT_KC, LANES, ATT_KC), BF16), jax.ShapeDtypeStruct((8, n), F32)]
                 + [jax.ShapeDtypeStruct((n, c), BF16) for c in cols_c])
    return pl.pallas_call(
        functools.partial(_inproj_kernel, b_scale=HEAD_DIM ** -0.5 * LOG2E,
                          c_scale=(C_NOPE + C_ROPE) ** -0.5 * LOG2E),
        grid=(nt,),
        in_specs=[row(d), full(w_p)] + [tab_spec] * 6 + [full(gq), full(gkv), full(wuq_p), full(wukv_p), full(w_t)],
        out_specs=out_specs,
        out_shape=out_shape,
        compiler_params=pltpu.CompilerParams(dimension_semantics=("parallel",), vmem_limit_bytes=VMEM_LIMIT),
    )(x2, w_p, *tables, gq, gkv, wuq_p, wukv_p, w_t)


SWA_ROWS = 2048


def _swa_kernel(sink_ref, q_ref, kc_ref, kp_ref, vc_ref, vp_ref, o_ref):
    i = pl.program_id(1)
    w = A_WINDOW
    qb = q_ref.shape[0]
    kj = lax.broadcasted_iota(jnp.int32, (2 * w, w), 0)
    qi = lax.broadcasted_iota(jnp.int32, (2 * w, w), 1)
    diff = qi + w - kj
    in_window = (diff >= 0) & (diff < w)
    groups = [slice(g * SUBLANES, (g + 1) * SUBLANES) for g in range(2 * w // SUBLANES)]
    lane_head = lax.broadcasted_iota(jnp.int32, (1, 4 * w), 1) // w
    per_group = A_HEADS // A_KV_HEADS
    tasks = [(win, g) for win in range(qb // w) for g in range(A_KV_HEADS)]
    head_order = lambda g: [per_group * g, per_group * g + 2, per_group * g + 1, per_group * g + 3]

    def keys(win, unit):
        cols = slice(unit * LANES, (unit + 1) * LANES)
        if win == 0:
            return jnp.concatenate([kp_ref[:, cols], kc_ref[0:w, cols]], axis=0)
        return kc_ref[(win - 1) * w:(win + 1) * w, cols]

    def values_t(win, g):
        if win == 0:
            return jnp.concatenate([vp_ref[g], vc_ref[g, :, 0:w]], axis=1)
        return vc_ref[g, :, (win - 1) * w:(win + 1) * w]

    scores = []
    for win, g in tasks:
        rows = slice(win * w, (win + 1) * w)
        ok = in_window if win > 0 else in_window & ((kj >= w) | (i > 0))
        bias = jnp.concatenate([jnp.where(ok, 0.0, NEG_BIG)] * per_group, axis=1)
        q2 = jnp.concatenate([q_ref[rows, (2 * g + u) * LANES:(2 * g + u + 1) * LANES] for u in range(2)], axis=0)
        scores.append(jnp.concatenate([_dot_nt(keys(win, 2 * g), q2), _dot_nt(keys(win, 2 * g + 1), q2)], axis=1)
                      + bias)
    probs = []
    for (win, g), s in zip(tasks, scores):
        sink = jnp.zeros((1, per_group * w), F32)
        for slot, h in enumerate(head_order(g)):
            sink = jnp.where(lane_head == slot, sink_ref[h] * LOG2E, sink)
        parts = [s[gg] for gg in groups]
        while len(parts) > 1:
            parts = [jnp.maximum(parts[j], parts[j + 1]) for j in range(0, len(parts), 2)]
        m = jnp.maximum(jnp.max(parts[0], axis=0, keepdims=True), sink)
        probs.append((jnp.exp2(s - m).astype(BF16), jnp.exp2(sink - m)))
    outs = [_dot(values_t(win, g), p) for (win, g), (p, _) in zip(tasks, probs)]
    for (win, g), o, (_, sink_p) in zip(tasks, outs, probs):
        rows = slice(win * w, (win + 1) * w)
        o = (o / (o[ONES_AT:ONES_AT + 1, :] + sink_p)).T
        for slot, h in enumerate(head_order(g)):
            o_ref[rows, h * HEAD_DIM:(h + 1) * HEAD_DIM] = o[slot * w:(slot + 1) * w, :HEAD_DIM].astype(BF16)


def _swa(sinks, aq, ak, avt, b, s):
    n = aq.shape[0]
    w = A_WINDOW
    qb = min(SWA_ROWS, s)
    nb = s // qb
    wpb = qb // w
    cur = lambda bb, i: (bb * nb + i, 0)
    prev = lambda bb, i: ((bb * nb + i) * wpb - jnp.minimum(i, 1), 0)
    cur_t = lambda bb, i: (0, 0, bb * nb + i)
    prev_t = lambda bb, i: (0, 0, (bb * nb + i) * wpb - jnp.minimum(i, 1))
    return pl.pallas_call(
        _swa_kernel,
        grid=(b, nb),
        in_specs=[pl.BlockSpec(memory_space=pltpu.SMEM),
                  pl.BlockSpec((qb, 512), cur),
                  pl.BlockSpec((qb, 512), cur), pl.BlockSpec((w, 512), prev),
                  pl.BlockSpec((A_KV_HEADS, LANES, qb), cur_t), pl.BlockSpec((A_KV_HEADS, LANES, w), prev_t)],
        out_specs=pl.BlockSpec((qb, 512), cur),
        out_shape=jax.ShapeDtypeStruct((n, 512), BF16),
        compiler_params=pltpu.CompilerParams(dimension_semantics=("parallel", "parallel"),
                                             vmem_limit_bytes=VMEM_LIMIT),
    )(sinks, aq, ak, ak, avt, avt)


ATT_KC = 256
COARSE_CHAINS = 2


def _att_qb(s):
    return min(512, s)


def _causal_key_block(n_keys, s, width):
    rows = n_keys if s % n_keys == 0 else s
    return pl.BlockSpec((rows, width), lambda bb: (bb * (s // rows), 0))


def _fold_max(s):
    return jnp.maximum(s[:, :LANES], s[:, LANES:])


def _unrolled(n, body, carry):
    for c in range(n):
        carry = body(c, carry)
    return carry


def _dsa_kernel(bq_ref, iq_ref, iwt_ref, bk_ref, ik_ref, vt_ref, _aliased_out_ref, o_ref,
                sc_ref, sc16_ref, s_ref, tau_ref, j_ref, ist_ref, qst_ref, m_ref, *, blk, topk, idx_bits):
    i = blk
    qb = bq_ref.shape[0]
    kc = ATT_KC
    per = qb // kc
    n_full = per * i
    n_chunks = n_full + per
    kidx = lax.broadcasted_iota(jnp.int32, (kc, qb), 0)
    rpos = lax.broadcasted_iota(jnp.int32, (kc, qb), 1)
    neg_inf = float("-inf")
    kstart = lambda c: c * kc
    heads = range(B_HEADS)
    hrows = lambda h: slice(h * qb, (h + 1) * qb)
    groups = [slice(g * SUBLANES, (g + 1) * SUBLANES) for g in range(kc // SUBLANES)]

    def fold(x, op, ways=4):
        parts = [x[g] for g in groups[:ways]]
        for j, g in enumerate(groups[ways:]):
            parts[j % ways] = op(parts[j % ways], x[g])
        while len(parts) > 1:
            parts = [op(parts[j], parts[j + 1]) for j in range(0, len(parts), 2)]
        return parts[0]

    for h in range(IDX_HEADS):
        ist_ref[hrows(h), :] = iq_ref[:, h * LANES:(h + 1) * LANES]

    def logits(c):
        return _dot_nt(ik_ref[pl.ds(kstart(c), kc), :], ist_ref[...])

    def combine(lg):
        acc = None
        for h in range(IDX_HEADS):
            t = iwt_ref[h:h + 1, :] * jnp.maximum(lg[:, hrows(h)], 0.0)
            acc = t if acc is None else acc + t
        return acc

    rw = qb // per

    def put_scores(c, sc):
        for r in range(per):
            sc_ref[c, r] = sc[:, r * rw:(r + 1) * rw]
            sc16_ref[c, r] = sc[:, r * rw:(r + 1) * rw].astype(BF16)

    for c0 in range(0, n_chunks, per):
        lgs = [logits(c0 + d) for d in range(per)]
        for d, lg in enumerate(lgs):
            sc = combine(lg)
            if c0 == n_full:
                sc = jnp.where(d * kc + kidx <= rpos, sc, neg_inf)
            put_scores(c0 + d, sc)

    kidx_r = lax.broadcasted_iota(jnp.int32, (kc, rw), 0)
    for r in range(per):
        rows = slice(r * rw, (r + 1) * rw)
        tpos = i * qb + r * rw + lax.broadcasted_iota(jnp.int32, (1, rw), 1)
        n_r = n_full + r + 1
        if i * qb + (r + 1) * rw <= int(topk):
            tau_ref[:, rows] = jnp.full((SUBLANES, rw), neg_inf, F32)
            j_ref[:, rows] = jnp.broadcast_to(tpos, (SUBLANES, rw))
            continue

        def count(pred_fn):
            def cbody(c, acc):
                k = sc_ref[c, r]
                return acc + fold(jnp.where(pred_fn(k, c), 1.0, 0.0), jnp.add, ways=1)
            acc = _unrolled(n_r, cbody, jnp.zeros((SUBLANES, rw), F32))
            return jnp.sum(acc, axis=0, keepdims=True)

        def key_to_f32(key_u):
            key = key_u ^ INT_MIN
            return pltpu.bitcast(key ^ ((key >> 31) & 0x7FFFFFFF), F32)

        def count16(cand16):
            def cbody(c, acc):
                k = sc16_ref[c, r]
                hit = jnp.where(k >= cand16, jnp.ones_like(k), jnp.zeros_like(k))
                acc = list(acc)
                for g in range(kc // PACKED_ROWS):
                    acc[g % COARSE_CHAINS] = acc[g % COARSE_CHAINS] + hit[g * PACKED_ROWS:(g + 1) * PACKED_ROWS]
                return tuple(acc)
            zero = jnp.zeros((PACKED_ROWS, rw), BF16)
            acc = _unrolled(n_r, cbody, (zero,) * COARSE_CHAINS)
            total = acc[0].astype(F32)
            for part in acc[1:]:
                total = total + part.astype(F32)
            return jnp.sum(total, axis=0, keepdims=True)

        def coarse_body(step, prefix):
            cand_u = prefix | lax.shift_left(jnp.int32(1), 31 - step)
            key = cand_u ^ INT_MIN
            cand = pltpu.bitcast(key ^ ((key >> 31) & 0x7FFF0000), F32).astype(BF16)
            return jnp.where(count16(cand) >= topk, cand_u, prefix)

        coarse = lax.fori_loop(0, 16, coarse_body, jnp.zeros((1, rw), jnp.int32))
        step16 = 1 << 16
        key_neg_inf = INT_MIN + 0x7FFFFF
        key_pos_inf = 0x7F800000
        key16 = coarse ^ INT_MIN
        key16 = key16 + ((key16 >> 31) & 0xFFFF)
        base = jnp.clip(key16, key_neg_inf + step16, key_pos_inf) - step16

        def fine_body(step, off):
            cand_off = off | lax.shift_left(jnp.int32(1), 16 - step)
            cand = key_to_f32((base + cand_off) ^ INT_MIN)
            cnt = count(lambda k, c: k >= cand)
            return jnp.where(cnt >= topk, cand_off, off)

        off = lax.fori_loop(0, 17, fine_body, jnp.zeros((1, rw), jnp.int32))
        tau = key_to_f32((base + off) ^ INT_MIN)
        tau = jnp.where(tau != tau, neg_inf, tau)
        c_gt = count(lambda k, c: k > tau)
        c_eq = count(lambda k, c: k == tau)
        need = topk - c_gt

        def tie_search(_):
            def jbody(step, q):
                cand = q | lax.shift_left(jnp.int32(1), idx_bits - 1 - step)
                cnt = count(lambda k, c: (k == tau) & (c * kc + kidx_r < cand))
                return jnp.where(cnt < need, cand, q)
            return lax.fori_loop(0, idx_bits, jbody, jnp.zeros((1, rw), jnp.int32))

        any_split = jnp.max(c_eq - need) > 0.0
        jcut = lax.cond(any_split, tie_search, lambda _: jnp.full((1, rw), 2 ** idx_bits, jnp.int32), 0)
        few = tpos < int(topk)
        tau_ref[:, rows] = jnp.broadcast_to(jnp.where(few, neg_inf, tau), (SUBLANES, rw))
        j_ref[:, rows] = jnp.broadcast_to(jnp.where(few, tpos, jcut), (SUBLANES, rw))

    for h in heads:
        qst_ref[hrows(h), :] = bq_ref[:, h * LANES:(h + 1) * LANES]
    m_ref[...] = jnp.full(m_ref.shape, NEG_BIG, F32)

    def max_body(c, carry):
        s = _dot_nt(bk_ref[pl.ds(kstart(c), kc), :], qst_ref[...])
        k = jnp.concatenate([sc_ref[c, r] for r in range(per)], axis=1)
        tau = tau_ref[0:1, :]
        sel = (k > tau) | ((k == tau) & (c * kc + kidx <= j_ref[0:1, :]))
        bias = jnp.where(sel, 0.0, NEG_BIG)
        for h in heads:
            sm = s[:, hrows(h)] + bias
            s_ref[c, :, hrows(h)] = sm
            m_ref[:, hrows(h)] = jnp.maximum(m_ref[:, hrows(h)], fold(sm, jnp.maximum, ways=1))
        return carry

    _unrolled(n_chunks, max_body, 0)
    m_ref[...] = jnp.broadcast_to(jnp.max(m_ref[...], axis=0, keepdims=True), m_ref.shape)

    probs = [jnp.exp2(s_ref[c] - m_ref[0:1, :]).astype(BF16) for c in range(n_chunks)]
    out_t = _dot(jnp.concatenate([vt_ref[c] for c in range(n_chunks)], axis=1), jnp.concatenate(probs, axis=0))
    for h in heads:
        acc = out_t[:, hrows(h)]
        o_ref[:, h * HEAD_DIM:(h + 1) * HEAD_DIM] = (acc / acc[ONES_AT:ONES_AT + 1, :]).T[:, :HEAD_DIM].astype(BF16)


def _dsa(bq, iq, iwt, bk, ik, vt, b, s):
    n = bq.shape[0]
    qb = _att_qb(s)
    nb = s // qb
    nkc = s // ATT_KC
    topk = min(IDX_TOPK_MAX, s // 4)
    idx_bits = max(1, int(math.ceil(math.log2(s))))
    seq = lambda bb: (bb, 0)
    out = jnp.zeros((n, 256), BF16)
    for i in range(nb):
        blk = lambda bb, i=i: (bb * nb + i, 0)
        n_kc = (i + 1) * (qb // ATT_KC)
        out = pl.pallas_call(
            functools.partial(_dsa_kernel, blk=i, topk=float(topk), idx_bits=idx_bits),
            grid=(b,),
            in_specs=[pl.BlockSpec((qb, 512), blk), pl.BlockSpec((qb, 512), blk),
                      pl.BlockSpec((8, qb), lambda bb, i=i: (0, bb * nb + i)),
                      pl.BlockSpec((s, 128), seq), pl.BlockSpec((s, 128), seq),
                      pl.BlockSpec((nkc, LANES, ATT_KC), lambda bb: (bb, 0, 0)),
                      pl.BlockSpec(memory_space=pl.ANY)],
            out_specs=pl.BlockSpec((qb, 256), blk),
            out_shape=jax.ShapeDtypeStruct((n, 256), BF16),
            input_output_aliases={6: 0},
            scratch_shapes=[pltpu.VMEM((n_kc, qb // ATT_KC, ATT_KC, ATT_KC), F32),
                            pltpu.VMEM((n_kc, qb // ATT_KC, ATT_KC, ATT_KC), BF16),
                            pltpu.VMEM((n_kc, ATT_KC, B_HEADS * qb), F32),
                            pltpu.VMEM((SUBLANES, qb), F32), pltpu.VMEM((SUBLANES, qb), jnp.int32),
                            pltpu.VMEM((IDX_HEADS * qb, LANES), BF16), pltpu.VMEM((B_HEADS * qb, LANES), BF16),
                            pltpu.VMEM((SUBLANES, B_HEADS * qb), F32)],
            compiler_params=pltpu.CompilerParams(dimension_semantics=("parallel",), vmem_limit_bytes=VMEM_LIMIT),
        )(bq, iq, iwt, bk, ik, vt, out)
    return out


def _mla_kernel(q_ref, k_ref, v_ref, _aliased_out_ref, o_ref, s_ref, m_ref, *, blk):
    qb = q_ref.shape[0]
    kc = ATT_KC
    per = qb // kc
    n_full = per * blk
    n_chunks = n_full + per
    row = lax.broadcasted_iota(jnp.int32, (qb, kc), 0)
    col = lax.broadcasted_iota(jnp.int32, (qb, kc), 1)
    heads = range(C_HEADS)
    hs = lambda h: slice(h * LANES, (h + 1) * LANES)
    keys = lambda c: slice(c * kc, (c + 1) * kc)
    n_halves = kc // LANES
    m_ref[...] = jnp.full(m_ref.shape, NEG_BIG, F32)

    for c in range(n_chunks):
        diag = c - n_full
        for h in heads:
            s = _dot_nt(q_ref[:, hs(h)], k_ref[keys(c), hs(h)])
            if diag >= 0:
                s = s + jnp.where(diag * kc + col <= row, 0.0, NEG_BIG)
            s_ref[h, c] = s
            m_ref[h] = jnp.maximum(m_ref[h], _fold_max(s))
    for h in heads:
        m_ref[h] = jnp.broadcast_to(jnp.max(m_ref[h], axis=1, keepdims=True), (qb, LANES))
    for h in heads:
        m = jnp.concatenate([m_ref[h]] * n_halves, axis=1)
        p = jnp.concatenate([jnp.exp2(s_ref[h, c] - m).astype(BF16) for c in range(n_chunks)], axis=1)
        acc = _dot(p, v_ref[0:n_chunks * kc, hs(h)])
        o_ref[:, h * C_V:(h + 1) * C_V] = (acc[:, :C_V] / acc[:, ONES_AT:ONES_AT + 1]).astype(BF16)


def _mla(qc, kc, vc, b, s):
    n = qc.shape[0]
    qb = _att_qb(s)
    nb = s // qb
    seq = lambda bb: (bb, 0)
    out = jnp.zeros((n, 256), BF16)
    for i in range(nb):
        blk = lambda bb, i=i: (bb * nb + i, 0)
        n_kc = (i + 1) * (qb // ATT_KC)
        out = pl.pallas_call(
            functools.partial(_mla_kernel, blk=i),
            grid=(b,),
            in_specs=[pl.BlockSpec((qb, 512), blk),
                      _causal_key_block(n_kc * ATT_KC, s, 512), _causal_key_block(n_kc * ATT_KC, s, 512),
                      pl.BlockSpec(memory_space=pl.ANY)],
            out_specs=pl.BlockSpec((qb, 256), blk),
            out_shape=jax.ShapeDtypeStruct((n, 256), BF16),
            input_output_aliases={3: 0},
            scratch_shapes=[pltpu.VMEM((C_HEADS, n_kc, qb, ATT_KC), F32), pltpu.VMEM((C_HEADS, qb, LANES), F32)],
            compiler_params=pltpu.CompilerParams(dimension_semantics=("parallel",), vmem_limit_bytes=VMEM_LIMIT),
        )(qc, kc, vc, out)
    return out


def _layer_norm(z, g, b):
    mu = jnp.mean(z, axis=1, keepdims=True)
    zc = z - mu
    var = jnp.mean(zc * zc, axis=1, keepdims=True)
    return zc * lax.rsqrt(var + LN_EPS) * g + b


def _route(logits, rbias):
    scores = jax.nn.sigmoid(logits)
    biased = scores + rbias
    rows = [biased[e:e + 1, :] for e in range(N_EXPERTS)]
    gscore = []
    for g in range(N_GROUPS):
        v = rows[g * 4:(g + 1) * 4]
        best = None
        for a in range(4):
            for c in range(a + 1, 4):
                pair = v[a] + v[c]
                best = pair if best is None else jnp.maximum(best, pair)
        gscore.append(best)
    gmax = jnp.maximum(jnp.maximum(gscore[0], gscore[1]), jnp.maximum(gscore[2], gscore[3]))
    taken = jnp.zeros_like(gmax) > 1.0
    sel_rows = []
    for g in range(N_GROUPS):
        g_sel = (gscore[g] == gmax) & jnp.logical_not(taken)
        taken = taken | g_sel
        v = rows[g * 4:(g + 1) * 4]
        for e in range(4):
            rank = jnp.zeros_like(gmax)
            for j in range(4):
                if j == e:
                    continue
                ahead = (v[j] > v[e]) | ((v[j] == v[e]) & (j < e))
                rank = rank + jnp.where(ahead, 1.0, 0.0)
            sel_rows.append(g_sel & (rank < 2.0))
    wsel = [jnp.where(sel_rows[e], scores[e:e + 1, :], 0.0) for e in range(N_EXPERTS)]
    total = wsel[0]
    for e in range(1, N_EXPERTS):
        total = total + wsel[e]
    gates_t = jnp.concatenate(wsel + [jnp.zeros((LANES - N_EXPERTS, total.shape[1]), F32)], axis=0) / total
    return gates_t.T


OUT_SLAB = 256


def _outproj_kernel(x_ref, oa_ref, ob_ref, oc_ref, wo_ref, g_ref, b_ref, wr_ref, rb_ref, x1_ref, gates_ref,
                    *, alpha):
    tm = x_ref.shape[0]
    slabs = [slice(r, r + OUT_SLAB) for r in range(0, tm, OUT_SLAB)]
    ys = [_dot(oa_ref[rows, :], wo_ref[0:512, :]) + _dot(ob_ref[rows, :], wo_ref[512:768, :])
          + _dot(oc_ref[rows, :], wo_ref[768:1024, :]) for rows in slabs]
    x1s = [_layer_norm(alpha * x_ref[rows, :] + y, g_ref[...], b_ref[...]) for rows, y in zip(slabs, ys)]
    for rows, x1 in zip(slabs, x1s):
        x1_ref[rows, :] = x1

    wr = wr_ref[...]
    wrh = wr.astype(BF16)
    wrl = (wr - wrh.astype(F32)).astype(BF16)
    wr2 = jnp.concatenate([wrh, wrl], axis=0)
    for rows, x1 in zip(slabs, x1s):
        x1h = x1.astype(BF16)
        x1l = (x1 - x1h.astype(F32)).astype(BF16)
        hi = _dot_nt(wr2, x1h)
        gates = _route(hi[:N_EXPERTS] + hi[N_EXPERTS:] + _dot_nt(wrh, x1l), rb_ref[...])
        for g in range(N_GROUPS):
            gates_ref[g, rows, :] = gates[:, g * EXPERTS_PER_GROUP:(g + 1) * EXPERTS_PER_GROUP]


def _outproj(x2, oa, ob, oc, wo, g, bb, wr_t, rbias, alpha, tm):
    n, d = x2.shape
    nt = n // tm
    full = lambda a: pl.BlockSpec(a.shape, lambda t: (0,) * a.ndim)
    row = lambda c: pl.BlockSpec((tm, c), lambda t: (t, 0))
    return pl.pallas_call(
        functools.partial(_outproj_kernel, alpha=alpha),
        grid=(nt,),
        in_specs=[row(d), row(512), row(256), row(256), full(wo), full(g), full(bb), full(wr_t), full(rbias)],
        out_specs=[row(d), pl.BlockSpec((N_GROUPS, tm, EXPERTS_PER_GROUP), lambda t: (0, t, 0))],
        out_shape=[jax.ShapeDtypeStruct((n, d), F32),
                   jax.ShapeDtypeStruct((N_GROUPS, n, EXPERTS_PER_GROUP), F32)],
        compiler_params=pltpu.CompilerParams(dimension_semantics=("parallel",), vmem_limit_bytes=VMEM_LIMIT),
    )(x2, oa, ob, oc, wo, g, bb, wr_t, rbias)


def _moe_kernel(x_ref, gates_ref, wg_ref, wu_ref, wd_ref, g_ref, b_ref, o_ref, acc_ref, *, alpha):
    grp = pl.program_id(1)

    @pl.when(grp == 0)
    def _():
        acc_ref[...] = jnp.zeros(acc_ref.shape, F32)

    xb = x_ref[...].astype(BF16)
    hs = []
    for j in range(EXPERTS_PER_GROUP):
        hg = _dot(xb, wg_ref[j])
        hu = _dot(xb, wu_ref[j])
        hs.append((hg * jax.nn.sigmoid(hg) * hu * gates_ref[0, :, j:j + 1]).astype(BF16))
    wd = wd_ref[...]
    acc_ref[...] += _dot(jnp.concatenate(hs, axis=1), wd.reshape(wd.shape[0] * wd.shape[1], wd.shape[2]))

    @pl.when(grp == N_GROUPS - 1)
    def _():
        o_ref[...] = _layer_norm(alpha * x_ref[...] + acc_ref[...], g_ref[...], b_ref[...])


def _moe(x1, gates_g, wg, wu, wd, g, bb, alpha, tm):
    n, d = x1.shape
    nt = n // tm
    f = wg.shape[2]
    epg = EXPERTS_PER_GROUP
    full = lambda a: pl.BlockSpec(a.shape, lambda t, e: (0,) * a.ndim)
    return pl.pallas_call(
        functools.partial(_moe_kernel, alpha=alpha),
        grid=(nt, N_GROUPS),
        in_specs=[pl.BlockSpec((tm, d), lambda t, e: (t, 0)), pl.BlockSpec((1, tm, epg), lambda t, e: (e, t, 0)),
                  pl.BlockSpec((epg, d, f), lambda t, e: (e, 0, 0)), pl.BlockSpec((epg, d, f), lambda t, e: (e, 0, 0)),
                  pl.BlockSpec((epg, f, d), lambda t, e: (e, 0, 0)), full(g), full(bb)],
        out_specs=pl.BlockSpec((tm, d), lambda t, e: (t, 0)),
        out_shape=jax.ShapeDtypeStruct((n, d), F32),
        scratch_shapes=[pltpu.VMEM((tm, d), F32)],
        compiler_params=pltpu.CompilerParams(dimension_semantics=("parallel", "arbitrary"),
                                             vmem_limit_bytes=VMEM_LIMIT),
    )(x1, gates_g, wg, wu, wd, g, bb)


def kernel(x, w_in, attn_sinks, c_q_norm_g, c_kv_norm_g, w_uq, w_ukv, w_out, ln1_g, ln1_b, w_router, router_bias,
           w_gate, w_up, w_down, ln2_g, ln2_b):
    b, s, d = x.shape
    depth = w_in.shape[0]
    n = b * s
    alpha = (2 * depth) ** 0.25
    tm = min(512, s)
    tables = _rope_tables(s)
    wr_t = w_router.T
    rbias = router_bias.reshape(N_EXPERTS, 1)
    x2 = x.reshape(n, d)
    for l in range(depth):
        w_p, w_t = _pack_w_in(w_in[l])
        wuq_p, wukv_p = _pack_mla_w(w_uq[l], w_ukv[l])
        aq, ak, bq, bk, ik, iq, avt, vt, iwt, qc, kc, vc = _inproj(
            x2, w_p, w_t, tables, c_q_norm_g[l].reshape(1, -1), c_kv_norm_g[l].reshape(1, -1), wuq_p, wukv_p, s, tm)
        oa = _swa(attn_sinks[l], aq, ak, avt, b, s)
        ob = _dsa(bq, iq, iwt, bk, ik, vt, b, s)
        oc = _mla(qc, kc, vc, b, s)
        x1, gates_g = _outproj(x2, oa, ob, oc, w_out[l].astype(BF16), ln1_g[l].reshape(1, d), ln1_b[l].reshape(1, d),
                               wr_t, rbias, alpha, min(4 * OUT_SLAB, n))
        x2 = _moe(x1, gates_g, w_gate[l].astype(BF16), w_up[l].astype(BF16), w_down[l].astype(BF16),
                  ln2_g[l].reshape(1, d), ln2_b[l].reshape(1, d), alpha, min(1024, n))
    return x2.reshape(b, s, d)
```

```python
import functools
import math

import jax
import jax.numpy as jnp
import numpy as np
from jax import lax
from jax.experimental import pallas as pl
from jax.experimental.pallas import tpu as pltpu

HEAD_DIM = 64
ROPE_THETA = 10000.0
A_HEADS = 8
A_KV_HEADS = 2
A_WINDOW = 128
B_HEADS = 4
IDX_HEADS = 4
IDX_DIM = 64
IDX_TOPK_MAX = 256
C_HEADS = 4
C_NOPE = 64
C_ROPE = 32
C_V = 64
N_EXPERTS = 16
N_GROUPS = 4
EXPERTS_PER_GROUP = 4
LN_EPS = 1e-5
RMS_EPS = 1e-6

LANES = 128
SUBLANES = 8
PACKED_ROWS = 16
ONES_AT = 64
NEG_BIG = -1e30
LOG2E = math.log2(math.e)
INT_MIN = -(2 ** 31)
VMEM_LIMIT = 56 * 1024 * 1024

BF16 = jnp.bfloat16
F32 = jnp.float32

_NT = (((1,), (1,)), ((), ()))


def _dot(a, b):
    return jnp.dot(a, b, preferred_element_type=F32)


def _dot_nt(a, b):
    return lax.dot_general(a, b, _NT, preferred_element_type=F32)


N_ROPE_UNITS = 10


def _inproj_kernel(x_ref, w_ref, ch_ref, s1h_ref, s2h_ref, cr_ref, s1r_ref, s2r_ref, gq_ref, gkv_ref,
                   wuq_ref, wukv_ref, wt_ref,
                   aq_ref, ak_ref, bq_ref, bk_ref, ik_ref, iq_ref, avt_ref, vt_ref, iwt_ref, qc_ref, kc_ref, vc_ref,
                   *, b_scale, c_scale):
    xb = x_ref[...].astype(BF16)
    tm = xb.shape[0]
    lane = lax.broadcasted_iota(jnp.int32, (tm, LANES), 1)
    ch, s1h, s2h = ch_ref[...], s1h_ref[...], s2h_ref[...]
    cr, s1r, s2r = cr_ref[...], s1r_ref[...], s2r_ref[...]
    low = lane < 64

    def rope_h(u):
        return u * ch + pltpu.roll(u, 96, 1) * s1h + pltpu.roll(u, 32, 1) * s2h

    def rope_r(u):
        return u * cr + pltpu.roll(u, 112, 1) * s1r + pltpu.roll(u, 16, 1) * s2r

    def split(u):
        return jnp.where(low, u, 0.0), jnp.where(low, pltpu.roll(u, 64, 1), 0.0)

    def put(ref, k, u):
        ref[:, k * LANES:(k + 1) * LANES] = u.astype(BF16)

    projected = [_dot(xb, w_ref[:, g * 256:(g + 1) * 256]) for g in range(N_ROPE_UNITS // 2)]
    for g, hg in enumerate(projected):
        for half in range(2):
            unit = 2 * g + half
            u = rope_h(hg[:, half * LANES:(half + 1) * LANES])
            if unit < 4:
                put(aq_ref, unit, u * LOG2E)
            elif unit == 4:
                swapped = pltpu.roll(u, 64, 1)
                put(ak_ref, 0, jnp.where(low, u, 0.0))
                put(ak_ref, 1, jnp.where(low, 0.0, swapped))
                put(ak_ref, 2, jnp.where(low, swapped, 0.0))
                put(ak_ref, 3, jnp.where(low, 0.0, u))
            elif unit < 7:
                h0, h1 = split(u * b_scale)
                put(bq_ref, 2 * (unit - 5), h0)
                put(bq_ref, 2 * (unit - 5) + 1, h1)
            elif unit == 7:
                h0, h1 = split(u)
                put(bk_ref, 0, h0)
                put(ik_ref, 0, h1)
            else:
                h0, h1 = split(u)
                put(iq_ref, 2 * (unit - 8), h0)
                put(iq_ref, 2 * (unit - 8) + 1, h1)

    hb = _dot_nt(wt_ref[...], xb)
    sub = lax.broadcasted_iota(jnp.int32, (LANES, tm), 0)
    with_ones = lambda piece: jnp.where(sub == ONES_AT, 1.0, hb[piece * LANES:(piece + 1) * LANES]).astype(BF16)
    vt = with_ones(0)
    for j in range(tm // ATT_KC):
        vt_ref[j] = vt[:, j * ATT_KC:(j + 1) * ATT_KC]
    iwt_ref[...] = hb[LANES:LANES + 8]
    for g in range(A_KV_HEADS):
        avt_ref[g] = with_ones(2 + g)

    cq = _dot(xb, w_ref[:, 1280:1536])
    cqn = cq * lax.rsqrt(jnp.mean(cq * cq, axis=1, keepdims=True) + RMS_EPS) * gq_ref[...]
    qc = _dot(cqn.astype(BF16), wuq_ref[...])
    for h in range(C_HEADS):
        u = qc[:, h * LANES:(h + 1) * LANES]
        qc_ref[:, h * LANES:(h + 1) * LANES] = (rope_r(u) * c_scale).astype(BF16)

    ckv = _dot(xb, w_ref[:, 1536:1792])
    ckv_lat = ckv[:, :LANES]
    kr = rope_r(ckv[:, LANES:])
    ckvn = ckv_lat * lax.rsqrt(jnp.mean(ckv_lat * ckv_lat, axis=1, keepdims=True) + RMS_EPS) * gkv_ref[...]
    kv = _dot(ckvn.astype(BF16), wukv_ref[...])
    for h in range(C_HEADS):
        kc_ref[:, h * LANES:(h + 1) * LANES] = (kv[:, h * LANES:(h + 1) * LANES] + kr).astype(BF16)
        v = kv[:, (C_HEADS + h) * LANES:(C_HEADS + h + 1) * LANES]
        vc_ref[:, h * LANES:(h + 1) * LANES] = jnp.where(lane == ONES_AT, 1.0, v).astype(BF16)


def _pack_w_in(w):
    d = w.shape[0]
    z = lambda n: jnp.zeros((d, n), w.dtype)
    o = np.cumsum([0, 512, 128, 128, 256, 64, 64, 256, 64, 4, 256, 128, 32]).tolist()
    a_q, a_k, a_v, b_q, b_k, b_v, i_q, i_k, i_w, c_q, c_kv, c_kr = [w[:, o[j]:o[j + 1]] for j in range(12)]
    qs = HEAD_DIM ** -0.5
    ws = (IDX_HEADS * IDX_DIM) ** -0.5
    cols = [a_q * qs, a_k, b_q, b_k, i_k, i_q,
            c_q, c_kv, z(64), c_kr, z(32)]
    rows_t = jnp.concatenate([b_v, z(64), i_w * ws, z(LANES - IDX_HEADS),
                              a_v[:, :64], z(64), a_v[:, 64:], z(64)], axis=1).T
    return jnp.concatenate(cols, axis=1).astype(BF16), rows_t.astype(BF16)


def _pack_mla_w(w_uq, w_ukv):
    r = w_uq.shape[0]
    q = w_uq.reshape(r, C_HEADS, C_NOPE + C_ROPE)
    q = jnp.pad(q, ((0, 0), (0, 0), (0, LANES - C_NOPE - C_ROPE))).reshape(r, C_HEADS * LANES)
    r2 = w_ukv.shape[0]
    kv = w_ukv.reshape(r2, C_HEADS, C_NOPE + C_V)
    k = jnp.pad(kv[:, :, :C_NOPE], ((0, 0), (0, 0), (0, LANES - C_NOPE))).reshape(r2, C_HEADS * LANES)
    v = jnp.pad(kv[:, :, C_NOPE:], ((0, 0), (0, 0), (0, LANES - C_V))).reshape(r2, C_HEADS * LANES)
    return q.astype(BF16), jnp.concatenate([k, v], axis=1).astype(BF16)


def _rope_tables(s):
    pos = jnp.arange(s, dtype=F32)[:, None]
    lane = np.arange(LANES)
    inv_h = 1.0 / (ROPE_THETA ** (jnp.arange(0, HEAD_DIM, 2, dtype=F32) / HEAD_DIM))
    ang = pos * inv_h[None, :]
    cos, sin = jnp.cos(ang), jnp.sin(ang)
    j = lane % 32
    lo = jnp.asarray((lane % 64) < 32)
    ch = cos[:, j]
    s1h = jnp.where(lo, -sin[:, j], 0.0)
    s2h = jnp.where(lo, 0.0, sin[:, j])
    inv_r = 1.0 / (ROPE_THETA ** (jnp.arange(0, C_ROPE, 2, dtype=F32) / C_ROPE))
    angr = pos * inv_r[None, :]
    cosr, sinr = jnp.cos(angr), jnp.sin(angr)
    jr = lane % 16
    in_rope = jnp.asarray((lane >= 64) & (lane < 96))
    first = jnp.asarray((lane >= 64) & (lane < 80))
    second = jnp.asarray((lane >= 80) & (lane < 96))
    cr = jnp.where(in_rope, cosr[:, jr], 1.0)
    s1r = jnp.where(first, -sinr[:, jr], 0.0)
    s2r = jnp.where(second, sinr[:, jr], 0.0)
    return ch, s1h, s2h, cr, s1r, s2r


def _inproj(x2, w_p, w_t, tables, gq, gkv, wuq_p, wukv_p, s, tm):
    n, d = x2.shape
    nt = n // tm
    spt = s // tm
    cpt = tm // ATT_KC
    tab_spec = pl.BlockSpec((tm, LANES), lambda t: (t % spt, 0))
    full = lambda a: pl.BlockSpec(a.shape, lambda t: (0,) * a.ndim)
    row = lambda c: pl.BlockSpec((tm, c), lambda t: (t, 0))
    cols_a, cols_c = [512, 512, 512, 128, 128, 512], [512, 512, 512]
    out_specs = ([row(c) for c in cols_a]
                 + [pl.BlockSpec((A_KV_HEADS, LANES, tm), lambda t: (0, 0, t)),
                    pl.BlockSpec((cpt, LANES, ATT_KC), lambda t: (t, 0, 0)), pl.BlockSpec((8, tm), lambda t: (0, t))]
                 + [row(c) for c in cols_c])
    out_shape = ([jax.ShapeDtypeStruct((n, c), BF16) for c in cols_a]
                 + [jax.ShapeDtypeStruct((A_KV_HEADS, LANES, n), BF16),
                    jax.ShapeDtypeStruct((n // ATT_KC, LANES, ATT_KC), BF16), jax.ShapeDtypeStruct((8, n), F32)]
                 + [jax.ShapeDtypeStruct((n, c), BF16) for c in cols_c])
    return pl.pallas_call(
        functools.partial(_inproj_kernel, b_scale=HEAD_DIM ** -0.5 * LOG2E,
                          c_scale=(C_NOPE + C_ROPE) ** -0.5 * LOG2E),
        grid=(nt,),
        in_specs=[row(d), full(w_p)] + [tab_spec] * 6 + [full(gq), full(gkv), full(wuq_p), full(wukv_p), full(w_t)],
        out_specs=out_specs,
        out_shape=out_shape,
        compiler_params=pltpu.CompilerParams(dimension_semantics=("parallel",), vmem_limit_bytes=VMEM_LIMIT),
    )(x2, w_p, *tables, gq, gkv, wuq_p, wukv_p, w_t)


SWA_ROWS = 2048


def _swa_kernel(sink_ref, q_ref, kc_ref, kp_ref, vc_ref, vp_ref, o_ref):
    i = pl.program_id(1)
    w = A_WINDOW
    qb = q_ref.shape[0]
    kj = lax.broadcasted_iota(jnp.int32, (2 * w, w), 0)
    qi = lax.broadcasted_iota(jnp.int32, (2 * w, w), 1)
    diff = qi + w - kj
    in_window = (diff >= 0) & (diff < w)
    groups = [slice(g * SUBLANES, (g + 1) * SUBLANES) for g in range(2 * w // SUBLANES)]
    lane_head = lax.broadcasted_iota(jnp.int32, (1, 4 * w), 1) // w
    per_group = A_HEADS // A_KV_HEADS
    tasks = [(win, g) for win in range(qb // w) for g in range(A_KV_HEADS)]
    head_order = lambda g: [per_group * g, per_group * g + 2, per_group * g + 1, per_group * g + 3]

    def keys(win, unit):
        cols = slice(unit * LANES, (unit + 1) * LANES)
        if win == 0:
            return jnp.concatenate([kp_ref[:, cols], kc_ref[0:w, cols]], axis=0)
        return kc_ref[(win - 1) * w:(win + 1) * w, cols]

    def values_t(win, g):
        if win == 0:
            return jnp.concatenate([vp_ref[g], vc_ref[g, :, 0:w]], axis=1)
        return vc_ref[g, :, (win - 1) * w:(win + 1) * w]

    scores = []
    for win, g in tasks:
        rows = slice(win * w, (win + 1) * w)
        ok = in_window if win > 0 else in_window & ((kj >= w) | (i > 0))
        bias = jnp.concatenate([jnp.where(ok, 0.0, NEG_BIG)] * per_group, axis=1)
        q2 = jnp.concatenate([q_ref[rows, (2 * g + u) * LANES:(2 * g + u + 1) * LANES] for u in range(2)], axis=0)
        scores.append(jnp.concatenate([_dot_nt(keys(win, 2 * g), q2), _dot_nt(keys(win, 2 * g + 1), q2)], axis=1)
                      + bias)
    probs = []
    for (win, g), s in zip(tasks, scores):
        sink = jnp.zeros((1, per_group * w), F32)
        for slot, h in enumerate(head_order(g)):
            sink = jnp.where(lane_head == slot, sink_ref[h] * LOG2E, sink)
        parts = [s[gg] for gg in groups]
        while len(parts) > 1:
            parts = [jnp.maximum(parts[j], parts[j + 1]) for j in range(0, len(parts), 2)]
        m = jnp.maximum(jnp.max(parts[0], axis=0, keepdims=True), sink)
        probs.append((jnp.exp2(s - m).astype(BF16), jnp.exp2(sink - m)))
    outs = [_dot(values_t(win, g), p) for (win, g), (p, _) in zip(tasks, probs)]
    for (win, g), o, (_, sink_p) in zip(tasks, outs, probs):
        rows = slice(win * w, (win + 1) * w)
        o = (o / (o[ONES_AT:ONES_AT + 1, :] + sink_p)).T
        for slot, h in enumerate(head_order(g)):
            o_ref[rows, h * HEAD_DIM:(h + 1) * HEAD_DIM] = o[slot * w:(slot + 1) * w, :HEAD_DIM].astype(BF16)


def _swa(sinks, aq, ak, avt, b, s):
    n = aq.shape[0]
    w = A_WINDOW
    qb = min(SWA_ROWS, s)
    nb = s // qb
    wpb = qb // w
    cur = lambda bb, i: (bb * nb + i, 0)
    prev = lambda bb, i: ((bb * nb + i) * wpb - jnp.minimum(i, 1), 0)
    cur_t = lambda bb, i: (0, 0, bb * nb + i)
    prev_t = lambda bb, i: (0, 0, (bb * nb + i) * wpb - jnp.minimum(i, 1))
    return pl.pallas_call(
        _swa_kernel,
        grid=(b, nb),
        in_specs=[pl.BlockSpec(memory_space=pltpu.SMEM),
                  pl.BlockSpec((qb, 512), cur),
                  pl.BlockSpec((qb, 512), cur), pl.BlockSpec((w, 512), prev),
                  pl.BlockSpec((A_KV_HEADS, LANES, qb), cur_t), pl.BlockSpec((A_KV_HEADS, LANES, w), prev_t)],
        out_specs=pl.BlockSpec((qb, 512), cur),
        out_shape=jax.ShapeDtypeStruct((n, 512), BF16),
        compiler_params=pltpu.CompilerParams(dimension_semantics=("parallel", "parallel"),
                                             vmem_limit_bytes=VMEM_LIMIT),
    )(sinks, aq, ak, ak, avt, avt)


ATT_KC = 256
COARSE_CHAINS = 2


def _att_qb(s):
    return min(512, s)


def _causal_key_block(n_keys, s, width):
    rows = n_keys if s % n_keys == 0 else s
    return pl.BlockSpec((rows, width), lambda bb: (bb * (s // rows), 0))


def _fold_max(s):
    return jnp.maximum(s[:, :LANES], s[:, LANES:])


def _unrolled(n, body, carry):
    for c in range(n):
        carry = body(c, carry)
    return carry


def _dsa_kernel(bq_ref, iq_ref, iwt_ref, bk_ref, ik_ref, vt_ref, _aliased_out_ref, o_ref,
                sc_ref, sc16_ref, s_ref, tau_ref, j_ref, ist_ref, qst_ref, m_ref, *, blk, topk, idx_bits):
    i = blk
    qb = bq_ref.shape[0]
    kc = ATT_KC
    per = qb // kc
    n_full = per * i
    n_chunks = n_full + per
    kidx = lax.broadcasted_iota(jnp.int32, (kc, qb), 0)
    rpos = lax.broadcasted_iota(jnp.int32, (kc, qb), 1)
    neg_inf = float("-inf")
    kstart = lambda c: c * kc
    heads = range(B_HEADS)
    hrows = lambda h: slice(h * qb, (h + 1) * qb)
    groups = [slice(g * SUBLANES, (g + 1) * SUBLANES) for g in range(kc // SUBLANES)]

    def fold(x, op, ways=4):
        parts = [x[g] for g in groups[:ways]]
        for j, g in enumerate(groups[ways:]):
            parts[j % ways] = op(parts[j % ways], x[g])
        while len(parts) > 1:
            parts = [op(parts[j], parts[j + 1]) for j in range(0, len(parts), 2)]
        return parts[0]

    for h in range(IDX_HEADS):
        ist_ref[hrows(h), :] = iq_ref[:, h * LANES:(h + 1) * LANES]

    def logits(c):
        return _dot_nt(ik_ref[pl.ds(kstart(c), kc), :], ist_ref[...])

    def combine(lg):
        acc = None
        for h in range(IDX_HEADS):
            t = iwt_ref[h:h + 1, :] * jnp.maximum(lg[:, hrows(h)], 0.0)
            acc = t if acc is None else acc + t
        return acc

    rw = qb // per

    def put_scores(c, sc):
        for r in range(per):
            sc_ref[c, r] = sc[:, r * rw:(r + 1) * rw]
            sc16_ref[c, r] = sc[:, r * rw:(r + 1) * rw].astype(BF16)

    for c0 in range(0, n_chunks, per):
        lgs = [logits(c0 + d) for d in range(per)]
        for d, lg in enumerate(lgs):
            sc = combine(lg)
            if c0 == n_full:
                sc = jnp.where(d * kc + kidx <= rpos, sc, neg_inf)
            put_scores(c0 + d, sc)

    kidx_r = lax.broadcasted_iota(jnp.int32, (kc, rw), 0)
    for r in range(per):
        rows = slice(r * rw, (r + 1) * rw)
        tpos = i * qb + r * rw + lax.broadcasted_iota(jnp.int32, (1, rw), 1)
        n_r = n_full + r + 1
        if i * qb + (r + 1) * rw <= int(topk):
            tau_ref[:, rows] = jnp.full((SUBLANES, rw), neg_inf, F32)
            j_ref[:, rows] = jnp.broadcast_to(tpos, (SUBLANES, rw))
            continue

        def count(pred_fn):
            def cbody(c, acc):
                k = sc_ref[c, r]
                return acc + fold(jnp.where(pred_fn(k, c), 1.0, 0.0), jnp.add, ways=1)
            acc = _unrolled(n_r, cbody, jnp.zeros((SUBLANES, rw), F32))
            return jnp.sum(acc, axis=0, keepdims=True)

        def key_to_f32(key_u):
            key = key_u ^ INT_MIN
            return pltpu.bitcast(key ^ ((key >> 31) & 0x7FFFFFFF), F32)

        def count16(cand16):
            def cbody(c, acc):
                k = sc16_ref[c, r]
                hit = jnp.where(k >= cand16, jnp.ones_like(k), jnp.zeros_like(k))
                acc = list(acc)
                for g in range(kc // PACKED_ROWS):
                    acc[g % COARSE_CHAINS] = acc[g % COARSE_CHAINS] + hit[g * PACKED_ROWS:(g + 1) * PACKED_ROWS]
                return tuple(acc)
            zero = jnp.zeros((PACKED_ROWS, rw), BF16)
            acc = _unrolled(n_r, cbody, (zero,) * COARSE_CHAINS)
            total = acc[0].astype(F32)
            for part in acc[1:]:
                total = total + part.astype(F32)
            return jnp.sum(total, axis=0, keepdims=True)

        def coarse_body(step, prefix):
            cand_u = prefix | lax.shift_left(jnp.int32(1), 31 - step)
            key = cand_u ^ INT_MIN
            cand = pltpu.bitcast(key ^ ((key >> 31) & 0x7FFF0000), F32).astype(BF16)
            return jnp.where(count16(cand) >= topk, cand_u, prefix)

        coarse = lax.fori_loop(0, 16, coarse_body, jnp.zeros((1, rw), jnp.int32))
        step16 = 1 << 16
        key_neg_inf = INT_MIN + 0x7FFFFF
        key_pos_inf = 0x7F800000
        key16 = coarse ^ INT_MIN
        key16 = key16 + ((key16 >> 31) & 0xFFFF)
        base = jnp.clip(key16, key_neg_inf + step16, key_pos_inf) - step16

        def fine_body(step, off):
            cand_off = off | lax.shift_left(jnp.int32(1), 16 - step)
            cand = key_to_f32((base + cand_off) ^ INT_MIN)
            cnt = count(lambda k, c: k >= cand)
            return jnp.where(cnt >= topk, cand_off, off)

        off = lax.fori_loop(0, 17, fine_body, jnp.zeros((1, rw), jnp.int32))
        tau = key_to_f32((base + off) ^ INT_MIN)
        tau = jnp.where(tau != tau, neg_inf, tau)
        c_gt = count(lambda k, c: k > tau)
        c_eq = count(lambda k, c: k == tau)
        need = topk - c_gt

        def tie_search(_):
            def jbody(step, q):
                cand = q | lax.shift_left(jnp.int32(1), idx_bits - 1 - step)
                cnt = count(lambda k, c: (k == tau) & (c * kc + kidx_r < cand))
                return jnp.where(cnt < need, cand, q)
            return lax.fori_loop(0, idx_bits, jbody, jnp.zeros((1, rw), jnp.int32))

        any_split = jnp.max(c_eq - need) > 0.0
        jcut = lax.cond(any_split, tie_search, lambda _: jnp.full((1, rw), 2 ** idx_bits, jnp.int32), 0)
        few = tpos < int(topk)
        tau_ref[:, rows] = jnp.broadcast_to(jnp.where(few, neg_inf, tau), (SUBLANES, rw))
        j_ref[:, rows] = jnp.broadcast_to(jnp.where(few, tpos, jcut), (SUBLANES, rw))

    for h in heads:
        qst_ref[hrows(h), :] = bq_ref[:, h * LANES:(h + 1) * LANES]
    m_ref[...] = jnp.full(m_ref.shape, NEG_BIG, F32)

    def max_body(c, carry):
        s = _dot_nt(bk_ref[pl.ds(kstart(c), kc), :], qst_ref[...])
        k = jnp.concatenate([sc_ref[c, r] for r in range(per)], axis=1)
        tau = tau_ref[0:1, :]
        sel = (k > tau) | ((k == tau) & (c * kc + kidx <= j_ref[0:1, :]))
        bias = jnp.where(sel, 0.0, NEG_BIG)
        for h in heads:
            sm = s[:, hrows(h)] + bias
            s_ref[c, :, hrows(h)] = sm
            m_ref[:, hrows(h)] = jnp.maximum(m_ref[:, hrows(h)], fold(sm, jnp.maximum, ways=1))
        return carry

    _unrolled(n_chunks, max_body, 0)
    m_ref[...] = jnp.broadcast_to(jnp.max(m_ref[...], axis=0, keepdims=True), m_ref.shape)

    probs = [jnp.exp2(s_ref[c] - m_ref[0:1, :]).astype(BF16) for c in range(n_chunks)]
    out_t = _dot(jnp.concatenate([vt_ref[c] for c in range(n_chunks)], axis=1), jnp.concatenate(probs, axis=0))
    for h in heads:
        acc = out_t[:, hrows(h)]
        o_ref[:, h * HEAD_DIM:(h + 1) * HEAD_DIM] = (acc / acc[ONES_AT:ONES_AT + 1, :]).T[:, :HEAD_DIM].astype(BF16)


def _dsa(bq, iq, iwt, bk, ik, vt, b, s):
    n = bq.shape[0]
    qb = _att_qb(s)
    nb = s // qb
    nkc = s // ATT_KC
    topk = min(IDX_TOPK_MAX, s // 4)
    idx_bits = max(1, int(math.ceil(math.log2(s))))
    seq = lambda bb: (bb, 0)
    out = jnp.zeros((n, 256), BF16)
    for i in range(nb):
        blk = lambda bb, i=i: (bb * nb + i, 0)
        n_kc = (i + 1) * (qb // ATT_KC)
        out = pl.pallas_call(
            functools.partial(_dsa_kernel, blk=i, topk=float(topk), idx_bits=idx_bits),
            grid=(b,),
            in_specs=[pl.BlockSpec((qb, 512), blk), pl.BlockSpec((qb, 512), blk),
                      pl.BlockSpec((8, qb), lambda bb, i=i: (0, bb * nb + i)),
                      pl.BlockSpec((s, 128), seq), pl.BlockSpec((s, 128), seq),
                      pl.BlockSpec((nkc, LANES, ATT_KC), lambda bb: (bb, 0, 0)),
                      pl.BlockSpec(memory_space=pl.ANY)],
            out_specs=pl.BlockSpec((qb, 256), blk),
            out_shape=jax.ShapeDtypeStruct((n, 256), BF16),
            input_output_aliases={6: 0},
            scratch_shapes=[pltpu.VMEM((n_kc, qb // ATT_KC, ATT_KC, ATT_KC), F32),
                            pltpu.VMEM((n_kc, qb // ATT_KC, ATT_KC, ATT_KC), BF16),
                            pltpu.VMEM((n_kc, ATT_KC, B_HEADS * qb), F32),
                            pltpu.VMEM((SUBLANES, qb), F32), pltpu.VMEM((SUBLANES, qb), jnp.int32),
                            pltpu.VMEM((IDX_HEADS * qb, LANES), BF16), pltpu.VMEM((B_HEADS * qb, LANES), BF16),
                            pltpu.VMEM((SUBLANES, B_HEADS * qb), F32)],
            compiler_params=pltpu.CompilerParams(dimension_semantics=("parallel",), vmem_limit_bytes=VMEM_LIMIT),
        )(bq, iq, iwt, bk, ik, vt, out)
    return out


def _mla_kernel(q_ref, k_ref, v_ref, _aliased_out_ref, o_ref, s_ref, m_ref, *, blk):
    qb = q_ref.shape[0]
    kc = ATT_KC
    per = qb // kc
    n_full = per * blk
    n_chunks = n_full + per
    row = lax.broadcasted_iota(jnp.int32, (qb, kc), 0)
    col = lax.broadcasted_iota(jnp.int32, (qb, kc), 1)
    heads = range(C_HEADS)
    hs = lambda h: slice(h * LANES, (h + 1) * LANES)
    keys = lambda c: slice(c * kc, (c + 1) * kc)
    n_halves = kc // LANES
    m_ref[...] = jnp.full(m_ref.shape, NEG_BIG, F32)

    for c in range(n_chunks):
        diag = c - n_full
        for h in heads:
            s = _dot_nt(q_ref[:, hs(h)], k_ref[keys(c), hs(h)])
            if diag >= 0:
                s = s + jnp.where(diag * kc + col <= row, 0.0, NEG_BIG)
            s_ref[h, c] = s
            m_ref[h] = jnp.maximum(m_ref[h], _fold_max(s))
    for h in heads:
        m_ref[h] = jnp.broadcast_to(jnp.max(m_ref[h], axis=1, keepdims=True), (qb, LANES))
    for h in heads:
        m = jnp.concatenate([m_ref[h]] * n_halves, axis=1)
        p = jnp.concatenate([jnp.exp2(s_ref[h, c] - m).astype(BF16) for c in range(n_chunks)], axis=1)
        acc = _dot(p, v_ref[0:n_chunks * kc, hs(h)])
        o_ref[:, h * C_V:(h + 1) * C_V] = (acc[:, :C_V] / acc[:, ONES_AT:ONES_AT + 1]).astype(BF16)


def _mla(qc, kc, vc, b, s):
    n = qc.shape[0]
    qb = _att_qb(s)
    nb = s // qb
    seq = lambda bb: (bb, 0)
    out = jnp.zeros((n, 256), BF16)
    for i in range(nb):
        blk = lambda bb, i=i: (bb * nb + i, 0)
        n_kc = (i + 1) * (qb // ATT_KC)
        out = pl.pallas_call(
            functools.partial(_mla_kernel, blk=i),
            grid=(b,),
            in_specs=[pl.BlockSpec((qb, 512), blk),
                      _causal_key_block(n_kc * ATT_KC, s, 512), _causal_key_block(n_kc * ATT_KC, s, 512),
                      pl.BlockSpec(memory_space=pl.ANY)],
            out_specs=pl.BlockSpec((qb, 256), blk),
            out_shape=jax.ShapeDtypeStruct((n, 256), BF16),
            input_output_aliases={3: 0},
            scratch_shapes=[pltpu.VMEM((C_HEADS, n_kc, qb, ATT_KC), F32), pltpu.VMEM((C_HEADS, qb, LANES), F32)],
            compiler_params=pltpu.CompilerParams(dimension_semantics=("parallel",), vmem_limit_bytes=VMEM_LIMIT),
        )(qc, kc, vc, out)
    return out


def _layer_norm(z, g, b):
    mu = jnp.mean(z, axis=1, keepdims=True)
    zc = z - mu
    var = jnp.mean(zc * zc, axis=1, keepdims=True)
    return zc * lax.rsqrt(var + LN_EPS) * g + b


def _route(logits, rbias):
    scores = jax.nn.sigmoid(logits)
    biased = scores + rbias
    rows = [biased[e:e + 1, :] for e in range(N_EXPERTS)]
    gscore = []
    for g in range(N_GROUPS):
        v = rows[g * 4:(g + 1) * 4]
        best = None
        for a in range(4):
            for c in range(a + 1, 4):
                pair = v[a] + v[c]
                best = pair if best is None else jnp.maximum(best, pair)
        gscore.append(best)
    gmax = jnp.maximum(jnp.maximum(gscore[0], gscore[1]), jnp.maximum(gscore[2], gscore[3]))
    taken = jnp.zeros_like(gmax) > 1.0
    sel_rows = []
    for g in range(N_GROUPS):
        g_sel = (gscore[g] == gmax) & jnp.logical_not(taken)
        taken = taken | g_sel
        v = rows[g * 4:(g + 1) * 4]
        for e in range(4):
            rank = jnp.zeros_like(gmax)
            for j in range(4):
                if j == e:
                    continue
                ahead = (v[j] > v[e]) | ((v[j] == v[e]) & (j < e))
                rank = rank + jnp.where(ahead, 1.0, 0.0)
            sel_rows.append(g_sel & (rank < 2.0))
    wsel = [jnp.where(sel_rows[e], scores[e:e + 1, :], 0.0) for e in range(N_EXPERTS)]
    total = wsel[0]
    for e in range(1, N_EXPERTS):
        total = total + wsel[e]
    gates_t = jnp.concatenate(wsel + [jnp.zeros((LANES - N_EXPERTS, total.shape[1]), F32)], axis=0) / total
    return gates_t.T


OUT_SLAB = 256


def _outproj_kernel(x_ref, oa_ref, ob_ref, oc_ref, wo_ref, g_ref, b_ref, wr_ref, rb_ref, x1_ref, gates_ref,
                    *, alpha):
    tm = x_ref.shape[0]
    slabs = [slice(r, r + OUT_SLAB) for r in range(0, tm, OUT_SLAB)]
    ys = [_dot(oa_ref[rows, :], wo_ref[0:512, :]) + _dot(ob_ref[rows, :], wo_ref[512:768, :])
          + _dot(oc_ref[rows, :], wo_ref[768:1024, :]) for rows in slabs]
    x1s = [_layer_norm(alpha * x_ref[rows, :] + y, g_ref[...], b_ref[...]) for rows, y in zip(slabs, ys)]
    for rows, x1 in zip(slabs, x1s):
        x1_ref[rows, :] = x1

    wr = wr_ref[...]
    wrh = wr.astype(BF16)
    wrl = (wr - wrh.astype(F32)).astype(BF16)
    wr2 = jnp.concatenate([wrh, wrl], axis=0)
    for rows, x1 in zip(slabs, x1s):
        x1h = x1.astype(BF16)
        x1l = (x1 - x1h.astype(F32)).astype(BF16)
        hi = _dot_nt(wr2, x1h)
        gates = _route(hi[:N_EXPERTS] + hi[N_EXPERTS:] + _dot_nt(wrh, x1l), rb_ref[...])
        for g in range(N_GROUPS):
            gates_ref[g, rows, :] = gates[:, g * EXPERTS_PER_GROUP:(g + 1) * EXPERTS_PER_GROUP]


def _outproj(x2, oa, ob, oc, wo, g, bb, wr_t, rbias, alpha, tm):
    n, d = x2.shape
    nt = n // tm
    full = lambda a: pl.BlockSpec(a.shape, lambda t: (0,) * a.ndim)
    row = lambda c: pl.BlockSpec((tm, c), lambda t: (t, 0))
    return pl.pallas_call(
        functools.partial(_outproj_kernel, alpha=alpha),
        grid=(nt,),
        in_specs=[row(d), row(512), row(256), row(256), full(wo), full(g), full(bb), full(wr_t), full(rbias)],
        out_specs=[row(d), pl.BlockSpec((N_GROUPS, tm, EXPERTS_PER_GROUP), lambda t: (0, t, 0))],
        out_shape=[jax.ShapeDtypeStruct((n, d), F32),
                   jax.ShapeDtypeStruct((N_GROUPS, n, EXPERTS_PER_GROUP), F32)],
        compiler_params=pltpu.CompilerParams(dimension_semantics=("parallel",), vmem_limit_bytes=VMEM_LIMIT),
    )(x2, oa, ob, oc, wo, g, bb, wr_t, rbias)


def _moe_kernel(x_ref, gates_ref, wg_ref, wu_ref, wd_ref, g_ref, b_ref, o_ref, acc_ref, *, alpha):
    grp = pl.program_id(1)

    @pl.when(grp == 0)
    def _():
        acc_ref[...] = jnp.zeros(acc_ref.shape, F32)

    xb = x_ref[...].astype(BF16)
    hs = []
    for j in range(EXPERTS_PER_GROUP):
        hg = _dot(xb, wg_ref[j])
        hu = _dot(xb, wu_ref[j])
        hs.append((hg * jax.nn.sigmoid(hg) * hu * gates_ref[0, :, j:j + 1]).astype(BF16))
    wd = wd_ref[...]
    acc_ref[...] += _dot(jnp.concatenate(hs, axis=1), wd.reshape(wd.shape[0] * wd.shape[1], wd.shape[2]))

    @pl.when(grp == N_GROUPS - 1)
    def _():
        o_ref[...] = _layer_norm(alpha * x_ref[...] + acc_ref[...], g_ref[...], b_ref[...])


def _moe(x1, gates_g, wg, wu, wd, layer, g, bb, alpha, tm):
    n, d = x1.shape
    nt = n // tm
    f = wg.shape[3]
    epg = EXPERTS_PER_GROUP
    full = lambda a: pl.BlockSpec(a.shape, lambda t, e: (0,) * a.ndim)
    group_of_layer = lambda t, e: (layer, e, 0, 0)
    return pl.pallas_call(
        functools.partial(_moe_kernel, alpha=alpha),
        grid=(nt, N_GROUPS),
        in_specs=[pl.BlockSpec((tm, d), lambda t, e: (t, 0)), pl.BlockSpec((1, tm, epg), lambda t, e: (e, t, 0)),
                  pl.BlockSpec((None, epg, d, f), group_of_layer), pl.BlockSpec((None, epg, d, f), group_of_layer),
                  pl.BlockSpec((None, epg, f, d), group_of_layer), full(g), full(bb)],
        out_specs=pl.BlockSpec((tm, d), lambda t, e: (t, 0)),
        out_shape=jax.ShapeDtypeStruct((n, d), F32),
        scratch_shapes=[pltpu.VMEM((tm, d), F32)],
        compiler_params=pltpu.CompilerParams(dimension_semantics=("parallel", "arbitrary"),
                                             vmem_limit_bytes=VMEM_LIMIT),
    )(x1, gates_g, wg, wu, wd, g, bb)


def kernel(x, w_in, attn_sinks, c_q_norm_g, c_kv_norm_g, w_uq, w_ukv, w_out, ln1_g, ln1_b, w_router, router_bias,
           w_gate, w_up, w_down, ln2_g, ln2_b):
    b, s, d = x.shape
    depth = w_in.shape[0]
    n = b * s
    alpha = (2 * depth) ** 0.25
    tm = min(512, s)
    tables = _rope_tables(s)
    wr_t = w_router.T
    rbias = router_bias.reshape(N_EXPERTS, 1)
    x2 = x.reshape(n, d)
    w_in_b = w_in.astype(BF16)
    w_gate_b, w_up_b, w_down_b = w_gate.astype(BF16), w_up.astype(BF16), w_down.astype(BF16)
    for l in range(depth):
        w_p, w_t = _pack_w_in(w_in_b[l])
        wuq_p, wukv_p = _pack_mla_w(w_uq[l], w_ukv[l])
        aq, ak, bq, bk, ik, iq, avt, vt, iwt, qc, kc, vc = _inproj(
            x2, w_p, w_t, tables, c_q_norm_g[l].reshape(1, -1), c_kv_norm_g[l].reshape(1, -1), wuq_p, wukv_p, s, tm)
        oa = _swa(attn_sinks[l], aq, ak, avt, b, s)
        ob = _dsa(bq, iq, iwt, bk, ik, vt, b, s)
        oc = _mla(qc, kc, vc, b, s)
        x1, gates_g = _outproj(x2, oa, ob, oc, w_out[l].astype(BF16), ln1_g[l].reshape(1, d), ln1_b[l].reshape(1, d),
                               wr_t, rbias, alpha, min(4 * OUT_SLAB, n))
        x2 = _moe(x1, gates_g, w_gate_b, w_up_b, w_down_b, l,
                  ln2_g[l].reshape(1, d), ln2_b[l].reshape(1, d), alpha, min(1024, n))
    return x2.reshape(b, s, d)
```

```python
import functools
import math

import jax
import jax.numpy as jnp
import numpy as np
from jax import lax
from jax.experimental import pallas as pl
from jax.experimental.pallas import tpu as pltpu

HEAD_DIM = 64
ROPE_THETA = 10000.0
A_HEADS = 8
A_KV_HEADS = 2
A_WINDOW = 128
B_HEADS = 4
IDX_HEADS = 4
IDX_DIM = 64
IDX_TOPK_MAX = 256
C_HEADS = 4
C_NOPE = 64
C_ROPE = 32
C_V = 64
N_EXPERTS = 16
N_GROUPS = 4
EXPERTS_PER_GROUP = 4
LN_EPS = 1e-5
RMS_EPS = 1e-6

LANES = 128
SUBLANES = 8
PACKED_ROWS = 16
ONES_AT = 64
NEG_BIG = -1e30
LOG2E = math.log2(math.e)
INT_MIN = -(2 ** 31)
VMEM_LIMIT = 56 * 1024 * 1024

BF16 = jnp.bfloat16
F32 = jnp.float32

_NT = (((1,), (1,)), ((), ()))


def _dot(a, b):
    return jnp.dot(a, b, preferred_element_type=F32)


def _dot_nt(a, b):
    return lax.dot_general(a, b, _NT, preferred_element_type=F32)


N_ROPE_UNITS = 10


def _inproj_kernel(x_ref, w_ref, ch_ref, s1h_ref, s2h_ref, cr_ref, s1r_ref, s2r_ref, gq_ref, gkv_ref,
                   wuq_ref, wukv_ref, wt_ref,
                   aq_ref, ak_ref, bq_ref, bk_ref, ik_ref, iq_ref, avt_ref, vt_ref, iwt_ref, qc_ref, kc_ref, vc_ref,
                   *, b_scale, c_scale):
    xb = x_ref[...].astype(BF16)
    tm = xb.shape[0]
    lane = lax.broadcasted_iota(jnp.int32, (tm, LANES), 1)
    ch, s1h, s2h = ch_ref[...], s1h_ref[...], s2h_ref[...]
    cr, s1r, s2r = cr_ref[...], s1r_ref[...], s2r_ref[...]
    low = lane < 64

    def rope_h(u):
        return u * ch + pltpu.roll(u, 96, 1) * s1h + pltpu.roll(u, 32, 1) * s2h

    def rope_r(u):
        return u * cr + pltpu.roll(u, 112, 1) * s1r + pltpu.roll(u, 16, 1) * s2r

    def split(u):
        return jnp.where(low, u, 0.0), jnp.where(low, pltpu.roll(u, 64, 1), 0.0)

    def put(ref, k, u):
        ref[:, k * LANES:(k + 1) * LANES] = u.astype(BF16)

    projected = [_dot(xb, w_ref[:, g * 256:(g + 1) * 256]) for g in range(N_ROPE_UNITS // 2)]
    for g, hg in enumerate(projected):
        for half in range(2):
            unit = 2 * g + half
            u = rope_h(hg[:, half * LANES:(half + 1) * LANES])
            if unit < 4:
                put(aq_ref, unit, u * LOG2E)
            elif unit == 4:
                swapped = pltpu.roll(u, 64, 1)
                put(ak_ref, 0, jnp.where(low, u, 0.0))
                put(ak_ref, 1, jnp.where(low, 0.0, swapped))
                put(ak_ref, 2, jnp.where(low, swapped, 0.0))
                put(ak_ref, 3, jnp.where(low, 0.0, u))
            elif unit < 7:
                h0, h1 = split(u * b_scale)
                put(bq_ref, 2 * (unit - 5), h0)
                put(bq_ref, 2 * (unit - 5) + 1, h1)
            elif unit == 7:
                h0, h1 = split(u)
                put(bk_ref, 0, h0)
                put(ik_ref, 0, h1)
            else:
                h0, h1 = split(u)
                put(iq_ref, 2 * (unit - 8), h0)
                put(iq_ref, 2 * (unit - 8) + 1, h1)

    hb = _dot_nt(wt_ref[...], xb)
    sub = lax.broadcasted_iota(jnp.int32, (LANES, tm), 0)
    with_ones = lambda piece: jnp.where(sub == ONES_AT, 1.0, hb[piece * LANES:(piece + 1) * LANES]).astype(BF16)
    vt = with_ones(0)
    for j in range(tm // ATT_KC):
        vt_ref[j] = vt[:, j * ATT_KC:(j + 1) * ATT_KC]
    iwt_ref[...] = hb[LANES:LANES + 8]
    for g in range(A_KV_HEADS):
        avt_ref[g] = with_ones(2 + g)

    cq = _dot(xb, w_ref[:, 1280:1536])
    cqn = cq * lax.rsqrt(jnp.mean(cq * cq, axis=1, keepdims=True) + RMS_EPS) * gq_ref[...]
    qc = _dot(cqn.astype(BF16), wuq_ref[...])
    for h in range(C_HEADS):
        u = qc[:, h * LANES:(h + 1) * LANES]
        qc_ref[:, h * LANES:(h + 1) * LANES] = (rope_r(u) * c_scale).astype(BF16)

    ckv = _dot(xb, w_ref[:, 1536:1792])
    ckv_lat = ckv[:, :LANES]
    kr = rope_r(ckv[:, LANES:])
    ckvn = ckv_lat * lax.rsqrt(jnp.mean(ckv_lat * ckv_lat, axis=1, keepdims=True) + RMS_EPS) * gkv_ref[...]
    kv = _dot(ckvn.astype(BF16), wukv_ref[...])
    for h in range(C_HEADS):
        kc_ref[:, h * LANES:(h + 1) * LANES] = (kv[:, h * LANES:(h + 1) * LANES] + kr).astype(BF16)
        v = kv[:, (C_HEADS + h) * LANES:(C_HEADS + h + 1) * LANES]
        vc_ref[:, h * LANES:(h + 1) * LANES] = jnp.where(lane == ONES_AT, 1.0, v).astype(BF16)


def _pack_w_in(w):
    d = w.shape[0]
    z = lambda n: jnp.zeros((d, n), w.dtype)
    o = np.cumsum([0, 512, 128, 128, 256, 64, 64, 256, 64, 4, 256, 128, 32]).tolist()
    a_q, a_k, a_v, b_q, b_k, b_v, i_q, i_k, i_w, c_q, c_kv, c_kr = [w[:, o[j]:o[j + 1]] for j in range(12)]
    qs = HEAD_DIM ** -0.5
    ws = (IDX_HEADS * IDX_DIM) ** -0.5
    cols = [a_q * qs, a_k, b_q, b_k, i_k, i_q,
            c_q, c_kv, z(64), c_kr, z(32)]
    rows_t = jnp.concatenate([b_v, z(64), i_w * ws, z(LANES - IDX_HEADS),
                              a_v[:, :64], z(64), a_v[:, 64:], z(64)], axis=1).T
    return jnp.concatenate(cols, axis=1).astype(BF16), rows_t.astype(BF16)


def _pack_mla_w(w_uq, w_ukv):
    r = w_uq.shape[0]
    q = w_uq.reshape(r, C_HEADS, C_NOPE + C_ROPE)
    q = jnp.pad(q, ((0, 0), (0, 0), (0, LANES - C_NOPE - C_ROPE))).reshape(r, C_HEADS * LANES)
    r2 = w_ukv.shape[0]
    kv = w_ukv.reshape(r2, C_HEADS, C_NOPE + C_V)
    k = jnp.pad(kv[:, :, :C_NOPE], ((0, 0), (0, 0), (0, LANES - C_NOPE))).reshape(r2, C_HEADS * LANES)
    v = jnp.pad(kv[:, :, C_NOPE:], ((0, 0), (0, 0), (0, LANES - C_V))).reshape(r2, C_HEADS * LANES)
    return q.astype(BF16), jnp.concatenate([k, v], axis=1).astype(BF16)


def _rope_tables(s):
    pos = jnp.arange(s, dtype=F32)[:, None]
    lane = np.arange(LANES)
    inv_h = 1.0 / (ROPE_THETA ** (jnp.arange(0, HEAD_DIM, 2, dtype=F32) / HEAD_DIM))
    ang = pos * inv_h[None, :]
    cos, sin = jnp.cos(ang), jnp.sin(ang)
    j = lane % 32
    lo = jnp.asarray((lane % 64) < 32)
    ch = cos[:, j]
    s1h = jnp.where(lo, -sin[:, j], 0.0)
    s2h = jnp.where(lo, 0.0, sin[:, j])
    inv_r = 1.0 / (ROPE_THETA ** (jnp.arange(0, C_ROPE, 2, dtype=F32) / C_ROPE))
    angr = pos * inv_r[None, :]
    cosr, sinr = jnp.cos(angr), jnp.sin(angr)
    jr = lane % 16
    in_rope = jnp.asarray((lane >= 64) & (lane < 96))
    first = jnp.asarray((lane >= 64) & (lane < 80))
    second = jnp.asarray((lane >= 80) & (lane < 96))
    cr = jnp.where(in_rope, cosr[:, jr], 1.0)
    s1r = jnp.where(first, -sinr[:, jr], 0.0)
    s2r = jnp.where(second, sinr[:, jr], 0.0)
    return ch, s1h, s2h, cr, s1r, s2r


def _inproj(x2, w_p, w_t, tables, gq, gkv, wuq_p, wukv_p, s, tm):
    n, d = x2.shape
    nt = n // tm
    spt = s // tm
    cpt = tm // ATT_KC
    tab_spec = pl.BlockSpec((tm, LANES), lambda t: (t % spt, 0))
    full = lambda a: pl.BlockSpec(a.shape, lambda t: (0,) * a.ndim)
    row = lambda c: pl.BlockSpec((tm, c), lambda t: (t, 0))
    cols_a, cols_c = [512, 512, 512, 128, 128, 512], [512, 512, 512]
    out_specs = ([row(c) for c in cols_a]
                 + [pl.BlockSpec((A_KV_HEADS, LANES, tm), lambda t: (0, 0, t)),
                    pl.BlockSpec((cpt, LANES, ATT_KC), lambda t: (t, 0, 0)), pl.BlockSpec((8, tm), lambda t: (0, t))]
                 + [row(c) for c in cols_c])
    out_shape = ([jax.ShapeDtypeStruct((n, c), BF16) for c in cols_a]
                 + [jax.ShapeDtypeStruct((A_KV_HEADS, LANES, n), BF16),
                    jax.ShapeDtypeStruct((n // ATT_KC, LANES, ATT_KC), BF16), jax.ShapeDtypeStruct((8, n), F32)]
                 + [jax.ShapeDtypeStruct((n, c), BF16) for c in cols_c])
    return pl.pallas_call(
        functools.partial(_inproj_kernel, b_scale=HEAD_DIM ** -0.5 * LOG2E,
                          c_scale=(C_NOPE + C_ROPE) ** -0.5 * LOG2E),
        grid=(nt,),
        in_specs=[row(d), full(w_p)] + [tab_spec] * 6 + [full(gq), full(gkv), full(wuq_p), full(wukv_p), full(w_t)],
        out_specs=out_specs,
        out_shape=out_shape,
        compiler_params=pltpu.CompilerParams(dimension_semantics=("parallel",), vmem_limit_bytes=VMEM_LIMIT),
    )(x2, w_p, *tables, gq, gkv, wuq_p, wukv_p, w_t)


SWA_ROWS = 2048


def _swa_kernel(sink_ref, q_ref, kc_ref, kp_ref, vc_ref, vp_ref, o_ref):
    i = pl.program_id(1)
    w = A_WINDOW
    qb = q_ref.shape[0]
    kj = lax.broadcasted_iota(jnp.int32, (2 * w, w), 0)
    qi = lax.broadcasted_iota(jnp.int32, (2 * w, w), 1)
    diff = qi + w - kj
    in_window = (diff >= 0) & (diff < w)
    groups = [slice(g * SUBLANES, (g + 1) * SUBLANES) for g in range(2 * w // SUBLANES)]
    lane_head = lax.broadcasted_iota(jnp.int32, (1, 4 * w), 1) // w
    per_group = A_HEADS // A_KV_HEADS
    tasks = [(win, g) for win in range(qb // w) for g in range(A_KV_HEADS)]
    head_order = lambda g: [per_group * g, per_group * g + 2, per_group * g + 1, per_group * g + 3]

    def keys(win, unit):
        cols = slice(unit * LANES, (unit + 1) * LANES)
        if win == 0:
            return jnp.concatenate([kp_ref[:, cols], kc_ref[0:w, cols]], axis=0)
        return kc_ref[(win - 1) * w:(win + 1) * w, cols]

    def values_t(win, g):
        if win == 0:
            return jnp.concatenate([vp_ref[g], vc_ref[g, :, 0:w]], axis=1)
        return vc_ref[g, :, (win - 1) * w:(win + 1) * w]

    scores = []
    for win, g in tasks:
        rows = slice(win * w, (win + 1) * w)
        ok = in_window if win > 0 else in_window & ((kj >= w) | (i > 0))
        bias = jnp.concatenate([jnp.where(ok, 0.0, NEG_BIG)] * per_group, axis=1)
        q2 = jnp.concatenate([q_ref[rows, (2 * g + u) * LANES:(2 * g + u + 1) * LANES] for u in range(2)], axis=0)
        scores.append(jnp.concatenate([_dot_nt(keys(win, 2 * g), q2), _dot_nt(keys(win, 2 * g + 1), q2)], axis=1)
                      + bias)
    probs = []
    for (win, g), s in zip(tasks, scores):
        sink = jnp.zeros((1, per_group * w), F32)
        for slot, h in enumerate(head_order(g)):
            sink = jnp.where(lane_head == slot, sink_ref[h] * LOG2E, sink)
        parts = [s[gg] for gg in groups]
        while len(parts) > 1:
            parts = [jnp.maximum(parts[j], parts[j + 1]) for j in range(0, len(parts), 2)]
        m = jnp.maximum(jnp.max(parts[0], axis=0, keepdims=True), sink)
        probs.append((jnp.exp2(s - m).astype(BF16), jnp.exp2(sink - m)))
    outs = [_dot(values_t(win, g), p) for (win, g), (p, _) in zip(tasks, probs)]
    for (win, g), o, (_, sink_p) in zip(tasks, outs, probs):
        rows = slice(win * w, (win + 1) * w)
        o = (o / (o[ONES_AT:ONES_AT + 1, :] + sink_p)).T
        for slot, h in enumerate(head_order(g)):
            o_ref[rows, h * HEAD_DIM:(h + 1) * HEAD_DIM] = o[slot * w:(slot + 1) * w, :HEAD_DIM].astype(BF16)


def _swa(sinks, aq, ak, avt, b, s):
    n = aq.shape[0]
    w = A_WINDOW
    qb = min(SWA_ROWS, s)
    nb = s // qb
    wpb = qb // w
    cur = lambda bb, i: (bb * nb + i, 0)
    prev = lambda bb, i: ((bb * nb + i) * wpb - jnp.minimum(i, 1), 0)
    cur_t = lambda bb, i: (0, 0, bb * nb + i)
    prev_t = lambda bb, i: (0, 0, (bb * nb + i) * wpb - jnp.minimum(i, 1))
    return pl.pallas_call(
        _swa_kernel,
        grid=(b, nb),
        in_specs=[pl.BlockSpec(memory_space=pltpu.SMEM),
                  pl.BlockSpec((qb, 512), cur),
                  pl.BlockSpec((qb, 512), cur), pl.BlockSpec((w, 512), prev),
                  pl.BlockSpec((A_KV_HEADS, LANES, qb), cur_t), pl.BlockSpec((A_KV_HEADS, LANES, w), prev_t)],
        out_specs=pl.BlockSpec((qb, 512), cur),
        out_shape=jax.ShapeDtypeStruct((n, 512), BF16),
        compiler_params=pltpu.CompilerParams(dimension_semantics=("parallel", "parallel"),
                                             vmem_limit_bytes=VMEM_LIMIT),
    )(sinks, aq, ak, ak, avt, avt)


ATT_KC = 256
COARSE_CHAINS = 2


def _att_qb(s):
    return min(512, s)


def _causal_key_block(n_keys, s, width):
    rows = n_keys if s % n_keys == 0 else s
    return pl.BlockSpec((rows, width), lambda bb: (bb * (s // rows), 0))


def _fold_max(s):
    return jnp.maximum(s[:, :LANES], s[:, LANES:])


def _unrolled(n, body, carry):
    for c in range(n):
        carry = body(c, carry)
    return carry


def _dsa_kernel(bq_ref, iq_ref, iwt_ref, bk_ref, ik_ref, vt_ref, _aliased_out_ref, o_ref,
                sc_ref, sc16_ref, s_ref, tau_ref, j_ref, ist_ref, qst_ref, m_ref, *, blk, topk, idx_bits):
    i = blk
    qb = bq_ref.shape[0]
    kc = ATT_KC
    per = qb // kc
    n_full = per * i
    n_chunks = n_full + per
    kidx = lax.broadcasted_iota(jnp.int32, (kc, qb), 0)
    rpos = lax.broadcasted_iota(jnp.int32, (kc, qb), 1)
    neg_inf = float("-inf")
    kstart = lambda c: c * kc
    heads = range(B_HEADS)
    hrows = lambda h: slice(h * qb, (h + 1) * qb)
    groups = [slice(g * SUBLANES, (g + 1) * SUBLANES) for g in range(kc // SUBLANES)]

    def fold(x, op, ways=4):
        parts = [x[g] for g in groups[:ways]]
        for j, g in enumerate(groups[ways:]):
            parts[j % ways] = op(parts[j % ways], x[g])
        while len(parts) > 1:
            parts = [op(parts[j], parts[j + 1]) for j in range(0, len(parts), 2)]
        return parts[0]

    for h in range(IDX_HEADS):
        ist_ref[hrows(h), :] = iq_ref[:, h * LANES:(h + 1) * LANES]

    def logits(c):
        return _dot_nt(ik_ref[pl.ds(kstart(c), kc), :], ist_ref[...])

    def combine(lg):
        acc = None
        for h in range(IDX_HEADS):
            t = iwt_ref[h:h + 1, :] * jnp.maximum(lg[:, hrows(h)], 0.0)
            acc = t if acc is None else acc + t
        return acc

    rw = qb // per

    def put_scores(c, sc):
        for r in range(per):
            sc_ref[c, r] = sc[:, r * rw:(r + 1) * rw]
            sc16_ref[c, r] = sc[:, r * rw:(r + 1) * rw].astype(BF16)

    for c0 in range(0, n_chunks, per):
        lgs = [logits(c0 + d) for d in range(per)]
        for d, lg in enumerate(lgs):
            sc = combine(lg)
            if c0 == n_full:
                sc = jnp.where(d * kc + kidx <= rpos, sc, neg_inf)
            put_scores(c0 + d, sc)

    kidx_r = lax.broadcasted_iota(jnp.int32, (kc, rw), 0)
    for r in range(per):
        rows = slice(r * rw, (r + 1) * rw)
        tpos = i * qb + r * rw + lax.broadcasted_iota(jnp.int32, (1, rw), 1)
        n_r = n_full + r + 1
        if i * qb + (r + 1) * rw <= int(topk):
            tau_ref[:, rows] = jnp.full((SUBLANES, rw), neg_inf, F32)
            j_ref[:, rows] = jnp.broadcast_to(tpos, (SUBLANES, rw))
            continue

        def count(pred_fn):
            def cbody(c, acc):
                k = sc_ref[c, r]
                return acc + fold(jnp.where(pred_fn(k, c), 1.0, 0.0), jnp.add, ways=1)
            acc = _unrolled(n_r, cbody, jnp.zeros((SUBLANES, rw), F32))
            return jnp.sum(acc, axis=0, keepdims=True)

        def key_to_f32(key_u):
            key = key_u ^ INT_MIN
            return pltpu.bitcast(key ^ ((key >> 31) & 0x7FFFFFFF), F32)

        def count16(cand16):
            def cbody(c, acc):
                k = sc16_ref[c, r]
                hit = jnp.where(k >= cand16, jnp.ones_like(k), jnp.zeros_like(k))
                acc = list(acc)
                for g in range(kc // PACKED_ROWS):
                    acc[g % COARSE_CHAINS] = acc[g % COARSE_CHAINS] + hit[g * PACKED_ROWS:(g + 1) * PACKED_ROWS]
                return tuple(acc)
            zero = jnp.zeros((PACKED_ROWS, rw), BF16)
            acc = _unrolled(n_r, cbody, (zero,) * COARSE_CHAINS)
            total = acc[0].astype(F32)
            for part in acc[1:]:
                total = total + part.astype(F32)
            return jnp.sum(total, axis=0, keepdims=True)

        def coarse_body(step, prefix):
            cand_u = prefix | lax.shift_left(jnp.int32(1), 31 - step)
            key = cand_u ^ INT_MIN
            cand = pltpu.bitcast(key ^ ((key >> 31) & 0x7FFF0000), F32).astype(BF16)
            return jnp.where(count16(cand) >= topk, cand_u, prefix)

        coarse = lax.fori_loop(0, 16, coarse_body, jnp.zeros((1, rw), jnp.int32))
        step16 = 1 << 16
        key_neg_inf = INT_MIN + 0x7FFFFF
        key_pos_inf = 0x7F800000
        key16 = coarse ^ INT_MIN
        key16 = key16 + ((key16 >> 31) & 0xFFFF)
        base = jnp.clip(key16, key_neg_inf + step16, key_pos_inf) - step16

        def fine_body(step, off):
            cand_off = off | lax.shift_left(jnp.int32(1), 16 - step)
            cand = key_to_f32((base + cand_off) ^ INT_MIN)
            cnt = count(lambda k, c: k >= cand)
            return jnp.where(cnt >= topk, cand_off, off)

        off = lax.fori_loop(0, 17, fine_body, jnp.zeros((1, rw), jnp.int32))
        tau = key_to_f32((base + off) ^ INT_MIN)
        tau = jnp.where(tau != tau, neg_inf, tau)
        c_gt = count(lambda k, c: k > tau)
        c_eq = count(lambda k, c: k == tau)
        need = topk - c_gt

        def tie_search(_):
            def jbody(step, q):
                cand = q | lax.shift_left(jnp.int32(1), idx_bits - 1 - step)
                cnt = count(lambda k, c: (k == tau) & (c * kc + kidx_r < cand))
                return jnp.where(cnt < need, cand, q)
            return lax.fori_loop(0, idx_bits, jbody, jnp.zeros((1, rw), jnp.int32))

        any_split = jnp.max(c_eq - need) > 0.0
        jcut = lax.cond(any_split, tie_search, lambda _: jnp.full((1, rw), 2 ** idx_bits, jnp.int32), 0)
        few = tpos < int(topk)
        tau_ref[:, rows] = jnp.broadcast_to(jnp.where(few, neg_inf, tau), (SUBLANES, rw))
        j_ref[:, rows] = jnp.broadcast_to(jnp.where(few, tpos, jcut), (SUBLANES, rw))

    for h in heads:
        qst_ref[hrows(h), :] = bq_ref[:, h * LANES:(h + 1) * LANES]
    m_ref[...] = jnp.full(m_ref.shape, NEG_BIG, F32)

    def max_body(c, carry):
        s = _dot_nt(bk_ref[pl.ds(kstart(c), kc), :], qst_ref[...])
        k = jnp.concatenate([sc_ref[c, r] for r in range(per)], axis=1)
        tau = tau_ref[0:1, :]
        sel = (k > tau) | ((k == tau) & (c * kc + kidx <= j_ref[0:1, :]))
        bias = jnp.where(sel, 0.0, NEG_BIG)
        for h in heads:
            sm = s[:, hrows(h)] + bias
            s_ref[c, :, hrows(h)] = sm
            m_ref[:, hrows(h)] = jnp.maximum(m_ref[:, hrows(h)], fold(sm, jnp.maximum, ways=1))
        return carry

    _unrolled(n_chunks, max_body, 0)
    m_ref[...] = jnp.broadcast_to(jnp.max(m_ref[...], axis=0, keepdims=True), m_ref.shape)

    probs = [jnp.exp2(s_ref[c] - m_ref[0:1, :]).astype(BF16) for c in range(n_chunks)]
    out_t = _dot(jnp.concatenate([vt_ref[c] for c in range(n_chunks)], axis=1), jnp.concatenate(probs, axis=0))
    for h in heads:
        acc = out_t[:, hrows(h)]
        o_ref[:, h * HEAD_DIM:(h + 1) * HEAD_DIM] = (acc / acc[ONES_AT:ONES_AT + 1, :]).T[:, :HEAD_DIM].astype(BF16)


def _dsa(bq, iq, iwt, bk, ik, vt, b, s):
    n = bq.shape[0]
    qb = _att_qb(s)
    nb = s // qb
    nkc = s // ATT_KC
    topk = min(IDX_TOPK_MAX, s // 4)
    idx_bits = max(1, int(math.ceil(math.log2(s))))
    seq = lambda bb: (bb, 0)
    out = jnp.zeros((n, 256), BF16)
    for i in range(nb):
        blk = lambda bb, i=i: (bb * nb + i, 0)
        n_kc = (i + 1) * (qb // ATT_KC)
        out = pl.pallas_call(
            functools.partial(_dsa_kernel, blk=i, topk=float(topk), idx_bits=idx_bits),
            grid=(b,),
            in_specs=[pl.BlockSpec((qb, 512), blk), pl.BlockSpec((qb, 512), blk),
                      pl.BlockSpec((8, qb), lambda bb, i=i: (0, bb * nb + i)),
                      pl.BlockSpec((s, 128), seq), pl.BlockSpec((s, 128), seq),
                      pl.BlockSpec((nkc, LANES, ATT_KC), lambda bb: (bb, 0, 0)),
                      pl.BlockSpec(memory_space=pl.ANY)],
            out_specs=pl.BlockSpec((qb, 256), blk),
            out_shape=jax.ShapeDtypeStruct((n, 256), BF16),
            input_output_aliases={6: 0},
            scratch_shapes=[pltpu.VMEM((n_kc, qb // ATT_KC, ATT_KC, ATT_KC), F32),
                            pltpu.VMEM((n_kc, qb // ATT_KC, ATT_KC, ATT_KC), BF16),
                            pltpu.VMEM((n_kc, ATT_KC, B_HEADS * qb), F32),
                            pltpu.VMEM((SUBLANES, qb), F32), pltpu.VMEM((SUBLANES, qb), jnp.int32),
                            pltpu.VMEM((IDX_HEADS * qb, LANES), BF16), pltpu.VMEM((B_HEADS * qb, LANES), BF16),
                            pltpu.VMEM((SUBLANES, B_HEADS * qb), F32)],
            compiler_params=pltpu.CompilerParams(dimension_semantics=("parallel",), vmem_limit_bytes=VMEM_LIMIT),
        )(bq, iq, iwt, bk, ik, vt, out)
    return out


def _mla_kernel(q_ref, k_ref, v_ref, _aliased_out_ref, o_ref, s_ref, m_ref, *, blk):
    qb = q_ref.shape[0]
    kc = ATT_KC
    per = qb // kc
    n_full = per * blk
    n_chunks = n_full + per
    row = lax.broadcasted_iota(jnp.int32, (qb, kc), 0)
    col = lax.broadcasted_iota(jnp.int32, (qb, kc), 1)
    heads = range(C_HEADS)
    hs = lambda h: slice(h * LANES, (h + 1) * LANES)
    keys = lambda c: slice(c * kc, (c + 1) * kc)
    n_halves = kc // LANES
    m_ref[...] = jnp.full(m_ref.shape, NEG_BIG, F32)

    for c in range(n_chunks):
        diag = c - n_full
        for h in heads:
            s = _dot_nt(q_ref[:, hs(h)], k_ref[keys(c), hs(h)])
            if diag >= 0:
                s = s + jnp.where(diag * kc + col <= row, 0.0, NEG_BIG)
            s_ref[h, c] = s
            m_ref[h] = jnp.maximum(m_ref[h], _fold_max(s))
    for h in heads:
        m_ref[h] = jnp.broadcast_to(jnp.max(m_ref[h], axis=1, keepdims=True), (qb, LANES))
    for h in heads:
        m = jnp.concatenate([m_ref[h]] * n_halves, axis=1)
        p = jnp.concatenate([jnp.exp2(s_ref[h, c] - m).astype(BF16) for c in range(n_chunks)], axis=1)
        acc = _dot(p, v_ref[0:n_chunks * kc, hs(h)])
        o_ref[:, h * C_V:(h + 1) * C_V] = (acc[:, :C_V] / acc[:, ONES_AT:ONES_AT + 1]).astype(BF16)


def _mla(qc, kc, vc, b, s):
    n = qc.shape[0]
    qb = _att_qb(s)
    nb = s // qb
    seq = lambda bb: (bb, 0)
    out = jnp.zeros((n, 256), BF16)
    for i in range(nb):
        blk = lambda bb, i=i: (bb * nb + i, 0)
        n_kc = (i + 1) * (qb // ATT_KC)
        out = pl.pallas_call(
            functools.partial(_mla_kernel, blk=i),
            grid=(b,),
            in_specs=[pl.BlockSpec((qb, 512), blk),
                      _causal_key_block(n_kc * ATT_KC, s, 512), _causal_key_block(n_kc * ATT_KC, s, 512),
                      pl.BlockSpec(memory_space=pl.ANY)],
            out_specs=pl.BlockSpec((qb, 256), blk),
            out_shape=jax.ShapeDtypeStruct((n, 256), BF16),
            input_output_aliases={3: 0},
            scratch_shapes=[pltpu.VMEM((C_HEADS, n_kc, qb, ATT_KC), F32), pltpu.VMEM((C_HEADS, qb, LANES), F32)],
            compiler_params=pltpu.CompilerParams(dimension_semantics=("parallel",), vmem_limit_bytes=VMEM_LIMIT),
        )(qc, kc, vc, out)
    return out


def _layer_norm(z, g, b):
    mu = jnp.mean(z, axis=1, keepdims=True)
    zc = z - mu
    var = jnp.mean(zc * zc, axis=1, keepdims=True)
    return zc * lax.rsqrt(var + LN_EPS) * g + b


def _route(logits, rbias):
    scores = jax.nn.sigmoid(logits)
    biased = scores + rbias
    rows = [biased[e:e + 1, :] for e in range(N_EXPERTS)]
    gscore = []
    for g in range(N_GROUPS):
        v = rows[g * 4:(g + 1) * 4]
        best = None
        for a in range(4):
            for c in range(a + 1, 4):
                pair = v[a] + v[c]
                best = pair if best is None else jnp.maximum(best, pair)
        gscore.append(best)
    gmax = jnp.maximum(jnp.maximum(gscore[0], gscore[1]), jnp.maximum(gscore[2], gscore[3]))
    taken = jnp.zeros_like(gmax) > 1.0
    sel_rows = []
    for g in range(N_GROUPS):
        g_sel = (gscore[g] == gmax) & jnp.logical_not(taken)
        taken = taken | g_sel
        v = rows[g * 4:(g + 1) * 4]
        for e in range(4):
            rank = jnp.zeros_like(gmax)
            for j in range(4):
                if j == e:
                    continue
                ahead = (v[j] > v[e]) | ((v[j] == v[e]) & (j < e))
                rank = rank + jnp.where(ahead, 1.0, 0.0)
            sel_rows.append(g_sel & (rank < 2.0))
    wsel = [jnp.where(sel_rows[e], scores[e:e + 1, :], 0.0) for e in range(N_EXPERTS)]
    total = wsel[0]
    for e in range(1, N_EXPERTS):
        total = total + wsel[e]
    gates_t = jnp.concatenate(wsel + [jnp.zeros((LANES - N_EXPERTS, total.shape[1]), F32)], axis=0) / total
    return gates_t.T


OUT_SLAB = 256


def _outproj_kernel(x_ref, oa_ref, ob_ref, oc_ref, wo_ref, g_ref, b_ref, wr_ref, rb_ref, x1_ref, gates_ref,
                    *, alpha):
    tm = x_ref.shape[0]
    slabs = [slice(r, r + OUT_SLAB) for r in range(0, tm, OUT_SLAB)]
    ys = [_dot(oa_ref[rows, :], wo_ref[0:512, :]) + _dot(ob_ref[rows, :], wo_ref[512:768, :])
          + _dot(oc_ref[rows, :], wo_ref[768:1024, :]) for rows in slabs]
    x1s = [_layer_norm(alpha * x_ref[rows, :] + y, g_ref[...], b_ref[...]) for rows, y in zip(slabs, ys)]
    for rows, x1 in zip(slabs, x1s):
        x1_ref[rows, :] = x1

    wr = wr_ref[...]
    wrh = wr.astype(BF16)
    wrl = (wr - wrh.astype(F32)).astype(BF16)
    wr2 = jnp.concatenate([wrh, wrl], axis=0)
    for rows, x1 in zip(slabs, x1s):
        x1h = x1.astype(BF16)
        x1l = (x1 - x1h.astype(F32)).astype(BF16)
        hi = _dot_nt(wr2, x1h)
        gates = _route(hi[:N_EXPERTS] + hi[N_EXPERTS:] + _dot_nt(wrh, x1l), rb_ref[...])
        for g in range(N_GROUPS):
            gates_ref[g, rows, :] = gates[:, g * EXPERTS_PER_GROUP:(g + 1) * EXPERTS_PER_GROUP]


def _outproj(x2, oa, ob, oc, wo, g, bb, wr_t, rbias, alpha, tm):
    n, d = x2.shape
    nt = n // tm
    full = lambda a: pl.BlockSpec(a.shape, lambda t: (0,) * a.ndim)
    row = lambda c: pl.BlockSpec((tm, c), lambda t: (t, 0))
    return pl.pallas_call(
        functools.partial(_outproj_kernel, alpha=alpha),
        grid=(nt,),
        in_specs=[row(d), row(512), row(256), row(256), full(wo), full(g), full(bb), full(wr_t), full(rbias)],
        out_specs=[row(d), pl.BlockSpec((N_GROUPS, tm, EXPERTS_PER_GROUP), lambda t: (0, t, 0))],
        out_shape=[jax.ShapeDtypeStruct((n, d), F32),
                   jax.ShapeDtypeStruct((N_GROUPS, n, EXPERTS_PER_GROUP), F32)],
        compiler_params=pltpu.CompilerParams(dimension_semantics=("parallel",), vmem_limit_bytes=VMEM_LIMIT),
    )(x2, oa, ob, oc, wo, g, bb, wr_t, rbias)


def _moe_kernel(x_ref, gates_ref, wg_ref, wu_ref, wd_ref, g_ref, b_ref, o_ref, *, alpha):
    x = x_ref[...]
    xb = x.astype(BF16)
    hs = []
    for e in range(N_EXPERTS):
        hg = _dot(xb, wg_ref[e])
        hu = _dot(xb, wu_ref[e])
        gate = gates_ref[e // EXPERTS_PER_GROUP, :, e % EXPERTS_PER_GROUP:e % EXPERTS_PER_GROUP + 1]
        hs.append((hg * jax.nn.sigmoid(hg) * hu * gate).astype(BF16))
    wd = wd_ref[...]
    y = _dot(jnp.concatenate(hs, axis=1), wd.reshape(wd.shape[0] * wd.shape[1], wd.shape[2]))
    o_ref[...] = _layer_norm(alpha * x + y, g_ref[...], b_ref[...])


def _moe(x1, gates_g, wg, wu, wd, layer, g, bb, alpha, tm):
    n, d = x1.shape
    nt = n // tm
    f = wg.shape[3]
    epg = EXPERTS_PER_GROUP
    full = lambda a: pl.BlockSpec(a.shape, lambda t: (0,) * a.ndim)
    resident = lambda shape: pl.BlockSpec((None,) + shape, lambda t: (layer, 0, 0, 0), pipeline_mode=pl.Buffered(1))
    return pl.pallas_call(
        functools.partial(_moe_kernel, alpha=alpha),
        grid=(nt,),
        in_specs=[pl.BlockSpec((tm, d), lambda t: (t, 0)), pl.BlockSpec((N_GROUPS, tm, epg), lambda t: (0, t, 0)),
                  resident((N_EXPERTS, d, f)), resident((N_EXPERTS, d, f)), resident((N_EXPERTS, f, d)),
                  full(g), full(bb)],
        out_specs=pl.BlockSpec((tm, d), lambda t: (t, 0)),
        out_shape=jax.ShapeDtypeStruct((n, d), F32),
        compiler_params=pltpu.CompilerParams(dimension_semantics=("parallel",), vmem_limit_bytes=VMEM_LIMIT),
    )(x1, gates_g, wg, wu, wd, g, bb)


def kernel(x, w_in, attn_sinks, c_q_norm_g, c_kv_norm_g, w_uq, w_ukv, w_out, ln1_g, ln1_b, w_router, router_bias,
           w_gate, w_up, w_down, ln2_g, ln2_b):
    b, s, d = x.shape
    depth = w_in.shape[0]
    n = b * s
    alpha = (2 * depth) ** 0.25
    tm = min(512, s)
    tables = _rope_tables(s)
    wr_t = w_router.T
    rbias = router_bias.reshape(N_EXPERTS, 1)
    x2 = x.reshape(n, d)
    w_in_b = w_in.astype(BF16)
    w_gate_b, w_up_b, w_down_b = w_gate.astype(BF16), w_up.astype(BF16), w_down.astype(BF16)
    for l in range(depth):
        w_p, w_t = _pack_w_in(w_in_b[l])
        wuq_p, wukv_p = _pack_mla_w(w_uq[l], w_ukv[l])
        aq, ak, bq, bk, ik, iq, avt, vt, iwt, qc, kc, vc = _inproj(
            x2, w_p, w_t, tables, c_q_norm_g[l].reshape(1, -1), c_kv_norm_g[l].reshape(1, -1), wuq_p, wukv_p, s, tm)
        oa = _swa(attn_sinks[l], aq, ak, avt, b, s)
        ob = _dsa(bq, iq, iwt, bk, ik, vt, b, s)
        oc = _mla(qc, kc, vc, b, s)
        x1, gates_g = _outproj(x2, oa, ob, oc, w_out[l].astype(BF16), ln1_g[l].reshape(1, d), ln1_b[l].reshape(1, d),
                               wr_t, rbias, alpha, min(4 * OUT_SLAB, n))
        x2 = _moe(x1, gates_g, w_gate_b, w_up_b, w_down_b, l,
                  ln2_g[l].reshape(1, d), ln2_b[l].reshape(1, d), alpha, min(1024, n))
    return x2.reshape(b, s, d)
```

```python
import functools
import math

import jax
import jax.numpy as jnp
import numpy as np
from jax import lax
from jax.experimental import pallas as pl
from jax.experimental.pallas import tpu as pltpu

HEAD_DIM = 64
ROPE_THETA = 10000.0
A_HEADS = 8
A_KV_HEADS = 2
A_WINDOW = 128
B_HEADS = 4
IDX_HEADS = 4
IDX_DIM = 64
IDX_TOPK_MAX = 256
C_HEADS = 4
C_NOPE = 64
C_ROPE = 32
C_V = 64
N_EXPERTS = 16
N_GROUPS = 4
EXPERTS_PER_GROUP = 4
LN_EPS = 1e-5
RMS_EPS = 1e-6

LANES = 128
SUBLANES = 8
PACKED_ROWS = 16
ONES_AT = 64
NEG_BIG = -1e30
LOG2E = math.log2(math.e)
INT_MIN = -(2 ** 31)
VMEM_LIMIT = 56 * 1024 * 1024

BF16 = jnp.bfloat16
F32 = jnp.float32

_NT = (((1,), (1,)), ((), ()))


def _dot(a, b):
    return jnp.dot(a, b, preferred_element_type=F32)


def _dot_nt(a, b):
    return lax.dot_general(a, b, _NT, preferred_element_type=F32)


N_ROPE_UNITS = 10


def _inproj_kernel(x_ref, w_ref, ch_ref, s1h_ref, s2h_ref, cr_ref, s1r_ref, s2r_ref, gq_ref, gkv_ref,
                   wuq_ref, wukv_ref, wt_ref,
                   aq_ref, ak_ref, bq_ref, bk_ref, ik_ref, iq_ref, avt_ref, vt_ref, iwt_ref, qc_ref, kc_ref, vc_ref,
                   *, b_scale, c_scale):
    xb = x_ref[...].astype(BF16)
    tm = xb.shape[0]
    lane = lax.broadcasted_iota(jnp.int32, (tm, LANES), 1)
    ch, s1h, s2h = ch_ref[...], s1h_ref[...], s2h_ref[...]
    cr, s1r, s2r = cr_ref[...], s1r_ref[...], s2r_ref[...]
    low = lane < 64

    def rope_h(u):
        return u * ch + pltpu.roll(u, 96, 1) * s1h + pltpu.roll(u, 32, 1) * s2h

    def rope_r(u):
        return u * cr + pltpu.roll(u, 112, 1) * s1r + pltpu.roll(u, 16, 1) * s2r

    def split(u):
        return jnp.where(low, u, 0.0), jnp.where(low, pltpu.roll(u, 64, 1), 0.0)

    def put(ref, k, u):
        ref[:, k * LANES:(k + 1) * LANES] = u.astype(BF16)

    projected = [_dot(xb, w_ref[:, g * 256:(g + 1) * 256]) for g in range(N_ROPE_UNITS // 2)]
    for g, hg in enumerate(projected):
        for half in range(2):
            unit = 2 * g + half
            u = rope_h(hg[:, half * LANES:(half + 1) * LANES])
            if unit < 4:
                put(aq_ref, unit, u * LOG2E)
            elif unit == 4:
                swapped = pltpu.roll(u, 64, 1)
                put(ak_ref, 0, jnp.where(low, u, 0.0))
                put(ak_ref, 1, jnp.where(low, 0.0, swapped))
                put(ak_ref, 2, jnp.where(low, swapped, 0.0))
                put(ak_ref, 3, jnp.where(low, 0.0, u))
            elif unit < 7:
                h0, h1 = split(u * b_scale)
                put(bq_ref, 2 * (unit - 5), h0)
                put(bq_ref, 2 * (unit - 5) + 1, h1)
            elif unit == 7:
                h0, h1 = split(u)
                put(bk_ref, 0, h0)
                put(ik_ref, 0, h1)
            else:
                h0, h1 = split(u)
                put(iq_ref, 2 * (unit - 8), h0)
                put(iq_ref, 2 * (unit - 8) + 1, h1)

    hb = _dot_nt(wt_ref[...], xb)
    sub = lax.broadcasted_iota(jnp.int32, (LANES, tm), 0)
    with_ones = lambda piece: jnp.where(sub == ONES_AT, 1.0, hb[piece * LANES:(piece + 1) * LANES]).astype(BF16)
    vt = with_ones(0)
    for j in range(tm // ATT_KC):
        vt_ref[j] = vt[:, j * ATT_KC:(j + 1) * ATT_KC]
    iwt_ref[...] = hb[LANES:LANES + 8]
    for g in range(A_KV_HEADS):
        avt_ref[g] = with_ones(2 + g)

    cq = _dot(xb, w_ref[:, 1280:1536])
    cqn = cq * lax.rsqrt(jnp.mean(cq * cq, axis=1, keepdims=True) + RMS_EPS) * gq_ref[...]
    qc = _dot(cqn.astype(BF16), wuq_ref[...])
    for h in range(C_HEADS):
        u = qc[:, h * LANES:(h + 1) * LANES]
        qc_ref[:, h * LANES:(h + 1) * LANES] = (rope_r(u) * c_scale).astype(BF16)

    ckv = _dot(xb, w_ref[:, 1536:1792])
    ckv_lat = ckv[:, :LANES]
    kr = rope_r(ckv[:, LANES:])
    ckvn = ckv_lat * lax.rsqrt(jnp.mean(ckv_lat * ckv_lat, axis=1, keepdims=True) + RMS_EPS) * gkv_ref[...]
    kv = _dot(ckvn.astype(BF16), wukv_ref[...])
    for h in range(C_HEADS):
        kc_ref[:, h * LANES:(h + 1) * LANES] = (kv[:, h * LANES:(h + 1) * LANES] + kr).astype(BF16)
        v = kv[:, (C_HEADS + h) * LANES:(C_HEADS + h + 1) * LANES]
        vc_ref[:, h * LANES:(h + 1) * LANES] = jnp.where(lane == ONES_AT, 1.0, v).astype(BF16)


def _pack_w_in(w):
    d = w.shape[0]
    z = lambda n: jnp.zeros((d, n), w.dtype)
    o = np.cumsum([0, 512, 128, 128, 256, 64, 64, 256, 64, 4, 256, 128, 32]).tolist()
    a_q, a_k, a_v, b_q, b_k, b_v, i_q, i_k, i_w, c_q, c_kv, c_kr = [w[:, o[j]:o[j + 1]] for j in range(12)]
    qs = HEAD_DIM ** -0.5
    ws = (IDX_HEADS * IDX_DIM) ** -0.5
    cols = [a_q * qs, a_k, b_q, b_k, i_k, i_q,
            c_q, c_kv, z(64), c_kr, z(32)]
    rows_t = jnp.concatenate([b_v, z(64), i_w * ws, z(LANES - IDX_HEADS),
                              a_v[:, :64], z(64), a_v[:, 64:], z(64)], axis=1).T
    return jnp.concatenate(cols, axis=1).astype(BF16), rows_t.astype(BF16)


def _pack_mla_w(w_uq, w_ukv):
    r = w_uq.shape[0]
    q = w_uq.reshape(r, C_HEADS, C_NOPE + C_ROPE)
    q = jnp.pad(q, ((0, 0), (0, 0), (0, LANES - C_NOPE - C_ROPE))).reshape(r, C_HEADS * LANES)
    r2 = w_ukv.shape[0]
    kv = w_ukv.reshape(r2, C_HEADS, C_NOPE + C_V)
    k = jnp.pad(kv[:, :, :C_NOPE], ((0, 0), (0, 0), (0, LANES - C_NOPE))).reshape(r2, C_HEADS * LANES)
    v = jnp.pad(kv[:, :, C_NOPE:], ((0, 0), (0, 0), (0, LANES - C_V))).reshape(r2, C_HEADS * LANES)
    return q.astype(BF16), jnp.concatenate([k, v], axis=1).astype(BF16)


def _rope_tables(s):
    pos = jnp.arange(s, dtype=F32)[:, None]
    lane = np.arange(LANES)
    inv_h = 1.0 / (ROPE_THETA ** (jnp.arange(0, HEAD_DIM, 2, dtype=F32) / HEAD_DIM))
    ang = pos * inv_h[None, :]
    cos, sin = jnp.cos(ang), jnp.sin(ang)
    j = lane % 32
    lo = jnp.asarray((lane % 64) < 32)
    ch = cos[:, j]
    s1h = jnp.where(lo, -sin[:, j], 0.0)
    s2h = jnp.where(lo, 0.0, sin[:, j])
    inv_r = 1.0 / (ROPE_THETA ** (jnp.arange(0, C_ROPE, 2, dtype=F32) / C_ROPE))
    angr = pos * inv_r[None, :]
    cosr, sinr = jnp.cos(angr), jnp.sin(angr)
    jr = lane % 16
    in_rope = jnp.asarray((lane >= 64) & (lane < 96))
    first = jnp.asarray((lane >= 64) & (lane < 80))
    second = jnp.asarray((lane >= 80) & (lane < 96))
    cr = jnp.where(in_rope, cosr[:, jr], 1.0)
    s1r = jnp.where(first, -sinr[:, jr], 0.0)
    s2r = jnp.where(second, sinr[:, jr], 0.0)
    return ch, s1h, s2h, cr, s1r, s2r


def _inproj(x2, w_p, w_t, tables, gq, gkv, wuq_p, wukv_p, s, tm):
    n, d = x2.shape
    nt = n // tm
    spt = s // tm
    cpt = tm // ATT_KC
    tab_spec = pl.BlockSpec((tm, LANES), lambda t: (t % spt, 0))
    full = lambda a: pl.BlockSpec(a.shape, lambda t: (0,) * a.ndim)
    row = lambda c: pl.BlockSpec((tm, c), lambda t: (t, 0))
    cols_a, cols_c = [512, 512, 512, 128, 128, 512], [512, 512, 512]
    out_specs = ([row(c) for c in cols_a]
                 + [pl.BlockSpec((A_KV_HEADS, LANES, tm), lambda t: (0, 0, t)),
                    pl.BlockSpec((cpt, LANES, ATT_KC), lambda t: (t, 0, 0)), pl.BlockSpec((8, tm), lambda t: (0, t))]
                 + [row(c) for c in cols_c])
    out_shape = ([jax.ShapeDtypeStruct((n, c), BF16) for c in cols_a]
                 + [jax.ShapeDtypeStruct((A_KV_HEADS, LANES, n), BF16),
                    jax.ShapeDtypeStruct((n // ATT_KC, LANES, ATT_KC), BF16), jax.ShapeDtypeStruct((8, n), F32)]
                 + [jax.ShapeDtypeStruct((n, c), BF16) for c in cols_c])
    return pl.pallas_call(
        functools.partial(_inproj_kernel, b_scale=HEAD_DIM ** -0.5 * LOG2E,
                          c_scale=(C_NOPE + C_ROPE) ** -0.5 * LOG2E),
        grid=(nt,),
        in_specs=[row(d), full(w_p)] + [tab_spec] * 6 + [full(gq), full(gkv), full(wuq_p), full(wukv_p), full(w_t)],
        out_specs=out_specs,
        out_shape=out_shape,
        compiler_params=pltpu.CompilerParams(dimension_semantics=("parallel",), vmem_limit_bytes=VMEM_LIMIT),
    )(x2, w_p, *tables, gq, gkv, wuq_p, wukv_p, w_t)


SWA_ROWS = 2048


def _swa_kernel(sink_ref, q_ref, kc_ref, kp_ref, vc_ref, vp_ref, o_ref):
    i = pl.program_id(1)
    w = A_WINDOW
    qb = q_ref.shape[0]
    kj = lax.broadcasted_iota(jnp.int32, (2 * w, w), 0)
    qi = lax.broadcasted_iota(jnp.int32, (2 * w, w), 1)
    diff = qi + w - kj
    in_window = (diff >= 0) & (diff < w)
    groups = [slice(g * SUBLANES, (g + 1) * SUBLANES) for g in range(2 * w // SUBLANES)]
    lane_head = lax.broadcasted_iota(jnp.int32, (1, 4 * w), 1) // w
    per_group = A_HEADS // A_KV_HEADS
    tasks = [(win, g) for win in range(qb // w) for g in range(A_KV_HEADS)]
    head_order = lambda g: [per_group * g, per_group * g + 2, per_group * g + 1, per_group * g + 3]

    def keys(win, unit):
        cols = slice(unit * LANES, (unit + 1) * LANES)
        if win == 0:
            return jnp.concatenate([kp_ref[:, cols], kc_ref[0:w, cols]], axis=0)
        return kc_ref[(win - 1) * w:(win + 1) * w, cols]

    def values_t(win, g):
        if win == 0:
            return jnp.concatenate([vp_ref[g], vc_ref[g, :, 0:w]], axis=1)
        return vc_ref[g, :, (win - 1) * w:(win + 1) * w]

    scores = []
    for win, g in tasks:
        rows = slice(win * w, (win + 1) * w)
        ok = in_window if win > 0 else in_window & ((kj >= w) | (i > 0))
        bias = jnp.concatenate([jnp.where(ok, 0.0, NEG_BIG)] * per_group, axis=1)
        q2 = jnp.concatenate([q_ref[rows, (2 * g + u) * LANES:(2 * g + u + 1) * LANES] for u in range(2)], axis=0)
        scores.append(jnp.concatenate([_dot_nt(keys(win, 2 * g), q2), _dot_nt(keys(win, 2 * g + 1), q2)], axis=1)
                      + bias)
    probs = []
    for (win, g), s in zip(tasks, scores):
        sink = jnp.zeros((1, per_group * w), F32)
        for slot, h in enumerate(head_order(g)):
            sink = jnp.where(lane_head == slot, sink_ref[h] * LOG2E, sink)
        parts = [s[gg] for gg in groups]
        while len(parts) > 1:
            parts = [jnp.maximum(parts[j], parts[j + 1]) for j in range(0, len(parts), 2)]
        m = jnp.maximum(jnp.max(parts[0], axis=0, keepdims=True), sink)
        probs.append((jnp.exp2(s - m).astype(BF16), jnp.exp2(sink - m)))
    outs = [_dot(values_t(win, g), p) for (win, g), (p, _) in zip(tasks, probs)]
    for (win, g), o, (_, sink_p) in zip(tasks, outs, probs):
        rows = slice(win * w, (win + 1) * w)
        o = (o / (o[ONES_AT:ONES_AT + 1, :] + sink_p)).T
        for slot, h in enumerate(head_order(g)):
            o_ref[rows, h * HEAD_DIM:(h + 1) * HEAD_DIM] = o[slot * w:(slot + 1) * w, :HEAD_DIM].astype(BF16)


def _swa(sinks, aq, ak, avt, b, s):
    n = aq.shape[0]
    w = A_WINDOW
    qb = min(SWA_ROWS, s)
    nb = s // qb
    wpb = qb // w
    cur = lambda bb, i: (bb * nb + i, 0)
    prev = lambda bb, i: ((bb * nb + i) * wpb - jnp.minimum(i, 1), 0)
    cur_t = lambda bb, i: (0, 0, bb * nb + i)
    prev_t = lambda bb, i: (0, 0, (bb * nb + i) * wpb - jnp.minimum(i, 1))
    return pl.pallas_call(
        _swa_kernel,
        grid=(b, nb),
        in_specs=[pl.BlockSpec(memory_space=pltpu.SMEM),
                  pl.BlockSpec((qb, 512), cur),
                  pl.BlockSpec((qb, 512), cur), pl.BlockSpec((w, 512), prev),
                  pl.BlockSpec((A_KV_HEADS, LANES, qb), cur_t), pl.BlockSpec((A_KV_HEADS, LANES, w), prev_t)],
        out_specs=pl.BlockSpec((qb, 512), cur),
        out_shape=jax.ShapeDtypeStruct((n, 512), BF16),
        compiler_params=pltpu.CompilerParams(dimension_semantics=("parallel", "parallel"),
                                             vmem_limit_bytes=VMEM_LIMIT),
    )(sinks, aq, ak, ak, avt, avt)


ATT_KC = 256
COARSE_CHAINS = 2


def _att_qb(s):
    return min(512, s)


def _causal_key_block(n_keys, s, width):
    rows = n_keys if s % n_keys == 0 else s
    return pl.BlockSpec((rows, width), lambda bb: (bb * (s // rows), 0))


def _fold_max(s):
    return jnp.maximum(s[:, :LANES], s[:, LANES:])


def _unrolled(n, body, carry):
    for c in range(n):
        carry = body(c, carry)
    return carry


def _dsa_kernel(bq_ref, iq_ref, iwt_ref, bk_ref, ik_ref, vt_ref, _aliased_out_ref, o_ref,
                sc_ref, sc16_ref, s_ref, tau_ref, j_ref, ist_ref, qst_ref, m_ref, *, blk, topk, idx_bits):
    i = blk
    qb = bq_ref.shape[0]
    kc = ATT_KC
    per = qb // kc
    n_full = per * i
    n_chunks = n_full + per
    kidx = lax.broadcasted_iota(jnp.int32, (kc, qb), 0)
    rpos = lax.broadcasted_iota(jnp.int32, (kc, qb), 1)
    neg_inf = float("-inf")
    kstart = lambda c: c * kc
    heads = range(B_HEADS)
    hrows = lambda h: slice(h * qb, (h + 1) * qb)
    groups = [slice(g * SUBLANES, (g + 1) * SUBLANES) for g in range(kc // SUBLANES)]

    def fold(x, op, ways=4):
        parts = [x[g] for g in groups[:ways]]
        for j, g in enumerate(groups[ways:]):
            parts[j % ways] = op(parts[j % ways], x[g])
        while len(parts) > 1:
            parts = [op(parts[j], parts[j + 1]) for j in range(0, len(parts), 2)]
        return parts[0]

    for h in range(IDX_HEADS):
        ist_ref[hrows(h), :] = iq_ref[:, h * LANES:(h + 1) * LANES]

    def logits(c):
        return _dot_nt(ik_ref[pl.ds(kstart(c), kc), :], ist_ref[...])

    def combine(lg):
        acc = None
        for h in range(IDX_HEADS):
            t = iwt_ref[h:h + 1, :] * jnp.maximum(lg[:, hrows(h)], 0.0)
            acc = t if acc is None else acc + t
        return acc

    rw = qb // per

    def put_scores(c, sc):
        for r in range(per):
            sc_ref[c, r] = sc[:, r * rw:(r + 1) * rw]
            sc16_ref[c, r] = sc[:, r * rw:(r + 1) * rw].astype(BF16)

    for c0 in range(0, n_chunks, per):
        lgs = [logits(c0 + d) for d in range(per)]
        for d, lg in enumerate(lgs):
            sc = combine(lg)
            if c0 == n_full:
                sc = jnp.where(d * kc + kidx <= rpos, sc, neg_inf)
            put_scores(c0 + d, sc)

    kidx_r = lax.broadcasted_iota(jnp.int32, (kc, rw), 0)
    for r in range(per):
        rows = slice(r * rw, (r + 1) * rw)
        tpos = i * qb + r * rw + lax.broadcasted_iota(jnp.int32, (1, rw), 1)
        n_r = n_full + r + 1
        if i * qb + (r + 1) * rw <= int(topk):
            tau_ref[:, rows] = jnp.full((SUBLANES, rw), neg_inf, F32)
            j_ref[:, rows] = jnp.broadcast_to(tpos, (SUBLANES, rw))
            continue

        def count(pred_fn):
            def cbody(c, acc):
                k = sc_ref[c, r]
                return acc + fold(jnp.where(pred_fn(k, c), 1.0, 0.0), jnp.add, ways=1)
            acc = _unrolled(n_r, cbody, jnp.zeros((SUBLANES, rw), F32))
            return jnp.sum(acc, axis=0, keepdims=True)

        def key_to_f32(key_u):
            key = key_u ^ INT_MIN
            return pltpu.bitcast(key ^ ((key >> 31) & 0x7FFFFFFF), F32)

        def count16(cand16):
            def cbody(c, acc):
                k = sc16_ref[c, r]
                hit = jnp.where(k >= cand16, jnp.ones_like(k), jnp.zeros_like(k))
                acc = list(acc)
                for g in range(kc // PACKED_ROWS):
                    acc[g % COARSE_CHAINS] = acc[g % COARSE_CHAINS] + hit[g * PACKED_ROWS:(g + 1) * PACKED_ROWS]
                return tuple(acc)
            zero = jnp.zeros((PACKED_ROWS, rw), BF16)
            acc = _unrolled(n_r, cbody, (zero,) * COARSE_CHAINS)
            total = acc[0].astype(F32)
            for part in acc[1:]:
                total = total + part.astype(F32)
            return jnp.sum(total, axis=0, keepdims=True)

        def coarse_body(step, prefix):
            cand_u = prefix | lax.shift_left(jnp.int32(1), 31 - step)
            key = cand_u ^ INT_MIN
            cand = pltpu.bitcast(key ^ ((key >> 31) & 0x7FFF0000), F32).astype(BF16)
            return jnp.where(count16(cand) >= topk, cand_u, prefix)

        coarse = lax.fori_loop(0, 16, coarse_body, jnp.zeros((1, rw), jnp.int32))
        step16 = 1 << 16
        key_neg_inf = INT_MIN + 0x7FFFFF
        key_pos_inf = 0x7F800000
        key16 = coarse ^ INT_MIN
        key16 = key16 + ((key16 >> 31) & 0xFFFF)
        base = jnp.clip(key16, key_neg_inf + step16, key_pos_inf) - step16

        def fine_body(step, off):
            cand_off = off | lax.shift_left(jnp.int32(1), 16 - step)
            cand = key_to_f32((base + cand_off) ^ INT_MIN)
            cnt = count(lambda k, c: k >= cand)
            return jnp.where(cnt >= topk, cand_off, off)

        off = lax.fori_loop(0, 17, fine_body, jnp.zeros((1, rw), jnp.int32))
        tau = key_to_f32((base + off) ^ INT_MIN)
        tau = jnp.where(tau != tau, neg_inf, tau)
        c_gt = count(lambda k, c: k > tau)
        c_eq = count(lambda k, c: k == tau)
        need = topk - c_gt

        def tie_search(_):
            def jbody(step, q):
                cand = q | lax.shift_left(jnp.int32(1), idx_bits - 1 - step)
                cnt = count(lambda k, c: (k == tau) & (c * kc + kidx_r < cand))
                return jnp.where(cnt < need, cand, q)
            return lax.fori_loop(0, idx_bits, jbody, jnp.zeros((1, rw), jnp.int32))

        any_split = jnp.max(c_eq - need) > 0.0
        jcut = lax.cond(any_split, tie_search, lambda _: jnp.full((1, rw), 2 ** idx_bits, jnp.int32), 0)
        few = tpos < int(topk)
        tau_ref[:, rows] = jnp.broadcast_to(jnp.where(few, neg_inf, tau), (SUBLANES, rw))
        j_ref[:, rows] = jnp.broadcast_to(jnp.where(few, tpos, jcut), (SUBLANES, rw))

    for h in heads:
        qst_ref[hrows(h), :] = bq_ref[:, h * LANES:(h + 1) * LANES]
    m_ref[...] = jnp.full(m_ref.shape, NEG_BIG, F32)

    def max_body(c, carry):
        s = _dot_nt(bk_ref[pl.ds(kstart(c), kc), :], qst_ref[...])
        k = jnp.concatenate([sc_ref[c, r] for r in range(per)], axis=1)
        tau = tau_ref[0:1, :]
        sel = (k > tau) | ((k == tau) & (c * kc + kidx <= j_ref[0:1, :]))
        bias = jnp.where(sel, 0.0, NEG_BIG)
        for h in heads:
            sm = s[:, hrows(h)] + bias
            s_ref[c, :, hrows(h)] = sm
            m_ref[:, hrows(h)] = jnp.maximum(m_ref[:, hrows(h)], fold(sm, jnp.maximum, ways=1))
        return carry

    _unrolled(n_chunks, max_body, 0)
    m_ref[...] = jnp.broadcast_to(jnp.max(m_ref[...], axis=0, keepdims=True), m_ref.shape)

    probs = [jnp.exp2(s_ref[c] - m_ref[0:1, :]).astype(BF16) for c in range(n_chunks)]
    out_t = _dot(jnp.concatenate([vt_ref[c] for c in range(n_chunks)], axis=1), jnp.concatenate(probs, axis=0))
    for h in heads:
        acc = out_t[:, hrows(h)]
        o_ref[:, h * HEAD_DIM:(h + 1) * HEAD_DIM] = (acc / acc[ONES_AT:ONES_AT + 1, :]).T[:, :HEAD_DIM].astype(BF16)


def _dsa(bq, iq, iwt, bk, ik, vt, b, s):
    n = bq.shape[0]
    qb = _att_qb(s)
    nb = s // qb
    nkc = s // ATT_KC
    topk = min(IDX_TOPK_MAX, s // 4)
    idx_bits = max(1, int(math.ceil(math.log2(s))))
    seq = lambda bb: (bb, 0)
    out = jnp.zeros((n, 256), BF16)
    for i in range(nb):
        blk = lambda bb, i=i: (bb * nb + i, 0)
        n_kc = (i + 1) * (qb // ATT_KC)
        out = pl.pallas_call(
            functools.partial(_dsa_kernel, blk=i, topk=float(topk), idx_bits=idx_bits),
            grid=(b,),
            in_specs=[pl.BlockSpec((qb, 512), blk), pl.BlockSpec((qb, 512), blk),
                      pl.BlockSpec((8, qb), lambda bb, i=i: (0, bb * nb + i)),
                      pl.BlockSpec((s, 128), seq), pl.BlockSpec((s, 128), seq),
                      pl.BlockSpec((nkc, LANES, ATT_KC), lambda bb: (bb, 0, 0)),
                      pl.BlockSpec(memory_space=pl.ANY)],
            out_specs=pl.BlockSpec((qb, 256), blk),
            out_shape=jax.ShapeDtypeStruct((n, 256), BF16),
            input_output_aliases={6: 0},
            scratch_shapes=[pltpu.VMEM((n_kc, qb // ATT_KC, ATT_KC, ATT_KC), F32),
                            pltpu.VMEM((n_kc, qb // ATT_KC, ATT_KC, ATT_KC), BF16),
                            pltpu.VMEM((n_kc, ATT_KC, B_HEADS * qb), F32),
                            pltpu.VMEM((SUBLANES, qb), F32), pltpu.VMEM((SUBLANES, qb), jnp.int32),
                            pltpu.VMEM((IDX_HEADS * qb, LANES), BF16), pltpu.VMEM((B_HEADS * qb, LANES), BF16),
                            pltpu.VMEM((SUBLANES, B_HEADS * qb), F32)],
            compiler_params=pltpu.CompilerParams(dimension_semantics=("parallel",), vmem_limit_bytes=VMEM_LIMIT),
        )(bq, iq, iwt, bk, ik, vt, out)
    return out


def _mla_kernel(q_ref, k_ref, v_ref, _aliased_out_ref, o_ref, s_ref, m_ref, *, blk):
    qb = q_ref.shape[0]
    kc = ATT_KC
    per = qb // kc
    n_full = per * blk
    n_chunks = n_full + per
    row = lax.broadcasted_iota(jnp.int32, (qb, kc), 0)
    col = lax.broadcasted_iota(jnp.int32, (qb, kc), 1)
    heads = range(C_HEADS)
    hs = lambda h: slice(h * LANES, (h + 1) * LANES)
    keys = lambda c: slice(c * kc, (c + 1) * kc)
    n_halves = kc // LANES
    m_ref[...] = jnp.full(m_ref.shape, NEG_BIG, F32)

    for c in range(n_chunks):
        diag = c - n_full
        for h in heads:
            s = _dot_nt(q_ref[:, hs(h)], k_ref[keys(c), hs(h)])
            if diag >= 0:
                s = s + jnp.where(diag * kc + col <= row, 0.0, NEG_BIG)
            s_ref[h, c] = s
            m_ref[h] = jnp.maximum(m_ref[h], _fold_max(s))
    for h in heads:
        m_ref[h] = jnp.broadcast_to(jnp.max(m_ref[h], axis=1, keepdims=True), (qb, LANES))
    for h in heads:
        m = jnp.concatenate([m_ref[h]] * n_halves, axis=1)
        p = jnp.concatenate([jnp.exp2(s_ref[h, c] - m).astype(BF16) for c in range(n_chunks)], axis=1)
        acc = _dot(p, v_ref[0:n_chunks * kc, hs(h)])
        o_ref[:, h * C_V:(h + 1) * C_V] = (acc[:, :C_V] / acc[:, ONES_AT:ONES_AT + 1]).astype(BF16)


def _mla(qc, kc, vc, b, s):
    n = qc.shape[0]
    qb = _att_qb(s)
    nb = s // qb
    out = jnp.zeros((n, 256), BF16)
    for i in range(nb):
        blk = lambda bb, i=i: (bb * nb + i, 0)
        n_kc = (i + 1) * (qb // ATT_KC)
        out = pl.pallas_call(
            functools.partial(_mla_kernel, blk=i),
            grid=(b,),
            in_specs=[pl.BlockSpec((qb, 512), blk),
                      _causal_key_block(n_kc * ATT_KC, s, 512), _causal_key_block(n_kc * ATT_KC, s, 512),
                      pl.BlockSpec(memory_space=pl.ANY)],
            out_specs=pl.BlockSpec((qb, 256), blk),
            out_shape=jax.ShapeDtypeStruct((n, 256), BF16),
            input_output_aliases={3: 0},
            scratch_shapes=[pltpu.VMEM((C_HEADS, n_kc, qb, ATT_KC), F32), pltpu.VMEM((C_HEADS, qb, LANES), F32)],
            compiler_params=pltpu.CompilerParams(dimension_semantics=("parallel",), vmem_limit_bytes=VMEM_LIMIT),
        )(qc, kc, vc, out)
    return out


def _layer_norm(z, g, b):
    mu = jnp.mean(z, axis=1, keepdims=True)
    zc = z - mu
    var = jnp.mean(zc * zc, axis=1, keepdims=True)
    return zc * lax.rsqrt(var + LN_EPS) * g + b


def _route(logits, rbias):
    scores = jax.nn.sigmoid(logits)
    biased = scores + rbias
    rows = [biased[e:e + 1, :] for e in range(N_EXPERTS)]
    gscore = []
    for g in range(N_GROUPS):
        v = rows[g * 4:(g + 1) * 4]
        best = None
        for a in range(4):
            for c in range(a + 1, 4):
                pair = v[a] + v[c]
                best = pair if best is None else jnp.maximum(best, pair)
        gscore.append(best)
    gmax = jnp.maximum(jnp.maximum(gscore[0], gscore[1]), jnp.maximum(gscore[2], gscore[3]))
    taken = jnp.zeros_like(gmax) > 1.0
    sel_rows = []
    for g in range(N_GROUPS):
        g_sel = (gscore[g] == gmax) & jnp.logical_not(taken)
        taken = taken | g_sel
        v = rows[g * 4:(g + 1) * 4]
        for e in range(4):
            rank = jnp.zeros_like(gmax)
            for j in range(4):
                if j == e:
                    continue
                ahead = (v[j] > v[e]) | ((v[j] == v[e]) & (j < e))
                rank = rank + jnp.where(ahead, 1.0, 0.0)
            sel_rows.append(g_sel & (rank < 2.0))
    wsel = [jnp.where(sel_rows[e], scores[e:e + 1, :], 0.0) for e in range(N_EXPERTS)]
    total = wsel[0]
    for e in range(1, N_EXPERTS):
        total = total + wsel[e]
    gates_t = jnp.concatenate(wsel + [jnp.zeros((LANES - N_EXPERTS, total.shape[1]), F32)], axis=0) / total
    return gates_t.T


OUT_SLAB = 256


def _outproj_kernel(x_ref, oa_ref, ob_ref, oc_ref, wo_ref, g_ref, b_ref, wr_ref, rb_ref, x1_ref, gates_ref,
                    *, alpha):
    tm = x_ref.shape[0]
    slabs = [slice(r, r + OUT_SLAB) for r in range(0, tm, OUT_SLAB)]
    ys = [_dot(oa_ref[rows, :], wo_ref[0:512, :]) + _dot(ob_ref[rows, :], wo_ref[512:768, :])
          + _dot(oc_ref[rows, :], wo_ref[768:1024, :]) for rows in slabs]
    x1s = [_layer_norm(alpha * x_ref[rows, :] + y, g_ref[...], b_ref[...]) for rows, y in zip(slabs, ys)]
    for rows, x1 in zip(slabs, x1s):
        x1_ref[rows, :] = x1

    wr = wr_ref[...]
    wrh = wr.astype(BF16)
    wrl = (wr - wrh.astype(F32)).astype(BF16)
    wr2 = jnp.concatenate([wrh, wrl], axis=0)
    for rows, x1 in zip(slabs, x1s):
        x1h = x1.astype(BF16)
        x1l = (x1 - x1h.astype(F32)).astype(BF16)
        hi = _dot_nt(wr2, x1h)
        gates = _route(hi[:N_EXPERTS] + hi[N_EXPERTS:] + _dot_nt(wrh, x1l), rb_ref[...])
        for g in range(N_GROUPS):
            gates_ref[g, rows, :] = gates[:, g * EXPERTS_PER_GROUP:(g + 1) * EXPERTS_PER_GROUP]


def _outproj(x2, oa, ob, oc, wo, g, bb, wr_t, rbias, alpha, tm):
    n, d = x2.shape
    nt = n // tm
    full = lambda a: pl.BlockSpec(a.shape, lambda t: (0,) * a.ndim)
    row = lambda c: pl.BlockSpec((tm, c), lambda t: (t, 0))
    return pl.pallas_call(
        functools.partial(_outproj_kernel, alpha=alpha),
        grid=(nt,),
        in_specs=[row(d), row(512), row(256), row(256), full(wo), full(g), full(bb), full(wr_t), full(rbias)],
        out_specs=[row(d), pl.BlockSpec((N_GROUPS, tm, EXPERTS_PER_GROUP), lambda t: (0, t, 0))],
        out_shape=[jax.ShapeDtypeStruct((n, d), F32),
                   jax.ShapeDtypeStruct((N_GROUPS, n, EXPERTS_PER_GROUP), F32)],
        compiler_params=pltpu.CompilerParams(dimension_semantics=("parallel",), vmem_limit_bytes=VMEM_LIMIT),
    )(x2, oa, ob, oc, wo, g, bb, wr_t, rbias)


def _moe_kernel(x_ref, gates_ref, wg_ref, wu_ref, wd_ref, g_ref, b_ref, o_ref, *, alpha):
    x = x_ref[...]
    xb = x.astype(BF16)
    hs = []
    for e in range(N_EXPERTS):
        hg = _dot(xb, wg_ref[e])
        hu = _dot(xb, wu_ref[e])
        gate = gates_ref[e // EXPERTS_PER_GROUP, :, e % EXPERTS_PER_GROUP:e % EXPERTS_PER_GROUP + 1]
        hs.append((hg * jax.nn.sigmoid(hg) * hu * gate).astype(BF16))
    wd = wd_ref[...]
    y = _dot(jnp.concatenate(hs, axis=1), wd.reshape(wd.shape[0] * wd.shape[1], wd.shape[2]))
    o_ref[...] = _layer_norm(alpha * x + y, g_ref[...], b_ref[...])


def _moe(x1, gates_g, wg, wu, wd, layer, g, bb, alpha, tm):
    n, d = x1.shape
    nt = n // tm
    f = wg.shape[3]
    epg = EXPERTS_PER_GROUP
    full = lambda a: pl.BlockSpec(a.shape, lambda t: (0,) * a.ndim)
    resident = lambda shape: pl.BlockSpec((None,) + shape, lambda t: (layer, 0, 0, 0), pipeline_mode=pl.Buffered(1))
    return pl.pallas_call(
        functools.partial(_moe_kernel, alpha=alpha),
        grid=(nt,),
        in_specs=[pl.BlockSpec((tm, d), lambda t: (t, 0)), pl.BlockSpec((N_GROUPS, tm, epg), lambda t: (0, t, 0)),
                  resident((N_EXPERTS, d, f)), resident((N_EXPERTS, d, f)), resident((N_EXPERTS, f, d)),
                  full(g), full(bb)],
        out_specs=pl.BlockSpec((tm, d), lambda t: (t, 0)),
        out_shape=jax.ShapeDtypeStruct((n, d), F32),
        compiler_params=pltpu.CompilerParams(dimension_semantics=("parallel",), vmem_limit_bytes=VMEM_LIMIT),
    )(x1, gates_g, wg, wu, wd, g, bb)


def kernel(x, w_in, attn_sinks, c_q_norm_g, c_kv_norm_g, w_uq, w_ukv, w_out, ln1_g, ln1_b, w_router, router_bias,
           w_gate, w_up, w_down, ln2_g, ln2_b):
    b, s, d = x.shape
    depth = w_in.shape[0]
    n = b * s
    alpha = (2 * depth) ** 0.25
    tm = min(512, s)
    tables = _rope_tables(s)
    wr_t = w_router.T
    rbias = router_bias.reshape(N_EXPERTS, 1)
    x2 = x.reshape(n, d)
    w_in_b = w_in.astype(BF16)
    w_gate_b, w_up_b, w_down_b = w_gate.astype(BF16), w_up.astype(BF16), w_down.astype(BF16)
    for l in range(depth):
        w_p, w_t = _pack_w_in(w_in_b[l])
        wuq_p, wukv_p = _pack_mla_w(w_uq[l], w_ukv[l])
        aq, ak, bq, bk, ik, iq, avt, vt, iwt, qc, kc, vc = _inproj(
            x2, w_p, w_t, tables, c_q_norm_g[l].reshape(1, -1), c_kv_norm_g[l].reshape(1, -1), wuq_p, wukv_p, s, tm)
        oa = _swa(attn_sinks[l], aq, ak, avt, b, s)
        ob = _dsa(bq, iq, iwt, bk, ik, vt, b, s)
        oc = _mla(qc, kc, vc, b, s)
        x1, gates_g = _outproj(x2, oa, ob, oc, w_out[l].astype(BF16), ln1_g[l].reshape(1, d), ln1_b[l].reshape(1, d),
                               wr_t, rbias, alpha, min(4 * OUT_SLAB, n))
        x2 = _moe(x1, gates_g, w_gate_b, w_up_b, w_down_b, l,
                  ln2_g[l].reshape(1, d), ln2_b[l].reshape(1, d), alpha, min(1024, n))
    return x2.reshape(b, s, d)
```
